```python
import math, functools
import jax, jax.numpy as jnp
from jax import lax
import numpy as np

D_MODEL = 4096
BATCH = 1
SEQ = 8192
DEPTH = 2

GRID_W = 64
CTX_LEN = 256
MIX_WIDTH = D_MODEL
N_MOD = 9
NORM_EPS = 1e-6
D_FF = 2 * D_MODEL
FFN_RES = 0.5

RW_WIDTH = MIX_WIDTH // 2
RW_HEAD = 64
RW_HEADS = RW_WIDTH // RW_HEAD
RW_DECAY_LORA = 96
RW_AAA_LORA = 96
RW_GATE_LORA = 256
RW_GN_EPS = 64e-5
RW_DECAY_SCALE = math.exp(-0.5)
RW_COLS = 3 * RW_WIDTH + RW_DECAY_LORA + RW_AAA_LORA + RW_GATE_LORA
RW_SPLITS = (RW_WIDTH, 2 * RW_WIDTH, 3 * RW_WIDTH,
             3 * RW_WIDTH + RW_DECAY_LORA, 3 * RW_WIDTH + RW_DECAY_LORA + RW_AAA_LORA)

MLA_WIDTH = MIX_WIDTH - RW_WIDTH
MLA_V_DIM = 128
MLA_HEADS = MLA_WIDTH // MLA_V_DIM
MLA_NOPE = 128
MLA_ROPE = 64
MLA_QK = MLA_NOPE + MLA_ROPE
MLA_Q_RANK = D_MODEL // 4
MLA_KV_RANK = D_MODEL // 8
MLA_COLS = MLA_Q_RANK + MLA_KV_RANK + MLA_ROPE
MLA_SPLITS = (MLA_Q_RANK, MLA_Q_RANK + MLA_KV_RANK)
ROPE_FREQ = MLA_ROPE // 4
ROPE_THETA = 10000.0
Q_BLOCK = 128

IN_COLS = RW_COLS + MLA_COLS

kernel_name = "hybrid_rwkv7_mla_macaron_dit"


def rms_norm(x, g):
    xf = x.astype(jnp.float32)
    y = xf * lax.rsqrt(jnp.mean(xf * xf, axis=-1, keepdims=True) + NORM_EPS)
    return (y * g.astype(jnp.float32)).astype(x.dtype)


def modulate(x, g, shift, scale):
    return rms_norm(x, g) * (1 + scale[..., None, :]) + shift[..., None, :]


def ffn_half_step(x, m, g, w_in, w_out):
    h = modulate(x, g, m[..., 0, :], m[..., 1, :])
    gate, up = jnp.split(h @ w_in, 2, axis=-1)
    return x + FFN_RES * m[..., 2, :][..., None, :] * ((jax.nn.silu(gate) * up) @ w_out)


def grid_shift(z, rows):
    B, T, C = z.shape
    q = C // 4
    zg = z.reshape(B, rows, GRID_W, C)
    left = jnp.pad(zg[:, :, :-1, :q], ((0, 0), (0, 0), (1, 0), (0, 0)))
    right = jnp.pad(zg[:, :, 1:, q:2 * q], ((0, 0), (0, 0), (0, 1), (0, 0)))
    up = jnp.pad(zg[:, :-1, :, 2 * q:3 * q], ((0, 0), (1, 0), (0, 0), (0, 0)))
    down = jnp.pad(zg[:, 1:, :, 3 * q:], ((0, 0), (0, 1), (0, 0), (0, 0)))
    return jnp.concatenate([left, right, up, down], axis=-1).reshape(B, T, C)


def seq_shift(z):
    h = z.shape[-1] // 2
    prev = jnp.pad(z[:, :-1, :h], ((0, 0), (1, 0), (0, 0)))
    nxt = jnp.pad(z[:, 1:, h:], ((0, 0), (0, 1), (0, 0)))
    return jnp.concatenate([prev, nxt], axis=-1)


def rwkv_streams(z, shift_fn, mu, w0, w_up, a0, a_up, g_up, k_k, k_a):
    z = z.astype(jnp.float32)
    z = z + (shift_fn(z) - z) * mu
    r, k, v, wd, ad, gd = jnp.split(z, RW_SPLITS, axis=-1)
    heads = lambda t: t.reshape(t.shape[:-1] + (RW_HEADS, RW_HEAD))
    kk = heads(k * k_k)
    kk = kk / jnp.maximum(jnp.sqrt(jnp.sum(kk * kk, axis=-1, keepdims=True)), 1e-12)
    w_logit = w0[:, None, None, :] + jnp.einsum('btl,dlc->dbtc', jnp.tanh(wd), w_up)
    w = jnp.exp(-RW_DECAY_SCALE * jax.nn.sigmoid(w_logit))
    a = jax.nn.sigmoid(a0[:, None, None, :] + jnp.einsum('btl,dlc->dbtc', ad, a_up))
    k_dir = k[None] * (1 + (a - 1) * k_a)
    g = jax.nn.sigmoid(gd) @ g_up
    return (heads(r), heads(v), kk, heads(w), heads(a), heads(k_dir), g)


def wkv_scan(r, w, k, v, kk, a, s0, reverse):
    def step(S, inp):
        r_t, w_t, k_t, v_t, kk_t, a_t = inp
        s_kk = jnp.einsum('bhvk,bhk->bhv', S, kk_t)
        S = (S * w_t[:, :, None, :] - s_kk[..., None] * (kk_t * a_t)[:, :, None, :]
             + v_t[..., None] * k_t[:, :, None, :])
        return S, jnp.einsum('bhvk,bhk->bhv', S, r_t)
    xs = tuple(jnp.swapaxes(t, 0, 1) for t in (r, w, k, v, kk, a))
    S, ys = lax.scan(step, s0, xs, reverse=reverse)
    return jnp.swapaxes(ys, 0, 1), S


def rwkv_bidirectional(st, s0_f, s0_b):
    r, v, kk, w, a, k_dir, g = st
    y_f, s_f = wkv_scan(r, w[0], k_dir[0], v, kk, a[0], s0_f, False)
    y_b, s_b = wkv_scan(r, w[1], k_dir[1], v, kk, a[1], s0_b, True)
    return y_f + y_b, s_f, s_b


def rwkv_output(y, st, r_k, ln_w, ln_b, dtype):
    r, v, kk, w, a, k_dir, g = st
    B, T = y.shape[:2]
    mean = jnp.mean(y, axis=-1, keepdims=True)
    var = jnp.mean(jnp.square(y - mean), axis=-1, keepdims=True)
    o = ((y - mean) * lax.rsqrt(var + RW_GN_EPS)).reshape(B, T, RW_WIDTH) * ln_w + ln_b
    bonus = jnp.sum(r * jnp.mean(k_dir, axis=0) * r_k, axis=-1, keepdims=True) * v
    o = o + bonus.reshape(B, T, RW_WIDTH)
    return (o * g).astype(dtype)


def axial_rope_tables(T):
    t = jnp.arange(T, dtype=jnp.int32)
    pos = jnp.stack([t // GRID_W, t % GRID_W], axis=-1).astype(jnp.float32)
    inv_freq = ROPE_THETA ** (-jnp.arange(ROPE_FREQ, dtype=jnp.float32) / ROPE_FREQ)
    ang = pos[:, :, None] * inv_freq
    return jnp.cos(ang), jnp.sin(ang)


def apply_axial_rope(x, cos, sin):
    x_nope, x_rot = x[..., :MLA_NOPE], x[..., MLA_NOPE:]
    xr = x_rot.reshape(x_rot.shape[:-1] + (2, 2, ROPE_FREQ)).astype(jnp.float32)
    x1, x2 = xr[..., 0, :], xr[..., 1, :]
    cs, sn = cos[None, :, None], sin[None, :, None]
    rot = jnp.stack([x1 * cs - x2 * sn, x1 * sn + x2 * cs], axis=-2).reshape(x_rot.shape)
    return jnp.concatenate([x_nope, rot.astype(x.dtype)], axis=-1)


def mla_qkv(z, rope, q_norm, q_up, kv_norm, kv_up, q_gain, k_gain):
    B, T, _ = z.shape
    q_lat, kv_lat, k_rope = jnp.split(z, MLA_SPLITS, axis=-1)
    q = (rms_norm(q_lat, q_norm) @ q_up).reshape(B, T, MLA_HEADS, MLA_QK)
    kv = (rms_norm(kv_lat, kv_norm) @ kv_up).reshape(B, T, MLA_HEADS, MLA_NOPE + MLA_V_DIM)
    k_nope, v = jnp.split(kv, [MLA_NOPE], axis=-1)
    k = jnp.concatenate(
        [k_nope, jnp.broadcast_to(k_rope[:, :, None, :], (B, T, MLA_HEADS, MLA_ROPE))], axis=-1)
    q = rms_norm(q, q_gain)
    k = rms_norm(k, k_gain)
    if rope is not None:
        q = apply_axial_rope(q, *rope)
        k = apply_axial_rope(k, *rope)
    return q, k, v


def latent_attention(q, k, v, k_ctx, v_ctx):
    B, S, H, Dq = q.shape
    k_all = jnp.concatenate([k, k_ctx], axis=1)
    v_all = jnp.concatenate([v, v_ctx], axis=1)
    qb = jnp.moveaxis(q.reshape(B, S // Q_BLOCK, Q_BLOCK, H, Dq), 1, 0)
    scale = 1.0 / math.sqrt(Dq)

    def block(q_blk):
        s = jnp.einsum('bqhd,bkhd->bhqk', q_blk, k_all).astype(jnp.float32) * scale
        p = jax.nn.softmax(s, axis=-1).astype(v_all.dtype)
        return jnp.einsum('bhqk,bkhd->bqhd', p, v_all)

    o = lax.map(block, qb)
    return jnp.moveaxis(o, 0, 1).reshape(B, S, H * MLA_V_DIM)


def context_attention(q, k, v):
    B, T, H, Dq = q.shape
    s = jnp.einsum('bqhd,bkhd->bhqk', q, k).astype(jnp.float32) * (1.0 / math.sqrt(Dq))
    p = jax.nn.softmax(s, axis=-1).astype(v.dtype)
    return jnp.einsum('bhqk,bkhd->bqhd', p, v).reshape(B, T, H * MLA_V_DIM)


def setup_inputs(seed: int = 0) -> dict:
    key = jax.random.key(seed)
    ks = iter(jax.random.split(key, 32))
    nrm = lambda shape, scale: jax.random.normal(next(ks), shape, jnp.float32) * scale
    L, D = DEPTH, D_MODEL
    return {
        "x": nrm((BATCH, SEQ, D), 1.0),
        "c": nrm((BATCH, D), 1.0),
        "ctx": nrm((BATCH, CTX_LEN, D), 1.0),
        "c_ctx": nrm((D,), 1.0),
        "w_mod": nrm((L, D, N_MOD * D), 0.5 * D ** -0.5),
        "b_mod": nrm((L, N_MOD * D), 0.02),
        "norm_g": 1.0 + nrm((L, 3, D), 0.05),
        "ffn_w_in": nrm((L, 2, D, 2 * D_FF), D ** -0.5),
        "ffn_w_out": nrm((L, 2, D_FF, D), D_FF ** -0.5),
        "w_in": nrm((L, D, IN_COLS), D ** -0.5),
        "w_out": nrm((L, MIX_WIDTH, D), MIX_WIDTH ** -0.5),
        "rw_mu": jax.random.uniform(next(ks), (L, RW_COLS), jnp.float32, 0.0, 1.0),
        "rw_w0": nrm((L, 2, RW_WIDTH), 1.0),
        "rw_w_up": nrm((L, 2, RW_DECAY_LORA, RW_WIDTH), RW_DECAY_LORA ** -0.5),
        "rw_a0": nrm((L, 2, RW_WIDTH), 0.5),
        "rw_a_up": nrm((L, 2, RW_AAA_LORA, RW_WIDTH), RW_AAA_LORA ** -0.5),
        "rw_g_up": nrm((L, RW_GATE_LORA, RW_WIDTH), RW_GATE_LORA ** -0.5),
        "rw_k_k": 0.85 + nrm((L, RW_WIDTH), 0.05),
        "rw_k_a": 1.0 + nrm((L, RW_WIDTH), 0.05),
        "rw_r_k": nrm((L, RW_HEADS, RW_HEAD), 0.1),
        "rw_ln_w": 1.0 + nrm((L, RW_WIDTH), 0.05),
        "rw_ln_b": nrm((L, RW_WIDTH), 0.02),
        "mla_q_norm": 1.0 + nrm((L, MLA_Q_RANK), 0.05),
        "mla_q_up": nrm((L, MLA_Q_RANK, MLA_HEADS * MLA_QK), MLA_Q_RANK ** -0.5),
        "mla_kv_norm": 1.0 + nrm((L, MLA_KV_RANK), 0.05),
        "mla_kv_up": nrm((L, MLA_KV_RANK, MLA_HEADS * (MLA_NOPE + MLA_V_DIM)), MLA_KV_RANK ** -0.5),
        "mla_q_gain": 1.0 + nrm((L, MLA_QK), 0.05),
        "mla_k_gain": 1.0 + nrm((L, MLA_QK), 0.05),
    }


def reference(x, c, ctx, c_ctx, w_mod, b_mod, norm_g, ffn_w_in, ffn_w_out, w_in, w_out,
              rw_mu, rw_w0, rw_w_up, rw_a0, rw_a_up, rw_g_up, rw_k_k, rw_k_a, rw_r_k,
              rw_ln_w, rw_ln_b, mla_q_norm, mla_q_up, mla_kv_norm, mla_kv_up,
              mla_q_gain, mla_k_gain):
    B, S, _ = x.shape
    rows = S // GRID_W
    lat_shift = functools.partial(grid_shift, rows=rows)
    rope = axial_rope_tables(S)
    s_zero = jnp.zeros((ctx.shape[0], RW_HEADS, RW_HEAD, RW_HEAD), jnp.float32)
    xc = ctx
    for l in range(DEPTH):
        last = l == DEPTH - 1
        mod_x = (jax.nn.silu(c) @ w_mod[l] + b_mod[l]).reshape(B, N_MOD, D_MODEL)
        mod_c = (jax.nn.silu(c_ctx) @ w_mod[l] + b_mod[l]).reshape(N_MOD, D_MODEL)

        x = ffn_half_step(x, mod_x[..., 0:3, :], norm_g[l, 0], ffn_w_in[l, 0], ffn_w_out[l, 0])
        xc = ffn_half_step(xc, mod_c[..., 0:3, :], norm_g[l, 0], ffn_w_in[l, 0], ffn_w_out[l, 0])

        z_x = modulate(x, norm_g[l, 1], mod_x[..., 3, :], mod_x[..., 4, :]) @ w_in[l]
        z_c = modulate(xc, norm_g[l, 1], mod_c[..., 3, :], mod_c[..., 4, :]) @ w_in[l]

        rw_p = (rw_mu[l], rw_w0[l], rw_w_up[l], rw_a0[l], rw_a_up[l], rw_g_up[l], rw_k_k[l], rw_k_a[l])
        st_c = rwkv_streams(z_c[..., :RW_COLS], seq_shift, *rw_p)
        st_x = rwkv_streams(z_x[..., :RW_COLS], lat_shift, *rw_p)
        y_c, s_f, s_b = rwkv_bidirectional(st_c, s_zero, s_zero)
        y_x, _, _ = rwkv_bidirectional(st_x, s_f, s_b)
        o_rw_x = rwkv_output(y_x, st_x, rw_r_k[l], rw_ln_w[l], rw_ln_b[l], x.dtype)

        mla_p = (mla_q_norm[l], mla_q_up[l], mla_kv_norm[l], mla_kv_up[l], mla_q_gain[l], mla_k_gain[l])
        q_c, k_c, v_c = mla_qkv(z_c[..., RW_COLS:], None, *mla_p)
        q_x, k_x, v_x = mla_qkv(z_x[..., RW_COLS:], rope, *mla_p)
        o_mla_x = latent_attention(q_x, k_x, v_x, k_c, v_c)

        x = x + mod_x[..., 5, :][..., None, :] * (
            jnp.concatenate([o_rw_x, o_mla_x], axis=-1) @ w_out[l])
        if not last:
            o_rw_c = rwkv_output(y_c, st_c, rw_r_k[l], rw_ln_w[l], rw_ln_b[l], xc.dtype)
            o_mla_c = context_attention(q_c, k_c, v_c)
            xc = xc + mod_c[..., 5, :][..., None, :] * (
                jnp.concatenate([o_rw_c, o_mla_c], axis=-1) @ w_out[l])

        x = ffn_half_step(x, mod_x[..., 6:9, :], norm_g[l, 2], ffn_w_in[l, 1], ffn_w_out[l, 1])
        if not last:
            xc = ffn_half_step(xc, mod_c[..., 6:9, :], norm_g[l, 2], ffn_w_in[l, 1], ffn_w_out[l, 1])
    return x
```

```python
import functools
import math

import numpy as np
import jax
import jax.numpy as jnp
from jax import lax
from jax.experimental import pallas as pl
from jax.experimental.pallas import tpu as pltpu

F32 = jnp.float32
BF16 = jnp.bfloat16
HI = lax.Precision.HIGHEST

GRID_W = 64
NORM_EPS = 1e-6
FFN_RES = 0.5
RW_HEAD = 64
RW_GN_EPS = 64e-5
RW_DECAY_SCALE = math.exp(-0.5)
MLA_V_DIM = 128
MLA_NOPE = 128
MLA_ROPE = 64
MLA_QK = MLA_NOPE + MLA_ROPE
MLA_HEAD_PAD = 256
ROPE_FREQ = MLA_ROPE // 4
ROPE_THETA = 10000.0
N_MOD = 9
LANE = 128
CHUNK = 64
VMEM_LIMIT = 60 * 1024 * 1024


def _pick(n, candidates):
    for c in candidates:
        if n % c == 0:
            return c
    raise ValueError(f"no block size for {n} among {candidates}")


def _params(sem):
    return pltpu.CompilerParams(dimension_semantics=sem, vmem_limit_bytes=VMEM_LIMIT)


def _dot(a, b, precision=None):
    return jnp.dot(a, b, preferred_element_type=F32, precision=precision)


def _dot_nt(a, b, precision=None):
    return lax.dot_general(a, b, (((1,), (1,)), ((), ())), preferred_element_type=F32,
                           precision=precision)


def _dot_tn(a, b, precision=None):
    return lax.dot_general(a, b, (((0,), (0,)), ((), ())), preferred_element_type=F32,
                           precision=precision)


def _row_select(row0, n_rows, n_lat, ref):
    rows = row0 + lax.broadcasted_iota(jnp.int32, (n_rows, 1), 0)
    return jnp.where(rows < n_lat, ref[0:1, :], ref[1:2, :])


def _mod_kernel(c_ref, w_ref, b_ref, o_ref):
    c = c_ref[...]
    a = (c * jax.nn.sigmoid(c)).astype(BF16)
    o_ref[...] = _dot(a, w_ref[...].astype(BF16)) + b_ref[...]


def _modulation(cc, w_mod, b_mod):
    d, n = w_mod.shape
    bn = _pick(n, (512, 256, 128))
    return pl.pallas_call(
        _mod_kernel,
        out_shape=jax.ShapeDtypeStruct((8, n), F32),
        grid=(n // bn,),
        in_specs=[pl.BlockSpec((8, d), lambda j: (0, 0)),
                  pl.BlockSpec((d, bn), lambda j: (0, j)),
                  pl.BlockSpec((1, bn), lambda j: (0, j))],
        out_specs=pl.BlockSpec((8, bn), lambda j: (0, j)),
        compiler_params=_params(("parallel",)),
        name="modulation",
    )(cc, w_mod, b_mod.reshape(1, n))


def _norm_mod_kernel(x_ref, g_ref, sh_ref, sc_ref, o_ref, *, n_lat, bm):
    x = x_ref[...]
    y = x * lax.rsqrt(jnp.mean(x * x, axis=-1, keepdims=True) + NORM_EPS) * g_ref[...]
    row0 = pl.program_id(0) * bm
    sc = _row_select(row0, bm, n_lat, sc_ref)
    sh = _row_select(row0, bm, n_lat, sh_ref)
    o_ref[...] = (y * (1.0 + sc) + sh).astype(o_ref.dtype)


def _norm_mod(x, g, shift, scale, n_lat):
    m, d = x.shape
    bm = _pick(m, (384, 256, 128, 64, 32, 16))
    return pl.pallas_call(
        functools.partial(_norm_mod_kernel, n_lat=n_lat, bm=bm),
        out_shape=jax.ShapeDtypeStruct((m, d), BF16),
        grid=(m // bm,),
        in_specs=[pl.BlockSpec((bm, d), lambda i: (i, 0)),
                  pl.BlockSpec((1, d), lambda i: (0, 0)),
                  pl.BlockSpec((2, d), lambda i: (0, 0)),
                  pl.BlockSpec((2, d), lambda i: (0, 0))],
        out_specs=pl.BlockSpec((bm, d), lambda i: (i, 0)),
        compiler_params=_params(("parallel",)),
        name="norm_mod",
    )(x, g.reshape(1, d), shift, scale)


def _mm_kernel(a_ref, w_ref, o_ref):
    o_ref[...] = _dot(a_ref[...], w_ref[...]).astype(o_ref.dtype)


def _matmul(a, w, out_dtype=F32, bn_candidates=(1024, 512, 256, 128)):
    m, k = a.shape
    n = w.shape[1]
    bm = _pick(m, (1056, 1024, 512, 320, 256, 128, 64))
    bn = _pick(n, bn_candidates)
    return pl.pallas_call(
        _mm_kernel,
        out_shape=jax.ShapeDtypeStruct((m, n), out_dtype),
        grid=(m // bm, n // bn),
        in_specs=[pl.BlockSpec((bm, k), lambda i, j: (i, 0)),
                  pl.BlockSpec((k, bn), lambda i, j: (0, j))],
        out_specs=pl.BlockSpec((bm, bn), lambda i, j: (i, j)),
        compiler_params=_params(("parallel", "parallel")),
        name="matmul",
    )(a, w)


def _swiglu_kernel(a_ref, wg_ref, wu_ref, o_ref):
    a = a_ref[...]
    gate = _dot(a, wg_ref[...])
    up = _dot(a, wu_ref[...])
    o_ref[...] = (gate * jax.nn.sigmoid(gate) * up).astype(o_ref.dtype)


def _swiglu_matmul(a, w):
    m, k = a.shape
    f = w.shape[1] // 2
    bm = _pick(m, (1056, 1024, 512, 320, 256, 128, 64))
    bn = _pick(f, (512, 256, 128))
    nb = f // bn
    return pl.pallas_call(
        _swiglu_kernel,
        out_shape=jax.ShapeDtypeStruct((m, f), BF16),
        grid=(m // bm, nb),
        in_specs=[pl.BlockSpec((bm, k), lambda i, j: (i, 0)),
                  pl.BlockSpec((k, bn), lambda i, j: (0, j)),
                  pl.BlockSpec((k, bn), lambda i, j: (0, j + nb))],
        out_specs=pl.BlockSpec((bm, bn), lambda i, j: (i, j)),
        compiler_params=_params(("parallel", "parallel")),
        name="swiglu_matmul",
    )(a, w, w)


def _residual_kernel(*refs, n_a, n_lat, bm):
    a_refs = refs[:n_a]
    w_refs = refs[n_a:2 * n_a]
    x_ref, g_ref, o_ref = refs[2 * n_a:]
    acc = _dot(a_refs[0][...], w_refs[0][...])
    for a_ref, w_ref in zip(a_refs[1:], w_refs[1:]):
        acc = acc + _dot(a_ref[...], w_ref[...])
    gate = _row_select(pl.program_id(0) * bm, bm, n_lat, g_ref)
    o_ref[...] = x_ref[...] + gate * acc


def _residual_matmul(a_list, w, x, gate, n_lat):
    m, n = x.shape
    n_a = len(a_list)
    k = a_list[0].shape[1]
    assert all(a.shape[1] == k for a in a_list) and w.shape[0] == n_a * k
    bm = _pick(m, (1056, 1024, 512, 320, 256, 128, 64))
    bn = _pick(n, (512, 256, 128))
    a_specs = [pl.BlockSpec((bm, k), lambda i, j: (i, 0)) for _ in a_list]
    w_specs = [pl.BlockSpec((k, bn), functools.partial(lambda i, j, p: (p, j), p=p))
               for p in range(n_a)]
    return pl.pallas_call(
        functools.partial(_residual_kernel, n_a=n_a, n_lat=n_lat, bm=bm),
        out_shape=jax.ShapeDtypeStruct((m, n), F32),
        grid=(m // bm, n // bn),
        in_specs=a_specs + w_specs + [pl.BlockSpec((bm, bn), lambda i, j: (i, j)),
                                      pl.BlockSpec((2, bn), lambda i, j: (0, j))],
        out_specs=pl.BlockSpec((bm, bn), lambda i, j: (i, j)),
        compiler_params=_params(("parallel", "parallel")),
        name="residual_matmul",
    )(*a_list, *([w] * n_a), x, gate)


def _segsum(x, bd):
    parts = [_dot(x[:, j:j + LANE], bd, precision=HI) for j in range(0, x.shape[1], LANE)]
    return jnp.concatenate(parts, axis=1)


def _streams_kernel(zc_ref, zp_ref, zn_ref, mu_ref, code_ref, kkw_ref, ka_ref, rk_ref, w0_ref,
                    a0_ref, wup_ref, aup_ref, gup_ref, bd_ref,
                    r_out, v_out, kk_out, lw0_out, lw1_out, b0_out, b1_out, kd0_out, kd1_out,
                    g_out, bonus_out, *, n_lat_blocks, n_blocks, hw, lg):
    i = pl.program_id(0)
    z = zc_ref[...]
    zp = zp_ref[...]
    zn = zn_ref[...]
    code = code_ref[...]
    rows = lax.broadcasted_iota(jnp.int32, (GRID_W, 1), 0)
    first = rows == 0
    last = rows == GRID_W - 1
    prev_tok = pltpu.roll(z, 1, 0)
    next_tok = pltpu.roll(z, GRID_W - 1, 0)

    left = jnp.where(first, 0.0, prev_tok)
    right = jnp.where(last, 0.0, next_tok)
    up = zp * (i > 0).astype(F32)
    down = zn * (i < n_lat_blocks - 1).astype(F32)
    sh_lat = jnp.where(code == 0, left, jnp.where(code == 1, right, jnp.where(code == 2, up, down)))
    has_prev = (i > n_lat_blocks).astype(F32)
    has_next = (i < n_blocks - 1).astype(F32)
    prev_c = jnp.where(first, zp[GRID_W - 1:GRID_W, :] * has_prev, prev_tok)
    next_c = jnp.where(last, zn[0:1, :] * has_next, next_tok)
    sh_ctx = jnp.where(code < 2, prev_c, next_c)
    is_ctx = (i >= n_lat_blocks).astype(F32)
    shifted = sh_lat + (sh_ctx - sh_lat) * is_ctx
    zs = z + (shifted - z) * mu_ref[...]

    r = zs[:, 0:hw]
    k = zs[:, hw:2 * hw]
    v = zs[:, 2 * hw:3 * hw]
    wd = zs[:, 3 * hw:3 * hw + LANE]
    ad = zs[:, 3 * hw + LANE:3 * hw + 2 * LANE]
    gd = zs[:, 3 * hw + 2 * LANE:3 * hw + 2 * LANE + lg]
    bd = bd_ref[...]

    kk = k * kkw_ref[...]
    kk = kk / jnp.maximum(jnp.sqrt(_segsum(kk * kk, bd)), 1e-12)
    tw = jnp.tanh(wd).astype(BF16)
    adb = ad.astype(BF16)
    ka = ka_ref[...]
    kds = []
    for d, (lw_out, b_out, kd_out) in enumerate(((lw0_out, b0_out, kd0_out),
                                                 (lw1_out, b1_out, kd1_out))):
        w_logit = w0_ref[d:d + 1, :] + _dot(tw, wup_ref[d])
        lw_out[...] = -RW_DECAY_SCALE * jax.nn.sigmoid(w_logit)
        a = jax.nn.sigmoid(a0_ref[d:d + 1, :] + _dot(adb, aup_ref[d]))
        kd = k * (1.0 + (a - 1.0) * ka)
        kd_out[...] = kd
        b_out[...] = kk * a
        kds.append(kd)
    g_out[...] = _dot(jax.nn.sigmoid(gd).astype(BF16), gup_ref[...])
    kmean = (kds[0] + kds[1]) / 2.0
    bonus_out[...] = _segsum(r * kmean * rk_ref[...], bd) * v
    r_out[...] = r
    v_out[...] = v
    kk_out[...] = kk


def _rw_streams(z_rw, p, n_lat):
    t, zw = z_rw.shape
    hw = p["hw"]
    lg = p["gup"].shape[0]
    n_blocks = t // GRID_W
    n_lat_blocks = n_lat // GRID_W
    row = lambda f: pl.BlockSpec((GRID_W, zw), f)
    const2 = lambda shape: pl.BlockSpec(shape, lambda i: (0, 0))
    const3 = lambda shape: pl.BlockSpec(shape, lambda i: (0, 0, 0))
    out_spec = pl.BlockSpec((GRID_W, hw), lambda i: (i, 0))
    outs = pl.pallas_call(
        functools.partial(_streams_kernel, n_lat_blocks=n_lat_blocks, n_blocks=n_blocks, hw=hw,
                          lg=lg),
        out_shape=[jax.ShapeDtypeStruct((t, hw), F32)] * 11,
        grid=(n_blocks,),
        in_specs=[row(lambda i: (i, 0)),
                  row(lambda i: (jnp.maximum(i - 1, 0), 0)),
                  row(lambda i: (jnp.minimum(i + 1, n_blocks - 1), 0)),
                  const2((1, zw)), const2((1, zw)),
                  const2((1, hw)), const2((1, hw)), const2((1, hw)),
                  const2((2, hw)), const2((2, hw)),
                  const3((2, LANE, hw)), const3((2, LANE, hw)), const2((lg, hw)),
                  const2((LANE, LANE))],
        out_specs=[out_spec] * 11,
        compiler_params=_params(("parallel",)),
        name="rw_streams",
    )(z_rw, z_rw, z_rw, p["mu"], p["code"], p["k_k"], p["k_a"], p["r_k"], p["w0"], p["a0"],
      p["wup"], p["aup"], p["gup"], p["bd"])
    names = ("r", "v", "kk", "lw0", "lw1", "b0", "b1", "kd0", "kd1", "g", "bonus")
    return dict(zip(names, outs))


def _wkv_kernel(lw_ref, kk_ref, b_ref, kd_ref, r_ref, v_ref, y_ref, st_ref, *, heads, tb, reverse):
    @pl.when(pl.program_id(1) == 0)
    def _():
        st_ref[...] = jnp.zeros(st_ref.shape, F32)

    n_chunks = tb // CHUNK
    t_idx = lax.broadcasted_iota(jnp.int32, (CHUNK, CHUNK), 0)
    s_idx = lax.broadcasted_iota(jnp.int32, (CHUNK, CHUNK), 1)
    strict = (s_idx > t_idx) if reverse else (s_idx < t_idx)
    incl = (s_idx >= t_idx) if reverse else (s_idx <= t_idx)
    tri = incl.astype(F32)
    eye = (s_idx == t_idx).astype(F32)

    def chunk(ci, carry):
        cc = (n_chunks - 1 - ci) if reverse else ci
        rows = pl.ds(pl.multiple_of(cc * CHUNK, CHUNK), CHUNK)
        lw = lw_ref[rows, :]
        kk = kk_ref[rows, :]
        b = b_ref[rows, :]
        kd = kd_ref[rows, :]
        r = r_ref[rows, :]
        v = v_ref[rows, :]
        cum = _dot(tri, lw, precision=HI)
        tot = jnp.sum(lw, axis=0, keepdims=True)
        e_neg = jnp.exp(-cum)
        e_end = jnp.exp(tot - cum)
        alpha = kk * jnp.exp(cum - lw)
        rho = r * jnp.exp(cum)
        beta = b * e_neg
        kappa = kd * e_neg
        beta_e = b * e_end
        kappa_e = kd * e_end
        g_end = jnp.exp(tot)
        ys = []
        for h in range(heads):
            sl = slice(h * RW_HEAD, (h + 1) * RW_HEAD)
            al, rh, vv = alpha[:, sl], rho[:, sl], v[:, sl]
            m = _dot_nt(jnp.concatenate([al, rh], axis=0),
                        jnp.concatenate([beta[:, sl], kappa[:, sl]], axis=0), precision=HI)
            l_ab = jnp.where(strict, m[:CHUNK, :CHUNK], 0.0)
            m_ak = jnp.where(strict, m[:CHUNK, CHUNK:], 0.0)
            m_rb = jnp.where(incl, m[CHUNK:, :CHUNK], 0.0)
            m_rk = jnp.where(incl, m[CHUNK:, CHUNK:], 0.0)
            pw = -l_ab
            t_inv = eye + pw
            for _ in range(int(math.log2(CHUNK)) - 1):
                pw = _dot(pw, pw, precision=HI)
                t_inv = t_inv + _dot(t_inv, pw, precision=HI)
            w12 = _dot(t_inv, jnp.concatenate([_dot(m_ak, vv, precision=HI), al], axis=1),
                       precision=HI)
            mw = _dot(m_rb, w12, precision=HI)
            y1 = _dot(m_rk, vv, precision=HI) - mw[:, :RW_HEAD]
            r2 = rh - mw[:, RW_HEAD:]
            bw = _dot_tn(beta_e[:, sl], w12, precision=HI)
            g_m = _dot_tn(kappa_e[:, sl], vv, precision=HI) - bw[:, :RW_HEAD]
            p_m = eye * g_end[:, sl] - bw[:, RW_HEAD:]
            st = st_ref[h]
            ys.append(y1 + _dot(r2, st, precision=HI))
            st_ref[h] = _dot(p_m, st, precision=HI) + g_m
        y_ref[rows, :] = jnp.concatenate(ys, axis=1)
        return carry

    lax.fori_loop(0, n_chunks, chunk, 0)


def _wkv(st, d, n_lat, reverse):
    lw, b, kd = st[f"lw{d}"], st[f"b{d}"], st[f"kd{d}"]
    t, hw = lw.shape
    heads = 2
    tb = _pick(math.gcd(n_lat, t - n_lat), (256, 128, 64))
    nb = t // tb
    n_lat_blocks = n_lat // tb
    if reverse:
        blk = lambda h, c: (nb - 1 - c, h)
    else:
        blk = lambda h, c: ((c + n_lat_blocks) % nb, h)
    spec = pl.BlockSpec((tb, heads * RW_HEAD), blk)
    return pl.pallas_call(
        functools.partial(_wkv_kernel, heads=heads, tb=tb, reverse=reverse),
        out_shape=jax.ShapeDtypeStruct((t, hw), F32),
        grid=(hw // (heads * RW_HEAD), nb),
        in_specs=[spec] * 6,
        out_specs=spec,
        scratch_shapes=[pltpu.VMEM((heads, RW_HEAD, RW_HEAD), F32)],
        compiler_params=_params(("parallel", "arbitrary")),
        name="wkv_bwd" if reverse else "wkv_fwd",
    )(lw, st["kk"], b, kd, st["r"], st["v"])


def _rw_out_kernel(yf_ref, yb_ref, bonus_ref, g_ref, lnw_ref, lnb_ref, bd_ref, o_ref):
    y = yf_ref[...] + yb_ref[...]
    bd = bd_ref[...]
    mean = _segsum(y, bd) / RW_HEAD
    yc = y - mean
    var = _segsum(yc * yc, bd) / RW_HEAD
    o = yc * lax.rsqrt(var + RW_GN_EPS) * lnw_ref[...] + lnb_ref[...] + bonus_ref[...]
    o_ref[...] = (o * g_ref[...]).astype(o_ref.dtype)


def _rw_output(y_f, y_b, st, p):
    t, hw = y_f.shape
    bm = _pick(t, (256, 128, 64))
    row = pl.BlockSpec((bm, hw), lambda i: (i, 0))
    vec = pl.BlockSpec((1, hw), lambda i: (0, 0))
    return pl.pallas_call(
        _rw_out_kernel,
        out_shape=jax.ShapeDtypeStruct((t, hw), BF16),
        grid=(t // bm,),
        in_specs=[row, row, row, row, vec, vec, pl.BlockSpec((LANE, LANE), lambda i: (0, 0))],
        out_specs=row,
        compiler_params=_params(("parallel",)),
        name="rw_output",
    )(y_f, y_b, st["bonus"], st["g"], p["ln_w"], p["ln_b"], p["bd"])


def _mla_prep_kernel(z_ref, qn_ref, qup_ref, kvn_ref, kvup_ref, qg_ref, kg_ref, c_ref, s1_ref,
                     s2_ref, q_out, k_out, v_out, *, q_rank, kv_rank, heads):
    z = z_ref[...]

    def rms(x, g):
        return x * lax.rsqrt(jnp.mean(x * x, axis=-1, keepdims=True) + NORM_EPS) * g

    q = _dot(rms(z[:, :q_rank], qn_ref[...]).astype(BF16), qup_ref[...])
    kv = _dot(rms(z[:, q_rank:q_rank + kv_rank], kvn_ref[...]).astype(BF16), kvup_ref[...])
    k_rope = z[:, q_rank + kv_rank:q_rank + kv_rank + LANE]
    cos, s1, s2 = c_ref[...], s1_ref[...], s2_ref[...]

    def head_norm_rope(x, g):
        x = x * lax.rsqrt(jnp.sum(x * x, axis=-1, keepdims=True) / MLA_QK + NORM_EPS) * g
        return (x * cos + pltpu.roll(x, MLA_HEAD_PAD - ROPE_FREQ, 1) * s1
                + pltpu.roll(x, ROPE_FREQ, 1) * s2)

    for h in range(heads):
        cols = slice(h * MLA_HEAD_PAD, (h + 1) * MLA_HEAD_PAD)
        q_out[:, cols] = head_norm_rope(q[:, cols], qg_ref[...]).astype(BF16)
        kh = jnp.concatenate([kv[:, h * MLA_NOPE:(h + 1) * MLA_NOPE], k_rope], axis=1)
        k_out[:, cols] = head_norm_rope(kh, kg_ref[...]).astype(BF16)
    v_out[...] = kv[:, heads * MLA_NOPE:].astype(BF16)


def _mla_prep(z_mla, p, rope):
    t, zw = z_mla.shape
    heads = p["heads"]
    q_rank, kv_rank = p["q_up"].shape[0], p["kv_up"].shape[0]
    bm = _pick(t, (256, 128, 64))
    qw = heads * MLA_HEAD_PAD
    row = lambda w: pl.BlockSpec((bm, w), lambda i: (i, 0))
    const = lambda a: pl.BlockSpec(a.shape, lambda i: (0, 0))
    args = (z_mla, p["q_norm"], p["q_up"], p["kv_norm"], p["kv_up"], p["q_gain"], p["k_gain"])
    return pl.pallas_call(
        functools.partial(_mla_prep_kernel, q_rank=q_rank, kv_rank=kv_rank, heads=heads),
        out_shape=[jax.ShapeDtypeStruct((t, qw), BF16), jax.ShapeDtypeStruct((t, qw), BF16),
                   jax.ShapeDtypeStruct((t, heads * MLA_V_DIM), BF16)],
        grid=(t // bm,),
        in_specs=[row(zw)] + [const(a) for a in args[1:]] + [row(MLA_HEAD_PAD)] * 3,
        out_specs=[row(qw), row(qw), row(heads * MLA_V_DIM)],
        compiler_params=_params(("parallel",)),
        name="mla_prep",
    )(*args, *rope)


def _attn_kernel(q_ref, k_ref, v_ref, o_ref, *, n_lat, bq, scale, with_ctx):
    def attend(k, v):
        s = _dot_nt(q_ref[...], k) * scale
        p = jnp.exp(s - jnp.max(s, axis=-1, keepdims=True))
        denom = jnp.sum(p, axis=-1, keepdims=True)
        o_ref[...] = (_dot(p.astype(BF16), v) / denom).astype(o_ref.dtype)

    if not with_ctx:
        attend(k_ref[...], v_ref[...])
        return
    is_latent = pl.program_id(1) < n_lat // bq

    @pl.when(is_latent)
    def _():
        attend(k_ref[...], v_ref[...])

    @pl.when(jnp.logical_not(is_latent))
    def _():
        attend(k_ref[n_lat:, :], v_ref[n_lat:, :])


def _attention(q, k, v, heads, n_lat, with_ctx):
    t = k.shape[0]
    n_q = t if with_ctx else n_lat
    bq = _pick(math.gcd(n_lat, t - n_lat), (256, 128, 64))
    return pl.pallas_call(
        functools.partial(_attn_kernel, n_lat=n_lat, bq=bq, scale=1.0 / math.sqrt(MLA_QK),
                          with_ctx=with_ctx),
        out_shape=jax.ShapeDtypeStruct((n_q, heads * MLA_V_DIM), BF16),
        grid=(heads, n_q // bq),
        in_specs=[pl.BlockSpec((bq, MLA_HEAD_PAD), lambda h, i: (i, h)),
                  pl.BlockSpec((t, MLA_HEAD_PAD), lambda h, i: (0, h)),
                  pl.BlockSpec((t, MLA_V_DIM), lambda h, i: (0, h))],
        out_specs=pl.BlockSpec((bq, MLA_V_DIM), lambda h, i: (i, h)),
        compiler_params=_params(("parallel", "parallel")),
        name="attention",
    )(q, k, v)


def _pad_cols(w, width):
    return jnp.pad(w, ((0, 0), (0, width - w.shape[1])))


def _rope_tables(n_lat, n_ctx):
    t = jnp.arange(n_lat, dtype=jnp.int32)
    pos = jnp.stack([t // GRID_W, t % GRID_W], axis=-1).astype(F32)
    inv_freq = ROPE_THETA ** (-jnp.arange(ROPE_FREQ, dtype=F32) / ROPE_FREQ)
    ang = pos[:, :, None] * inv_freq
    cos, sin = jnp.cos(ang), jnp.sin(ang)
    zero = jnp.zeros_like(sin)
    lay = lambda first, second: jnp.stack([first, second], axis=2).reshape(n_lat, MLA_ROPE)
    tabs = []
    for rope_part, fill in ((lay(cos, cos), 1.0), (lay(-sin, zero), 0.0), (lay(zero, sin), 0.0)):
        tab = jnp.concatenate([jnp.full((n_lat, MLA_NOPE), fill, F32), rope_part,
                               jnp.full((n_lat, MLA_HEAD_PAD - MLA_QK), fill, F32)], axis=1)
        tabs.append(jnp.concatenate([tab, jnp.full((n_ctx, MLA_HEAD_PAD), fill, F32)], axis=0))
    return tuple(tabs)


def _layer_params(l, w_in, rw_mu, rw_w0, rw_w_up, rw_a0, rw_a_up, rw_g_up, rw_k_k, rw_k_a, rw_r_k,
                  rw_ln_w, rw_ln_b, mla_q_norm, mla_q_up, mla_kv_norm, mla_kv_up, mla_q_gain,
                  mla_k_gain):
    hw = rw_k_k.shape[1]
    ld, la, lg = rw_w_up.shape[2], rw_a_up.shape[2], rw_g_up.shape[1]
    q_rank, kv_rank = mla_q_up.shape[1], mla_kv_up.shape[1]
    heads = mla_kv_up.shape[2] // (MLA_NOPE + MLA_V_DIM)
    rw_cols = 3 * hw + ld + la + lg
    assert ld <= LANE and la <= LANE and lg % LANE == 0 and rw_cols % 4 == 0

    def rw_layout(a, fill=0.0):
        o = 3 * hw
        pad = lambda x: jnp.pad(x, ((0, 0), (0, LANE - x.shape[1])), constant_values=fill)
        return jnp.concatenate([a[:, :o], pad(a[:, o:o + ld]), pad(a[:, o + ld:o + ld + la]),
                                a[:, o + ld + la:rw_cols]], axis=1)

    w = w_in[l]
    mla_w = _pad_cols(w[:, rw_cols:], -(-(q_rank + kv_rank + LANE) // 512) * 512)
    code = np.arange(rw_cols, dtype=np.int32)[None, :] // (rw_cols // 4)
    bd = (np.arange(LANE)[:, None] // RW_HEAD == np.arange(LANE)[None, :] // RW_HEAD)
    pad_rows = lambda x: jnp.pad(x, ((0, 0), (0, LANE - x.shape[1]), (0, 0)))
    return {
        "hw": hw, "heads": heads,
        "w_rw": rw_layout(w[:, :rw_cols]).astype(BF16),
        "w_mla": mla_w.astype(BF16),
        "mu": rw_layout(rw_mu[l][None, :]),
        "code": rw_layout(jnp.asarray(code)),
        "k_k": rw_k_k[l][None, :], "k_a": rw_k_a[l][None, :], "r_k": rw_r_k[l].reshape(1, hw),
        "w0": rw_w0[l], "a0": rw_a0[l],
        "wup": pad_rows(rw_w_up[l]).astype(BF16), "aup": pad_rows(rw_a_up[l]).astype(BF16),
        "gup": rw_g_up[l].astype(BF16),
        "bd": jnp.asarray(bd, F32),
        "ln_w": rw_ln_w[l][None, :], "ln_b": rw_ln_b[l][None, :],
        "q_norm": mla_q_norm[l][None, :], "kv_norm": mla_kv_norm[l][None, :],
        "q_up": jnp.pad(mla_q_up[l].reshape(q_rank, heads, MLA_QK),
                        ((0, 0), (0, 0), (0, MLA_HEAD_PAD - MLA_QK))
                        ).reshape(q_rank, heads * MLA_HEAD_PAD).astype(BF16),
        "kv_up": mla_kv_up[l].reshape(kv_rank, heads, 2, MLA_NOPE).transpose(0, 2, 1, 3)
                             .reshape(kv_rank, 2 * heads * MLA_NOPE).astype(BF16),
        "q_gain": _pad_cols(mla_q_gain[l][None, :], MLA_HEAD_PAD),
        "k_gain": _pad_cols(mla_k_gain[l][None, :], MLA_HEAD_PAD),
    }


def _ffn_half_step(xt, mod, g, w_in, w_out, n_lat):
    h = _norm_mod(xt, g, mod[:, 0], mod[:, 1], n_lat)
    act = _swiglu_matmul(h, w_in.astype(BF16))
    return _residual_matmul([act], w_out.astype(BF16), xt, FFN_RES * mod[:, 2], n_lat)


def kernel(x, c, ctx, c_ctx, w_mod, b_mod, norm_g, ffn_w_in, ffn_w_out, w_in, w_out, rw_mu, rw_w0,
           rw_w_up, rw_a0, rw_a_up, rw_g_up, rw_k_k, rw_k_a, rw_r_k, rw_ln_w, rw_ln_b, mla_q_norm,
           mla_q_up, mla_kv_norm, mla_kv_up, mla_q_gain, mla_k_gain):
    batch, n_lat, d = x.shape
    n_ctx = ctx.shape[1]
    depth = w_mod.shape[0]
    assert batch == 1 and c.shape[0] == 1 and ctx.shape[0] == 1
    assert n_lat % GRID_W == 0 and n_ctx % GRID_W == 0

    xt = jnp.concatenate([x[0], ctx[0]], axis=0)
    cc = jnp.zeros((8, d), F32).at[0].set(c[0]).at[1].set(c_ctx)
    rope = _rope_tables(n_lat, n_ctx)

    for l in range(depth):
        last = l == depth - 1
        p = _layer_params(l, w_in, rw_mu, rw_w0, rw_w_up, rw_a0, rw_a_up, rw_g_up, rw_k_k, rw_k_a,
                          rw_r_k, rw_ln_w, rw_ln_b, mla_q_norm, mla_q_up, mla_kv_norm, mla_kv_up,
                          mla_q_gain, mla_k_gain)
        mod = _modulation(cc, w_mod[l], b_mod[l])[:2].reshape(2, N_MOD, d)

        xt = _ffn_half_step(xt, mod[:, 0:3], norm_g[l, 0], ffn_w_in[l, 0], ffn_w_out[l, 0], n_lat)

        hz = _norm_mod(xt, norm_g[l, 1], mod[:, 3], mod[:, 4], n_lat)
        z_rw = _matmul(hz, p["w_rw"], bn_candidates=(512, 256, 128))
        z_mla = _matmul(hz, p["w_mla"])

        st = _rw_streams(z_rw, p, n_lat)
        y_f = _wkv(st, 0, n_lat, reverse=False)
        y_b = _wkv(st, 1, n_lat, reverse=True)
        o_rw = _rw_output(y_f, y_b, st, p)

        q, k, v = _mla_prep(z_mla, p, rope)
        o_mla = _attention(q, k, v, p["heads"], n_lat, with_ctx=not last)

        if last:
            xt, o_rw = xt[:n_lat], o_rw[:n_lat]
        xt = _residual_matmul([o_rw, o_mla], w_out[l].astype(BF16), xt, mod[:, 5], n_lat)

        xt = _ffn_half_step(xt, mod[:, 6:9], norm_g[l, 2], ffn_w_in[l, 1], ffn_w_out[l, 1], n_lat)
    return xt[:n_lat][None]
```

```python
import functools
import math

import numpy as np
import jax
import jax.numpy as jnp
from jax import lax
from jax.experimental import pallas as pl
from jax.experimental.pallas import tpu as pltpu

F32 = jnp.float32
BF16 = jnp.bfloat16
HI = lax.Precision.HIGHEST

GRID_W = 64
NORM_EPS = 1e-6
FFN_RES = 0.5
RW_HEAD = 64
RW_GN_EPS = 64e-5
RW_DECAY_SCALE = math.exp(-0.5)
MLA_V_DIM = 128
MLA_NOPE = 128
MLA_ROPE = 64
MLA_QK = MLA_NOPE + MLA_ROPE
MLA_HEAD_PAD = 256
ROPE_FREQ = MLA_ROPE // 4
ROPE_THETA = 10000.0
N_MOD = 9
LANE = 128
CHUNK = 64
VMEM_LIMIT = 60 * 1024 * 1024


def _pick(n, candidates):
    for c in candidates:
        if n % c == 0:
            return c
    raise ValueError(f"no block size for {n} among {candidates}")


def _params(sem):
    return pltpu.CompilerParams(dimension_semantics=sem, vmem_limit_bytes=VMEM_LIMIT)


def _dot(a, b, precision=None):
    return jnp.dot(a, b, preferred_element_type=F32, precision=precision)


def _dot_nt(a, b, precision=None):
    return lax.dot_general(a, b, (((1,), (1,)), ((), ())), preferred_element_type=F32,
                           precision=precision)


def _dot_tn(a, b, precision=None):
    return lax.dot_general(a, b, (((0,), (0,)), ((), ())), preferred_element_type=F32,
                           precision=precision)


def _bdot(a, b):
    return _dot(a.astype(BF16), b.astype(BF16))


def _bdot_nt(a, b):
    return _dot_nt(a.astype(BF16), b.astype(BF16))


def _row_select(row0, n_rows, n_lat, ref):
    rows = row0 + lax.broadcasted_iota(jnp.int32, (n_rows, 1), 0)
    return jnp.where(rows < n_lat, ref[0:1, :], ref[1:2, :])


def _mod_kernel(c_ref, w_ref, b_ref, o_ref):
    c = c_ref[...]
    a = (c * jax.nn.sigmoid(c)).astype(BF16)
    o_ref[...] = _dot(a, w_ref[...].astype(BF16)) + b_ref[...]


def _modulation(cc, w_mod, b_mod):
    d, n = w_mod.shape
    bn = _pick(n, (512, 256, 128))
    return pl.pallas_call(
        _mod_kernel,
        out_shape=jax.ShapeDtypeStruct((8, n), F32),
        grid=(n // bn,),
        in_specs=[pl.BlockSpec((8, d), lambda j: (0, 0)),
                  pl.BlockSpec((d, bn), lambda j: (0, j)),
                  pl.BlockSpec((1, bn), lambda j: (0, j))],
        out_specs=pl.BlockSpec((8, bn), lambda j: (0, j)),
        compiler_params=_params(("parallel",)),
        name="modulation",
    )(cc, w_mod, b_mod.reshape(1, n))


def _norm_mod_kernel(x_ref, g_ref, sh_ref, sc_ref, o_ref, *, n_lat, bm):
    x = x_ref[...]
    y = x * lax.rsqrt(jnp.mean(x * x, axis=-1, keepdims=True) + NORM_EPS) * g_ref[...]
    row0 = pl.program_id(0) * bm
    sc = _row_select(row0, bm, n_lat, sc_ref)
    sh = _row_select(row0, bm, n_lat, sh_ref)
    o_ref[...] = (y * (1.0 + sc) + sh).astype(o_ref.dtype)


def _norm_mod(x, g, shift, scale, n_lat):
    m, d = x.shape
    bm = _pick(m, (384, 256, 128, 64, 32, 16))
    return pl.pallas_call(
        functools.partial(_norm_mod_kernel, n_lat=n_lat, bm=bm),
        out_shape=jax.ShapeDtypeStruct((m, d), BF16),
        grid=(m // bm,),
        in_specs=[pl.BlockSpec((bm, d), lambda i: (i, 0)),
                  pl.BlockSpec((1, d), lambda i: (0, 0)),
                  pl.BlockSpec((2, d), lambda i: (0, 0)),
                  pl.BlockSpec((2, d), lambda i: (0, 0))],
        out_specs=pl.BlockSpec((bm, d), lambda i: (i, 0)),
        compiler_params=_params(("parallel",)),
        name="norm_mod",
    )(x, g.reshape(1, d), shift, scale)


def _mm_kernel(a_ref, w_ref, o_ref):
    o_ref[...] = _dot(a_ref[...], w_ref[...]).astype(o_ref.dtype)


def _matmul(a, w, out_dtype=F32, bn_candidates=(1024, 512, 256, 128)):
    m, k = a.shape
    n = w.shape[1]
    bm = _pick(m, (1056, 1024, 512, 320, 256, 128, 64))
    bn = _pick(n, bn_candidates)
    return pl.pallas_call(
        _mm_kernel,
        out_shape=jax.ShapeDtypeStruct((m, n), out_dtype),
        grid=(m // bm, n // bn),
        in_specs=[pl.BlockSpec((bm, k), lambda i, j: (i, 0)),
                  pl.BlockSpec((k, bn), lambda i, j: (0, j))],
        out_specs=pl.BlockSpec((bm, bn), lambda i, j: (i, j)),
        compiler_params=_params(("parallel", "parallel")),
        name="matmul",
    )(a, w)


def _swiglu_kernel(a_ref, wg_ref, wu_ref, o_ref):
    a = a_ref[...]
    gate = _dot(a, wg_ref[...])
    up = _dot(a, wu_ref[...])
    o_ref[...] = (gate * jax.nn.sigmoid(gate) * up).astype(o_ref.dtype)


def _swiglu_matmul(a, w):
    m, k = a.shape
    f = w.shape[1] // 2
    bm = _pick(m, (1056, 1024, 512, 320, 256, 128, 64))
    bn = _pick(f, (512, 256, 128))
    nb = f // bn
    return pl.pallas_call(
        _swiglu_kernel,
        out_shape=jax.ShapeDtypeStruct((m, f), BF16),
        grid=(m // bm, nb),
        in_specs=[pl.BlockSpec((bm, k), lambda i, j: (i, 0)),
                  pl.BlockSpec((k, bn), lambda i, j: (0, j)),
                  pl.BlockSpec((k, bn), lambda i, j: (0, j + nb))],
        out_specs=pl.BlockSpec((bm, bn), lambda i, j: (i, j)),
        compiler_params=_params(("parallel", "parallel")),
        name="swiglu_matmul",
    )(a, w, w)


def _residual_kernel(*refs, n_a, n_lat, bm):
    a_refs = refs[:n_a]
    w_refs = refs[n_a:2 * n_a]
    x_ref, g_ref, o_ref = refs[2 * n_a:]
    acc = _dot(a_refs[0][...], w_refs[0][...])
    for a_ref, w_ref in zip(a_refs[1:], w_refs[1:]):
        acc = acc + _dot(a_ref[...], w_ref[...])
    gate = _row_select(pl.program_id(0) * bm, bm, n_lat, g_ref)
    o_ref[...] = x_ref[...] + gate * acc


def _residual_matmul(a_list, w, x, gate, n_lat):
    m, n = x.shape
    n_a = len(a_list)
    k = a_list[0].shape[1]
    assert all(a.shape[1] == k for a in a_list) and w.shape[0] == n_a * k
    bm = _pick(m, (1056, 1024, 512, 320, 256, 128, 64))
    bn = _pick(n, (512, 256, 128))
    a_specs = [pl.BlockSpec((bm, k), lambda i, j: (i, 0)) for _ in a_list]
    w_specs = [pl.BlockSpec((k, bn), functools.partial(lambda i, j, p: (p, j), p=p))
               for p in range(n_a)]
    return pl.pallas_call(
        functools.partial(_residual_kernel, n_a=n_a, n_lat=n_lat, bm=bm),
        out_shape=jax.ShapeDtypeStruct((m, n), F32),
        grid=(m // bm, n // bn),
        in_specs=a_specs + w_specs + [pl.BlockSpec((bm, bn), lambda i, j: (i, j)),
                                      pl.BlockSpec((2, bn), lambda i, j: (0, j))],
        out_specs=pl.BlockSpec((bm, bn), lambda i, j: (i, j)),
        compiler_params=_params(("parallel", "parallel")),
        name="residual_matmul",
    )(*a_list, *([w] * n_a), x, gate)


def _segsum(x, bd):
    parts = [_dot(x[:, j:j + LANE], bd, precision=HI) for j in range(0, x.shape[1], LANE)]
    return jnp.concatenate(parts, axis=1)


def _streams_kernel(zc_ref, zp_ref, zn_ref, mu_ref, code_ref, kkw_ref, ka_ref, rk_ref, w0_ref,
                    a0_ref, wup_ref, aup_ref, gup_ref, bd_ref,
                    r_out, v_out, kk_out, lw0_out, lw1_out, b0_out, b1_out, kd0_out, kd1_out,
                    g_out, bonus_out, *, n_lat_blocks, n_blocks, hw, lg):
    i = pl.program_id(0)
    z = zc_ref[...]
    zp = zp_ref[...]
    zn = zn_ref[...]
    code = code_ref[...]
    rows = lax.broadcasted_iota(jnp.int32, (GRID_W, 1), 0)
    first = rows == 0
    last = rows == GRID_W - 1
    prev_tok = pltpu.roll(z, 1, 0)
    next_tok = pltpu.roll(z, GRID_W - 1, 0)

    left = jnp.where(first, 0.0, prev_tok)
    right = jnp.where(last, 0.0, next_tok)
    up = zp * (i > 0).astype(F32)
    down = zn * (i < n_lat_blocks - 1).astype(F32)
    sh_lat = jnp.where(code == 0, left, jnp.where(code == 1, right, jnp.where(code == 2, up, down)))
    has_prev = (i > n_lat_blocks).astype(F32)
    has_next = (i < n_blocks - 1).astype(F32)
    prev_c = jnp.where(first, zp[GRID_W - 1:GRID_W, :] * has_prev, prev_tok)
    next_c = jnp.where(last, zn[0:1, :] * has_next, next_tok)
    sh_ctx = jnp.where(code < 2, prev_c, next_c)
    is_ctx = (i >= n_lat_blocks).astype(F32)
    shifted = sh_lat + (sh_ctx - sh_lat) * is_ctx
    zs = z + (shifted - z) * mu_ref[...]

    r = zs[:, 0:hw]
    k = zs[:, hw:2 * hw]
    v = zs[:, 2 * hw:3 * hw]
    wd = zs[:, 3 * hw:3 * hw + LANE]
    ad = zs[:, 3 * hw + LANE:3 * hw + 2 * LANE]
    gd = zs[:, 3 * hw + 2 * LANE:3 * hw + 2 * LANE + lg]
    bd = bd_ref[...]

    kk = k * kkw_ref[...]
    kk = kk / jnp.maximum(jnp.sqrt(_segsum(kk * kk, bd)), 1e-12)
    tw = jnp.tanh(wd).astype(BF16)
    adb = ad.astype(BF16)
    ka = ka_ref[...]
    kds = []
    for d, (lw_out, b_out, kd_out) in enumerate(((lw0_out, b0_out, kd0_out),
                                                 (lw1_out, b1_out, kd1_out))):
        w_logit = w0_ref[d:d + 1, :] + _dot(tw, wup_ref[d])
        lw_out[...] = -RW_DECAY_SCALE * jax.nn.sigmoid(w_logit)
        a = jax.nn.sigmoid(a0_ref[d:d + 1, :] + _dot(adb, aup_ref[d]))
        kd = k * (1.0 + (a - 1.0) * ka)
        kd_out[...] = kd
        b_out[...] = kk * a
        kds.append(kd)
    g_out[...] = _dot(jax.nn.sigmoid(gd).astype(BF16), gup_ref[...])
    kmean = (kds[0] + kds[1]) / 2.0
    bonus_out[...] = _segsum(r * kmean * rk_ref[...], bd) * v
    r_out[...] = r
    v_out[...] = v
    kk_out[...] = kk


def _rw_streams(z_rw, p, n_lat):
    t, zw = z_rw.shape
    hw = p["hw"]
    lg = p["gup"].shape[0]
    n_blocks = t // GRID_W
    n_lat_blocks = n_lat // GRID_W
    row = lambda f: pl.BlockSpec((GRID_W, zw), f)
    const2 = lambda shape: pl.BlockSpec(shape, lambda i: (0, 0))
    const3 = lambda shape: pl.BlockSpec(shape, lambda i: (0, 0, 0))
    out_spec = pl.BlockSpec((GRID_W, hw), lambda i: (i, 0))
    outs = pl.pallas_call(
        functools.partial(_streams_kernel, n_lat_blocks=n_lat_blocks, n_blocks=n_blocks, hw=hw,
                          lg=lg),
        out_shape=[jax.ShapeDtypeStruct((t, hw), F32)] * 11,
        grid=(n_blocks,),
        in_specs=[row(lambda i: (i, 0)),
                  row(lambda i: (jnp.maximum(i - 1, 0), 0)),
                  row(lambda i: (jnp.minimum(i + 1, n_blocks - 1), 0)),
                  const2((1, zw)), const2((1, zw)),
                  const2((1, hw)), const2((1, hw)), const2((1, hw)),
                  const2((2, hw)), const2((2, hw)),
                  const3((2, LANE, hw)), const3((2, LANE, hw)), const2((lg, hw)),
                  const2((LANE, LANE))],
        out_specs=[out_spec] * 11,
        compiler_params=_params(("parallel",)),
        name="rw_streams",
    )(z_rw, z_rw, z_rw, p["mu"], p["code"], p["k_k"], p["k_a"], p["r_k"], p["w0"], p["a0"],
      p["wup"], p["aup"], p["gup"], p["bd"])
    names = ("r", "v", "kk", "lw0", "lw1", "b0", "b1", "kd0", "kd1", "g", "bonus")
    return dict(zip(names, outs))


def _wkv_kernel(lw_ref, kk_ref, b_ref, kd_ref, r_ref, v_ref, y_ref, st_ref, *, heads, tb, reverse):
    @pl.when(pl.program_id(1) == 0)
    def _():
        st_ref[...] = jnp.zeros(st_ref.shape, F32)

    n_chunks = tb // CHUNK
    t_idx = lax.broadcasted_iota(jnp.int32, (CHUNK, CHUNK), 0)
    s_idx = lax.broadcasted_iota(jnp.int32, (CHUNK, CHUNK), 1)
    strict = (s_idx > t_idx) if reverse else (s_idx < t_idx)
    incl = (s_idx >= t_idx) if reverse else (s_idx <= t_idx)
    eye = (s_idx == t_idx).astype(F32)

    bt = lax.broadcasted_iota(jnp.int32, (tb, tb), 0)
    bs = lax.broadcasted_iota(jnp.int32, (tb, tb), 1)
    before = (bs >= bt) if reverse else (bs <= bt)
    tri = jnp.where(((bt // CHUNK) == (bs // CHUNK)) & before, 1.0, 0.0).astype(BF16)
    lw = lw_ref[...]
    lw_hi = lw.astype(BF16)
    lw_lo = (lw - lw_hi.astype(F32)).astype(BF16)
    cum = _dot(tri, lw_hi) + _dot(tri, lw_lo)
    edge = 0 if reverse else CHUNK - 1
    tot = jnp.concatenate(
        [jnp.broadcast_to(cum[cc * CHUNK + edge:cc * CHUNK + edge + 1], (CHUNK, cum.shape[1]))
         for cc in range(n_chunks)], axis=0)
    kk, b, kd = kk_ref[...], b_ref[...], kd_ref[...]
    e_neg = jnp.exp(-cum)
    e_end = jnp.exp(tot - cum)
    alpha = kk * jnp.exp(cum - lw)
    rho = r_ref[...] * jnp.exp(cum)
    ar_all = (alpha.astype(BF16), rho.astype(BF16))
    bk_all = ((b * e_neg).astype(BF16), (kd * e_neg).astype(BF16))
    beta_e = (b * e_end).astype(BF16)
    kappa_e = (kd * e_end).astype(BF16)
    g_end = jnp.exp(tot)
    v_all = v_ref[...].astype(BF16)

    pairs = [(cc, h) for cc in range(n_chunks) for h in range(heads)]
    rs = lambda cc: slice(cc * CHUNK, (cc + 1) * CHUNK)
    ls = lambda h: slice(h * RW_HEAD, (h + 1) * RW_HEAD)
    cut = lambda x, cc, h: x[rs(cc), ls(h)]
    stack = lambda xs, cc, h: jnp.concatenate([cut(x, cc, h) for x in xs], axis=0)
    m = [_dot_nt(stack(ar_all, cc, h), stack(bk_all, cc, h)) for cc, h in pairs]
    pw = [jnp.where(strict, -x[:CHUNK, :CHUNK], 0.0) for x in m]
    m_akrk = [jnp.concatenate([jnp.where(strict, x[:CHUNK, CHUNK:], 0.0),
                               jnp.where(incl, x[CHUNK:, CHUNK:], 0.0)], axis=0).astype(BF16)
              for x in m]
    m_rb = [jnp.where(incl, x[CHUNK:, :CHUNK], 0.0).astype(BF16) for x in m]
    t_inv = [eye + x for x in pw]
    for _ in range(int(math.log2(CHUNK)) - 1):
        pwb = [x.astype(BF16) for x in pw]
        pw = [_dot(x, x) for x in pwb]
        t_inv = [t + _dot(t.astype(BF16), x.astype(BF16)) for t, x in zip(t_inv, pw)]
    mv = [_dot(x, cut(v_all, cc, h)) for x, (cc, h) in zip(m_akrk, pairs)]
    w12 = [_dot(t.astype(BF16),
                jnp.concatenate([x[:CHUNK].astype(BF16), cut(ar_all[0], cc, h)], axis=1)
                ).astype(BF16)
           for t, x, (cc, h) in zip(t_inv, mv, pairs)]
    mw = [_dot(x, w) for x, w in zip(m_rb, w12)]
    bw = [_dot_tn(cut(beta_e, cc, h), w) for w, (cc, h) in zip(w12, pairs)]
    kv = [_dot_tn(cut(kappa_e, cc, h), cut(v_all, cc, h)) for cc, h in pairs]
    pre = {}
    for i, (cc, h) in enumerate(pairs):
        y1 = mv[i][CHUNK:] - mw[i][:, :RW_HEAD]
        r2 = cut(rho, cc, h) - mw[i][:, RW_HEAD:]
        g_m = kv[i] - bw[i][:, :RW_HEAD]
        p_m = eye * g_end[cc * CHUNK:cc * CHUNK + 1, ls(h)] - bw[i][:, RW_HEAD:]
        pre[cc, h] = (y1, jnp.concatenate([r2, p_m], axis=0).astype(BF16), g_m)

    st = [st_ref[h] for h in range(heads)]
    ys = {}
    for cc in (range(n_chunks - 1, -1, -1) if reverse else range(n_chunks)):
        prod = [_dot(pre[cc, h][1], st[h].astype(BF16)) for h in range(heads)]
        for h in range(heads):
            ys[cc, h] = pre[cc, h][0] + prod[h][:CHUNK]
            st[h] = prod[h][CHUNK:] + pre[cc, h][2]
    for h in range(heads):
        st_ref[h] = st[h]
    for cc in range(n_chunks):
        y_ref[rs(cc), :] = jnp.concatenate([ys[cc, h] for h in range(heads)], axis=1)


def _wkv(st, d, n_lat, reverse):
    lw, b, kd = st[f"lw{d}"], st[f"b{d}"], st[f"kd{d}"]
    t, hw = lw.shape
    heads = _pick(hw // RW_HEAD, (8, 4, 2))
    tb = _pick(math.gcd(n_lat, t - n_lat), (256, 128, 64))
    nb = t // tb
    n_lat_blocks = n_lat // tb
    if reverse:
        blk = lambda h, c: (nb - 1 - c, h)
    else:
        blk = lambda h, c: ((c + n_lat_blocks) % nb, h)
    spec = pl.BlockSpec((tb, heads * RW_HEAD), blk)
    return pl.pallas_call(
        functools.partial(_wkv_kernel, heads=heads, tb=tb, reverse=reverse),
        out_shape=jax.ShapeDtypeStruct((t, hw), F32),
        grid=(hw // (heads * RW_HEAD), nb),
        in_specs=[spec] * 6,
        out_specs=spec,
        scratch_shapes=[pltpu.VMEM((heads, RW_HEAD, RW_HEAD), F32)],
        compiler_params=_params(("parallel", "arbitrary")),
        name="wkv_bwd" if reverse else "wkv_fwd",
    )(lw, st["kk"], b, kd, st["r"], st["v"])


def _rw_out_kernel(yf_ref, yb_ref, bonus_ref, g_ref, lnw_ref, lnb_ref, bd_ref, o_ref):
    y = yf_ref[...] + yb_ref[...]
    bd = bd_ref[...]
    mean = _segsum(y, bd) / RW_HEAD
    yc = y - mean
    var = _segsum(yc * yc, bd) / RW_HEAD
    o = yc * lax.rsqrt(var + RW_GN_EPS) * lnw_ref[...] + lnb_ref[...] + bonus_ref[...]
    o_ref[...] = (o * g_ref[...]).astype(o_ref.dtype)


def _rw_output(y_f, y_b, st, p):
    t, hw = y_f.shape
    bm = _pick(t, (256, 128, 64))
    row = pl.BlockSpec((bm, hw), lambda i: (i, 0))
    vec = pl.BlockSpec((1, hw), lambda i: (0, 0))
    return pl.pallas_call(
        _rw_out_kernel,
        out_shape=jax.ShapeDtypeStruct((t, hw), BF16),
        grid=(t // bm,),
        in_specs=[row, row, row, row, vec, vec, pl.BlockSpec((LANE, LANE), lambda i: (0, 0))],
        out_specs=row,
        compiler_params=_params(("parallel",)),
        name="rw_output",
    )(y_f, y_b, st["bonus"], st["g"], p["ln_w"], p["ln_b"], p["bd"])


def _mla_prep_kernel(z_ref, qn_ref, qup_ref, kvn_ref, kvup_ref, qg_ref, kg_ref, c_ref, s1_ref,
                     s2_ref, q_out, k_out, v_out, *, q_rank, kv_rank, heads):
    z = z_ref[...]

    def rms(x, g):
        return x * lax.rsqrt(jnp.mean(x * x, axis=-1, keepdims=True) + NORM_EPS) * g

    q = _dot(rms(z[:, :q_rank], qn_ref[...]).astype(BF16), qup_ref[...])
    kv = _dot(rms(z[:, q_rank:q_rank + kv_rank], kvn_ref[...]).astype(BF16), kvup_ref[...])
    k_rope = z[:, q_rank + kv_rank:q_rank + kv_rank + LANE]
    cos, s1, s2 = c_ref[...], s1_ref[...], s2_ref[...]

    def head_norm_rope(x, g):
        x = x * lax.rsqrt(jnp.sum(x * x, axis=-1, keepdims=True) / MLA_QK + NORM_EPS) * g
        return (x * cos + pltpu.roll(x, MLA_HEAD_PAD - ROPE_FREQ, 1) * s1
                + pltpu.roll(x, ROPE_FREQ, 1) * s2)

    for h in range(heads):
        cols = slice(h * MLA_HEAD_PAD, (h + 1) * MLA_HEAD_PAD)
        q_out[:, cols] = head_norm_rope(q[:, cols], qg_ref[...]).astype(BF16)
        kh = jnp.concatenate([kv[:, h * MLA_NOPE:(h + 1) * MLA_NOPE], k_rope], axis=1)
        k_out[:, cols] = head_norm_rope(kh, kg_ref[...]).astype(BF16)
    v_out[...] = kv[:, heads * MLA_NOPE:].astype(BF16)


def _mla_prep(z_mla, p, rope):
    t, zw = z_mla.shape
    heads = p["heads"]
    q_rank, kv_rank = p["q_up"].shape[0], p["kv_up"].shape[0]
    bm = _pick(t, (256, 128, 64))
    qw = heads * MLA_HEAD_PAD
    row = lambda w: pl.BlockSpec((bm, w), lambda i: (i, 0))
    const = lambda a: pl.BlockSpec(a.shape, lambda i: (0, 0))
    args = (z_mla, p["q_norm"], p["q_up"], p["kv_norm"], p["kv_up"], p["q_gain"], p["k_gain"])
    return pl.pallas_call(
        functools.partial(_mla_prep_kernel, q_rank=q_rank, kv_rank=kv_rank, heads=heads),
        out_shape=[jax.ShapeDtypeStruct((t, qw), BF16), jax.ShapeDtypeStruct((t, qw), BF16),
                   jax.ShapeDtypeStruct((t, heads * MLA_V_DIM), BF16)],
        grid=(t // bm,),
        in_specs=[row(zw)] + [const(a) for a in args[1:]] + [row(MLA_HEAD_PAD)] * 3,
        out_specs=[row(qw), row(qw), row(heads * MLA_V_DIM)],
        compiler_params=_params(("parallel",)),
        name="mla_prep",
    )(*args, *rope)


def _attn_kernel(q_ref, k_ref, vt_ref, o_ref, *, n_lat, bq, scale, with_ctx):
    def attend(key0, n_keys):
        bk = _pick(n_keys, (768, 512, 256, 128))
        q = q_ref[...]
        c = scale * math.log2(math.e)
        scores = lambda j: _dot_nt(k_ref[key0 + j * bk:key0 + (j + 1) * bk, :], q)
        s_next = scores(0)
        m = jnp.full((1, bq), -jnp.inf, F32)
        denom = jnp.zeros((1, bq), F32)
        acc = jnp.zeros((MLA_V_DIM, bq), F32)
        for j in range(n_keys // bk):
            s = s_next
            if (j + 1) * bk < n_keys:
                s_next = scores(j + 1)
            m_new = jnp.maximum(m, jnp.max(s, axis=0, keepdims=True))
            p = jnp.exp2((s - m_new) * c)
            corr = jnp.exp2((m - m_new) * c)
            denom = corr * denom + jnp.sum(p, axis=0, keepdims=True)
            acc = corr * acc + _dot(vt_ref[:, key0 + j * bk:key0 + (j + 1) * bk], p.astype(BF16))
            m = m_new
        o_ref[...] = (acc / denom).T.astype(o_ref.dtype)

    n_all = k_ref.shape[0]
    if not with_ctx:
        attend(0, n_all)
        return
    is_latent = pl.program_id(1) < n_lat // bq

    @pl.when(is_latent)
    def _():
        attend(0, n_all)

    @pl.when(jnp.logical_not(is_latent))
    def _():
        attend(n_lat, n_all - n_lat)


def _attention(q, k, vt, heads, n_lat, with_ctx):
    t = k.shape[0]
    n_q = t if with_ctx else n_lat
    bq = _pick(math.gcd(n_lat, t - n_lat), (256, 128, 64))
    return pl.pallas_call(
        functools.partial(_attn_kernel, n_lat=n_lat, bq=bq, scale=1.0 / math.sqrt(MLA_QK),
                          with_ctx=with_ctx),
        out_shape=jax.ShapeDtypeStruct((n_q, heads * MLA_V_DIM), BF16),
        grid=(heads, n_q // bq),
        in_specs=[pl.BlockSpec((bq, MLA_HEAD_PAD), lambda h, i: (i, h)),
                  pl.BlockSpec((t, MLA_HEAD_PAD), lambda h, i: (0, h)),
                  pl.BlockSpec((MLA_V_DIM, t), lambda h, i: (h, 0))],
        out_specs=pl.BlockSpec((bq, MLA_V_DIM), lambda h, i: (i, h)),
        compiler_params=_params(("parallel", "parallel")),
        name="attention",
    )(q, k, vt)


def _pad_cols(w, width):
    return jnp.pad(w, ((0, 0), (0, width - w.shape[1])))


def _rope_tables(n_lat, n_ctx):
    t = jnp.arange(n_lat, dtype=jnp.int32)
    pos = jnp.stack([t // GRID_W, t % GRID_W], axis=-1).astype(F32)
    inv_freq = ROPE_THETA ** (-jnp.arange(ROPE_FREQ, dtype=F32) / ROPE_FREQ)
    ang = pos[:, :, None] * inv_freq
    cos, sin = jnp.cos(ang), jnp.sin(ang)
    zero = jnp.zeros_like(sin)
    lay = lambda first, second: jnp.stack([first, second], axis=2).reshape(n_lat, MLA_ROPE)
    tabs = []
    for rope_part, fill in ((lay(cos, cos), 1.0), (lay(-sin, zero), 0.0), (lay(zero, sin), 0.0)):
        tab = jnp.concatenate([jnp.full((n_lat, MLA_NOPE), fill, F32), rope_part,
                               jnp.full((n_lat, MLA_HEAD_PAD - MLA_QK), fill, F32)], axis=1)
        tabs.append(jnp.concatenate([tab, jnp.full((n_ctx, MLA_HEAD_PAD), fill, F32)], axis=0))
    return tuple(tabs)


def _layer_params(l, w_in, rw_mu, rw_w0, rw_w_up, rw_a0, rw_a_up, rw_g_up, rw_k_k, rw_k_a, rw_r_k,
                  rw_ln_w, rw_ln_b, mla_q_norm, mla_q_up, mla_kv_norm, mla_kv_up, mla_q_gain,
                  mla_k_gain):
    hw = rw_k_k.shape[1]
    ld, la, lg = rw_w_up.shape[2], rw_a_up.shape[2], rw_g_up.shape[1]
    q_rank, kv_rank = mla_q_up.shape[1], mla_kv_up.shape[1]
    heads = mla_kv_up.shape[2] // (MLA_NOPE + MLA_V_DIM)
    rw_cols = 3 * hw + ld + la + lg
    assert ld <= LANE and la <= LANE and lg % LANE == 0 and rw_cols % 4 == 0

    def rw_layout(a, fill=0.0):
        o = 3 * hw
        pad = lambda x: jnp.pad(x, ((0, 0), (0, LANE - x.shape[1])), constant_values=fill)
        return jnp.concatenate([a[:, :o], pad(a[:, o:o + ld]), pad(a[:, o + ld:o + ld + la]),
                                a[:, o + ld + la:rw_cols]], axis=1)

    w = w_in[l]
    mla_w = _pad_cols(w[:, rw_cols:], -(-(q_rank + kv_rank + LANE) // 512) * 512)
    code = np.arange(rw_cols, dtype=np.int32)[None, :] // (rw_cols // 4)
    bd = (np.arange(LANE)[:, None] // RW_HEAD == np.arange(LANE)[None, :] // RW_HEAD)
    pad_rows = lambda x: jnp.pad(x, ((0, 0), (0, LANE - x.shape[1]), (0, 0)))
    return {
        "hw": hw, "heads": heads,
        "w_rw": rw_layout(w[:, :rw_cols]).astype(BF16),
        "w_mla": mla_w.astype(BF16),
        "mu": rw_layout(rw_mu[l][None, :]),
        "code": rw_layout(jnp.asarray(code)),
        "k_k": rw_k_k[l][None, :], "k_a": rw_k_a[l][None, :], "r_k": rw_r_k[l].reshape(1, hw),
        "w0": rw_w0[l], "a0": rw_a0[l],
        "wup": pad_rows(rw_w_up[l]).astype(BF16), "aup": pad_rows(rw_a_up[l]).astype(BF16),
        "gup": rw_g_up[l].astype(BF16),
        "bd": jnp.asarray(bd, F32),
        "ln_w": rw_ln_w[l][None, :], "ln_b": rw_ln_b[l][None, :],
        "q_norm": mla_q_norm[l][None, :], "kv_norm": mla_kv_norm[l][None, :],
        "q_up": jnp.pad(mla_q_up[l].reshape(q_rank, heads, MLA_QK),
                        ((0, 0), (0, 0), (0, MLA_HEAD_PAD - MLA_QK))
                        ).reshape(q_rank, heads * MLA_HEAD_PAD).astype(BF16),
        "kv_up": mla_kv_up[l].reshape(kv_rank, heads, 2, MLA_NOPE).transpose(0, 2, 1, 3)
                             .reshape(kv_rank, 2 * heads * MLA_NOPE).astype(BF16),
        "q_gain": _pad_cols(mla_q_gain[l][None, :], MLA_HEAD_PAD),
        "k_gain": _pad_cols(mla_k_gain[l][None, :], MLA_HEAD_PAD),
    }


def _ffn_half_step(xt, mod, g, w_in, w_out, n_lat):
    h = _norm_mod(xt, g, mod[:, 0], mod[:, 1], n_lat)
    act = _swiglu_matmul(h, w_in.astype(BF16))
    return _residual_matmul([act], w_out.astype(BF16), xt, FFN_RES * mod[:, 2], n_lat)


def kernel(x, c, ctx, c_ctx, w_mod, b_mod, norm_g, ffn_w_in, ffn_w_out, w_in, w_out, rw_mu, rw_w0,
           rw_w_up, rw_a0, rw_a_up, rw_g_up, rw_k_k, rw_k_a, rw_r_k, rw_ln_w, rw_ln_b, mla_q_norm,
           mla_q_up, mla_kv_norm, mla_kv_up, mla_q_gain, mla_k_gain):
    batch, n_lat, d = x.shape
    n_ctx = ctx.shape[1]
    depth = w_mod.shape[0]
    assert batch == 1 and c.shape[0] == 1 and ctx.shape[0] == 1
    assert n_lat % GRID_W == 0 and n_ctx % GRID_W == 0

    xt = jnp.concatenate([x[0], ctx[0]], axis=0)
    cc = jnp.zeros((8, d), F32).at[0].set(c[0]).at[1].set(c_ctx)
    rope = _rope_tables(n_lat, n_ctx)

    for l in range(depth):
        last = l == depth - 1
        p = _layer_params(l, w_in, rw_mu, rw_w0, rw_w_up, rw_a0, rw_a_up, rw_g_up, rw_k_k, rw_k_a,
                          rw_r_k, rw_ln_w, rw_ln_b, mla_q_norm, mla_q_up, mla_kv_norm, mla_kv_up,
                          mla_q_gain, mla_k_gain)
        mod = _modulation(cc, w_mod[l], b_mod[l])[:2].reshape(2, N_MOD, d)

        xt = _ffn_half_step(xt, mod[:, 0:3], norm_g[l, 0], ffn_w_in[l, 0], ffn_w_out[l, 0], n_lat)

        hz = _norm_mod(xt, norm_g[l, 1], mod[:, 3], mod[:, 4], n_lat)
        z_rw = _matmul(hz, p["w_rw"], bn_candidates=(512, 256, 128))
        z_mla = _matmul(hz, p["w_mla"])

        st = _rw_streams(z_rw, p, n_lat)
        y_f = _wkv(st, 0, n_lat, reverse=False)
        y_b = _wkv(st, 1, n_lat, reverse=True)
        o_rw = _rw_output(y_f, y_b, st, p)

        q, k, v = _mla_prep(z_mla, p, rope)
        o_mla = _attention(q, k, v.T, p["heads"], n_lat, with_ctx=not last)

        if last:
            xt, o_rw = xt[:n_lat], o_rw[:n_lat]
        xt = _residual_matmul([o_rw, o_mla], w_out[l].astype(BF16), xt, mod[:, 5], n_lat)

        xt = _ffn_half_step(xt, mod[:, 6:9], norm_g[l, 2], ffn_w_in[l, 1], ffn_w_out[l, 1], n_lat)
    return xt[:n_lat][None]
```

```python
import functools
import math

import numpy as np
import jax
import jax.numpy as jnp
from jax import lax
from jax.experimental import pallas as pl
from jax.experimental.pallas import tpu as pltpu

F32 = jnp.float32
BF16 = jnp.bfloat16
HI = lax.Precision.HIGHEST

GRID_W = 64
NORM_EPS = 1e-6
FFN_RES = 0.5
RW_HEAD = 64
RW_GN_EPS = 64e-5
RW_DECAY_SCALE = math.exp(-0.5)
MLA_V_DIM = 128
MLA_NOPE = 128
MLA_ROPE = 64
MLA_QK = MLA_NOPE + MLA_ROPE
MLA_HEAD_PAD = 256
ROPE_FREQ = MLA_ROPE // 4
ROPE_THETA = 10000.0
N_MOD = 9
LANE = 128
CHUNK = 64
ATTN_Q_SCALE = math.log2(math.e) / math.sqrt(MLA_QK)
V_ROWS = MLA_V_DIM + 16
VMEM_LIMIT = 60 * 1024 * 1024


def _pick(n, candidates):
    for c in candidates:
        if n % c == 0:
            return c
    raise ValueError(f"no block size for {n} among {candidates}")


def _params(sem):
    return pltpu.CompilerParams(dimension_semantics=sem, vmem_limit_bytes=VMEM_LIMIT)


def _dot(a, b, precision=None):
    return jnp.dot(a, b, preferred_element_type=F32, precision=precision)


def _dot_nt(a, b, precision=None):
    return lax.dot_general(a, b, (((1,), (1,)), ((), ())), preferred_element_type=F32,
                           precision=precision)


def _dot_tn(a, b, precision=None):
    return lax.dot_general(a, b, (((0,), (0,)), ((), ())), preferred_element_type=F32,
                           precision=precision)


def _bdot(a, b):
    return _dot(a.astype(BF16), b.astype(BF16))


def _bdot_nt(a, b):
    return _dot_nt(a.astype(BF16), b.astype(BF16))


def _row_select(row0, n_rows, n_lat, ref):
    rows = row0 + lax.broadcasted_iota(jnp.int32, (n_rows, 1), 0)
    return jnp.where(rows < n_lat, ref[0:1, :], ref[1:2, :])


def _mod_kernel(c_ref, w_ref, b_ref, o_ref):
    c = c_ref[...]
    a = (c * jax.nn.sigmoid(c)).astype(BF16)
    o_ref[...] = _dot(a, w_ref[...].astype(BF16)) + b_ref[...]


def _modulation(cc, w_mod, b_mod):
    d, n = w_mod.shape
    bn = _pick(n, (512, 256, 128))
    return pl.pallas_call(
        _mod_kernel,
        out_shape=jax.ShapeDtypeStruct((8, n), F32),
        grid=(n // bn,),
        in_specs=[pl.BlockSpec((8, d), lambda j: (0, 0)),
                  pl.BlockSpec((d, bn), lambda j: (0, j)),
                  pl.BlockSpec((1, bn), lambda j: (0, j))],
        out_specs=pl.BlockSpec((8, bn), lambda j: (0, j)),
        compiler_params=_params(("parallel",)),
        name="modulation",
    )(cc, w_mod, b_mod.reshape(1, n))


def _norm_mod_kernel(x_ref, g_ref, sh_ref, sc_ref, o_ref, *, n_lat, bm):
    x = x_ref[...]
    y = x * lax.rsqrt(jnp.mean(x * x, axis=-1, keepdims=True) + NORM_EPS) * g_ref[...]
    row0 = pl.program_id(0) * bm
    sc = _row_select(row0, bm, n_lat, sc_ref)
    sh = _row_select(row0, bm, n_lat, sh_ref)
    o_ref[...] = (y * (1.0 + sc) + sh).astype(o_ref.dtype)


def _norm_mod(x, g, shift, scale, n_lat):
    m, d = x.shape
    bm = _pick(m, (384, 256, 128, 64, 32, 16))
    return pl.pallas_call(
        functools.partial(_norm_mod_kernel, n_lat=n_lat, bm=bm),
        out_shape=jax.ShapeDtypeStruct((m, d), BF16),
        grid=(m // bm,),
        in_specs=[pl.BlockSpec((bm, d), lambda i: (i, 0)),
                  pl.BlockSpec((1, d), lambda i: (0, 0)),
                  pl.BlockSpec((2, d), lambda i: (0, 0)),
                  pl.BlockSpec((2, d), lambda i: (0, 0))],
        out_specs=pl.BlockSpec((bm, d), lambda i: (i, 0)),
        compiler_params=_params(("parallel",)),
        name="norm_mod",
    )(x, g.reshape(1, d), shift, scale)


def _mm_kernel(a_ref, w_ref, o_ref):
    o_ref[...] = _dot(a_ref[...], w_ref[...]).astype(o_ref.dtype)


def _matmul(a, w, out_dtype=F32, bn_candidates=(1024, 512, 256, 128)):
    m, k = a.shape
    n = w.shape[1]
    bm = _pick(m, (1056, 1024, 512, 320, 256, 128, 64))
    bn = _pick(n, bn_candidates)
    return pl.pallas_call(
        _mm_kernel,
        out_shape=jax.ShapeDtypeStruct((m, n), out_dtype),
        grid=(m // bm, n // bn),
        in_specs=[pl.BlockSpec((bm, k), lambda i, j: (i, 0)),
                  pl.BlockSpec((k, bn), lambda i, j: (0, j))],
        out_specs=pl.BlockSpec((bm, bn), lambda i, j: (i, j)),
        compiler_params=_params(("parallel", "parallel")),
        name="matmul",
    )(a, w)


def _resident(block_shape, index_map):
    return pl.BlockSpec(block_shape, index_map, pipeline_mode=pl.Buffered(1))


def _swiglu_kernel(a_ref, wg_ref, wu_ref, o_ref):
    a = a_ref[...]
    gate = _dot(a, wg_ref[...].astype(BF16))
    up = _dot(a, wu_ref[...].astype(BF16))
    o_ref[...] = (gate * jax.nn.sigmoid(gate) * up).astype(o_ref.dtype)


def _swiglu_matmul(a, w):
    m, k = a.shape
    f = w.shape[1] // 2
    bm = _pick(m, (1056, 1024, 512, 320, 256, 128, 64))
    bn = _pick(f, (256, 128))
    nb = f // bn
    return pl.pallas_call(
        _swiglu_kernel,
        out_shape=jax.ShapeDtypeStruct((m, f), BF16),
        grid=(m // bm, nb),
        in_specs=[_resident((bm, k), lambda i, j: (i, 0)),
                  pl.BlockSpec((k, bn), lambda i, j: (0, j)),
                  pl.BlockSpec((k, bn), lambda i, j: (0, j + nb))],
        out_specs=pl.BlockSpec((bm, bn), lambda i, j: (i, j)),
        compiler_params=_params(("parallel", "arbitrary")),
        name="swiglu_matmul",
    )(a, w, w)


def _residual_kernel(*refs, n_a, n_lat, bm):
    a_refs = refs[:n_a]
    w_refs = refs[n_a:2 * n_a]
    x_ref, g_ref, o_ref = refs[2 * n_a:]
    acc = _dot(a_refs[0][...], w_refs[0][...].astype(BF16))
    for a_ref, w_ref in zip(a_refs[1:], w_refs[1:]):
        acc = acc + _dot(a_ref[...], w_ref[...].astype(BF16))
    gate = _row_select(pl.program_id(0) * bm, bm, n_lat, g_ref)
    o_ref[...] = x_ref[...] + gate * acc


def _residual_matmul(a_list, w, x, gate, n_lat):
    m, n = x.shape
    n_a = len(a_list)
    k = a_list[0].shape[1]
    assert all(a.shape[1] == k for a in a_list) and w.shape[0] == n_a * k
    bm = _pick(m, (1056, 1024, 512, 320, 256, 128, 64))
    bn = _pick(n, (256, 128))
    a_specs = [_resident((bm, k), lambda i, j: (i, 0)) for _ in a_list]
    w_specs = [pl.BlockSpec((k, bn), functools.partial(lambda i, j, p: (p, j), p=p))
               for p in range(n_a)]
    return pl.pallas_call(
        functools.partial(_residual_kernel, n_a=n_a, n_lat=n_lat, bm=bm),
        out_shape=jax.ShapeDtypeStruct((m, n), F32),
        grid=(m // bm, n // bn),
        in_specs=a_specs + w_specs + [pl.BlockSpec((bm, bn), lambda i, j: (i, j)),
                                      pl.BlockSpec((2, bn), lambda i, j: (0, j))],
        out_specs=pl.BlockSpec((bm, bn), lambda i, j: (i, j)),
        compiler_params=_params(("parallel", "arbitrary")),
        name="residual_matmul",
    )(*a_list, *([w] * n_a), x, gate)


def _segsum(x, bd):
    parts = [_dot(x[:, j:j + LANE], bd, precision=HI) for j in range(0, x.shape[1], LANE)]
    return jnp.concatenate(parts, axis=1)


def _streams_kernel(zc_ref, zp_ref, zn_ref, mu_ref, code_ref, kkw_ref, ka_ref, rk_ref, w0_ref,
                    a0_ref, wup_ref, aup_ref, gup_ref, bd_ref,
                    r_out, v_out, kk_out, lw0_out, lw1_out, b0_out, b1_out, kd0_out, kd1_out,
                    g_out, bonus_out, *, n_lat_blocks, n_blocks, hw, lg):
    i = pl.program_id(0)
    z = zc_ref[...]
    zp = zp_ref[...]
    zn = zn_ref[...]
    code = code_ref[...]
    rows = lax.broadcasted_iota(jnp.int32, (GRID_W, 1), 0)
    first = rows == 0
    last = rows == GRID_W - 1
    prev_tok = pltpu.roll(z, 1, 0)
    next_tok = pltpu.roll(z, GRID_W - 1, 0)

    left = jnp.where(first, 0.0, prev_tok)
    right = jnp.where(last, 0.0, next_tok)
    up = zp * (i > 0).astype(F32)
    down = zn * (i < n_lat_blocks - 1).astype(F32)
    sh_lat = jnp.where(code == 0, left, jnp.where(code == 1, right, jnp.where(code == 2, up, down)))
    has_prev = (i > n_lat_blocks).astype(F32)
    has_next = (i < n_blocks - 1).astype(F32)
    prev_c = jnp.where(first, zp[GRID_W - 1:GRID_W, :] * has_prev, prev_tok)
    next_c = jnp.where(last, zn[0:1, :] * has_next, next_tok)
    sh_ctx = jnp.where(code < 2, prev_c, next_c)
    is_ctx = (i >= n_lat_blocks).astype(F32)
    shifted = sh_lat + (sh_ctx - sh_lat) * is_ctx
    zs = z + (shifted - z) * mu_ref[...]

    r = zs[:, 0:hw]
    k = zs[:, hw:2 * hw]
    v = zs[:, 2 * hw:3 * hw]
    wd = zs[:, 3 * hw:3 * hw + LANE]
    ad = zs[:, 3 * hw + LANE:3 * hw + 2 * LANE]
    gd = zs[:, 3 * hw + 2 * LANE:3 * hw + 2 * LANE + lg]
    bd = bd_ref[...]

    kk = k * kkw_ref[...]
    kk = kk / jnp.maximum(jnp.sqrt(_segsum(kk * kk, bd)), 1e-12)
    tw = jnp.tanh(wd).astype(BF16)
    adb = ad.astype(BF16)
    ka = ka_ref[...]
    kds = []
    for d, (lw_out, b_out, kd_out) in enumerate(((lw0_out, b0_out, kd0_out),
                                                 (lw1_out, b1_out, kd1_out))):
        w_logit = w0_ref[d:d + 1, :] + _dot(tw, wup_ref[d])
        lw_out[...] = -RW_DECAY_SCALE * jax.nn.sigmoid(w_logit)
        a = jax.nn.sigmoid(a0_ref[d:d + 1, :] + _dot(adb, aup_ref[d]))
        kd = k * (1.0 + (a - 1.0) * ka)
        kd_out[...] = kd
        b_out[...] = kk * a
        kds.append(kd)
    g_out[...] = _dot(jax.nn.sigmoid(gd).astype(BF16), gup_ref[...])
    kmean = (kds[0] + kds[1]) / 2.0
    bonus_out[...] = _segsum(r * kmean * rk_ref[...], bd) * v
    r_out[...] = r
    v_out[...] = v
    kk_out[...] = kk


def _rw_streams(z_rw, p, n_lat):
    t, zw = z_rw.shape
    hw = p["hw"]
    lg = p["gup"].shape[0]
    n_blocks = t // GRID_W
    n_lat_blocks = n_lat // GRID_W
    row = lambda f: pl.BlockSpec((GRID_W, zw), f)
    const2 = lambda shape: pl.BlockSpec(shape, lambda i: (0, 0))
    const3 = lambda shape: pl.BlockSpec(shape, lambda i: (0, 0, 0))
    out_spec = pl.BlockSpec((GRID_W, hw), lambda i: (i, 0))
    outs = pl.pallas_call(
        functools.partial(_streams_kernel, n_lat_blocks=n_lat_blocks, n_blocks=n_blocks, hw=hw,
                          lg=lg),
        out_shape=[jax.ShapeDtypeStruct((t, hw), F32)] * 11,
        grid=(n_blocks,),
        in_specs=[row(lambda i: (i, 0)),
                  row(lambda i: (jnp.maximum(i - 1, 0), 0)),
                  row(lambda i: (jnp.minimum(i + 1, n_blocks - 1), 0)),
                  const2((1, zw)), const2((1, zw)),
                  const2((1, hw)), const2((1, hw)), const2((1, hw)),
                  const2((2, hw)), const2((2, hw)),
                  const3((2, LANE, hw)), const3((2, LANE, hw)), const2((lg, hw)),
                  const2((LANE, LANE))],
        out_specs=[out_spec] * 11,
        compiler_params=_params(("parallel",)),
        name="rw_streams",
    )(z_rw, z_rw, z_rw, p["mu"], p["code"], p["k_k"], p["k_a"], p["r_k"], p["w0"], p["a0"],
      p["wup"], p["aup"], p["gup"], p["bd"])
    names = ("r", "v", "kk", "lw0", "lw1", "b0", "b1", "kd0", "kd1", "g", "bonus")
    return dict(zip(names, outs))


def _wkv_kernel(lw_ref, kk_ref, b_ref, kd_ref, r_ref, v_ref, y_ref, st_ref, *, heads, tb, reverse):
    @pl.when(pl.program_id(1) == 0)
    def _():
        st_ref[...] = jnp.zeros(st_ref.shape, F32)

    n_chunks = tb // CHUNK
    t_idx = lax.broadcasted_iota(jnp.int32, (CHUNK, CHUNK), 0)
    s_idx = lax.broadcasted_iota(jnp.int32, (CHUNK, CHUNK), 1)
    strict = (s_idx > t_idx) if reverse else (s_idx < t_idx)
    incl = (s_idx >= t_idx) if reverse else (s_idx <= t_idx)
    eye = (s_idx == t_idx).astype(F32)

    bt = lax.broadcasted_iota(jnp.int32, (tb, tb), 0)
    bs = lax.broadcasted_iota(jnp.int32, (tb, tb), 1)
    before = (bs >= bt) if reverse else (bs <= bt)
    tri = jnp.where(((bt // CHUNK) == (bs // CHUNK)) & before, 1.0, 0.0).astype(BF16)
    lw = lw_ref[...]
    lw_hi = lw.astype(BF16)
    lw_lo = (lw - lw_hi.astype(F32)).astype(BF16)
    cum = _dot(tri, lw_hi) + _dot(tri, lw_lo)
    edge = 0 if reverse else CHUNK - 1
    tot = jnp.concatenate(
        [jnp.broadcast_to(cum[cc * CHUNK + edge:cc * CHUNK + edge + 1], (CHUNK, cum.shape[1]))
         for cc in range(n_chunks)], axis=0)
    kk, b, kd = kk_ref[...], b_ref[...], kd_ref[...]
    e_neg = jnp.exp(-cum)
    e_end = jnp.exp(tot - cum)
    alpha = kk * jnp.exp(cum - lw)
    rho = r_ref[...] * jnp.exp(cum)
    ar_all = (alpha.astype(BF16), rho.astype(BF16))
    bk_all = ((b * e_neg).astype(BF16), (kd * e_neg).astype(BF16))
    beta_e = (b * e_end).astype(BF16)
    kappa_e = (kd * e_end).astype(BF16)
    g_end = jnp.exp(tot)
    v_all = v_ref[...].astype(BF16)

    pairs = [(cc, h) for cc in range(n_chunks) for h in range(heads)]
    rs = lambda cc: slice(cc * CHUNK, (cc + 1) * CHUNK)
    ls = lambda h: slice(h * RW_HEAD, (h + 1) * RW_HEAD)
    cut = lambda x, cc, h: x[rs(cc), ls(h)]
    stack = lambda xs, cc, h: jnp.concatenate([cut(x, cc, h) for x in xs], axis=0)
    m = [_dot_nt(stack(ar_all, cc, h), stack(bk_all, cc, h)) for cc, h in pairs]
    pw = [jnp.where(strict, -x[:CHUNK, :CHUNK], 0.0) for x in m]
    m_akrk = [jnp.concatenate([jnp.where(strict, x[:CHUNK, CHUNK:], 0.0),
                               jnp.where(incl, x[CHUNK:, CHUNK:], 0.0)], axis=0).astype(BF16)
              for x in m]
    m_rb = [jnp.where(incl, x[CHUNK:, :CHUNK], 0.0).astype(BF16) for x in m]
    t_inv = [eye + x for x in pw]
    for _ in range(int(math.log2(CHUNK)) - 1):
        pwb = [x.astype(BF16) for x in pw]
        pw = [_dot(x, x) for x in pwb]
        t_inv = [t + _dot(t.astype(BF16), x.astype(BF16)) for t, x in zip(t_inv, pw)]
    mv = [_dot(x, cut(v_all, cc, h)) for x, (cc, h) in zip(m_akrk, pairs)]
    w12 = [_dot(t.astype(BF16),
                jnp.concatenate([x[:CHUNK].astype(BF16), cut(ar_all[0], cc, h)], axis=1)
                ).astype(BF16)
           for t, x, (cc, h) in zip(t_inv, mv, pairs)]
    mw = [_dot(x, w) for x, w in zip(m_rb, w12)]
    bw = [_dot_tn(cut(beta_e, cc, h), w) for w, (cc, h) in zip(w12, pairs)]
    kv = [_dot_tn(cut(kappa_e, cc, h), cut(v_all, cc, h)) for cc, h in pairs]
    pre = {}
    for i, (cc, h) in enumerate(pairs):
        y1 = mv[i][CHUNK:] - mw[i][:, :RW_HEAD]
        r2 = cut(rho, cc, h) - mw[i][:, RW_HEAD:]
        g_m = kv[i] - bw[i][:, :RW_HEAD]
        p_m = eye * g_end[cc * CHUNK:cc * CHUNK + 1, ls(h)] - bw[i][:, RW_HEAD:]
        pre[cc, h] = (y1, jnp.concatenate([r2, p_m], axis=0).astype(BF16), g_m)

    st = [st_ref[h] for h in range(heads)]
    ys = {}
    for cc in (range(n_chunks - 1, -1, -1) if reverse else range(n_chunks)):
        prod = [_dot(pre[cc, h][1], st[h].astype(BF16)) for h in range(heads)]
        for h in range(heads):
            ys[cc, h] = pre[cc, h][0] + prod[h][:CHUNK]
            st[h] = prod[h][CHUNK:] + pre[cc, h][2]
    for h in range(heads):
        st_ref[h] = st[h]
    for cc in range(n_chunks):
        y_ref[rs(cc), :] = jnp.concatenate([ys[cc, h] for h in range(heads)], axis=1)


def _wkv(st, d, n_lat, reverse):
    lw, b, kd = st[f"lw{d}"], st[f"b{d}"], st[f"kd{d}"]
    t, hw = lw.shape
    heads = _pick(hw // RW_HEAD, (8, 4, 2))
    tb = _pick(math.gcd(n_lat, t - n_lat), (256, 128, 64))
    nb = t // tb
    n_lat_blocks = n_lat // tb
    if reverse:
        blk = lambda h, c: (nb - 1 - c, h)
    else:
        blk = lambda h, c: ((c + n_lat_blocks) % nb, h)
    spec = pl.BlockSpec((tb, heads * RW_HEAD), blk)
    return pl.pallas_call(
        functools.partial(_wkv_kernel, heads=heads, tb=tb, reverse=reverse),
        out_shape=jax.ShapeDtypeStruct((t, hw), F32),
        grid=(hw // (heads * RW_HEAD), nb),
        in_specs=[spec] * 6,
        out_specs=spec,
        scratch_shapes=[pltpu.VMEM((heads, RW_HEAD, RW_HEAD), F32)],
        compiler_params=_params(("parallel", "arbitrary")),
        name="wkv_bwd" if reverse else "wkv_fwd",
    )(lw, st["kk"], b, kd, st["r"], st["v"])


def _rw_out_kernel(yf_ref, yb_ref, bonus_ref, g_ref, lnw_ref, lnb_ref, bd_ref, o_ref):
    y = yf_ref[...] + yb_ref[...]
    bd = bd_ref[...]
    mean = _segsum(y, bd) / RW_HEAD
    yc = y - mean
    var = _segsum(yc * yc, bd) / RW_HEAD
    o = yc * lax.rsqrt(var + RW_GN_EPS) * lnw_ref[...] + lnb_ref[...] + bonus_ref[...]
    o_ref[...] = (o * g_ref[...]).astype(o_ref.dtype)


def _rw_output(y_f, y_b, st, p):
    t, hw = y_f.shape
    bm = _pick(t, (256, 128, 64))
    row = pl.BlockSpec((bm, hw), lambda i: (i, 0))
    vec = pl.BlockSpec((1, hw), lambda i: (0, 0))
    return pl.pallas_call(
        _rw_out_kernel,
        out_shape=jax.ShapeDtypeStruct((t, hw), BF16),
        grid=(t // bm,),
        in_specs=[row, row, row, row, vec, vec, pl.BlockSpec((LANE, LANE), lambda i: (0, 0))],
        out_specs=row,
        compiler_params=_params(("parallel",)),
        name="rw_output",
    )(y_f, y_b, st["bonus"], st["g"], p["ln_w"], p["ln_b"], p["bd"])


def _mla_prep_kernel(z_ref, qn_ref, qup_ref, kvn_ref, kvup_ref, qg_ref, kg_ref, c_ref, s1_ref,
                     s2_ref, q_out, k_out, v_out, *, q_rank, kv_rank, heads, q_scale):
    z = z_ref[...]

    def rms(x, g):
        return x * lax.rsqrt(jnp.mean(x * x, axis=-1, keepdims=True) + NORM_EPS) * g

    q = _dot(rms(z[:, :q_rank], qn_ref[...]).astype(BF16), qup_ref[...])
    kv = _dot(rms(z[:, q_rank:q_rank + kv_rank], kvn_ref[...]).astype(BF16), kvup_ref[...])
    k_rope = z[:, q_rank + kv_rank:q_rank + kv_rank + LANE]
    cos, s1, s2 = c_ref[...], s1_ref[...], s2_ref[...]

    def head_norm_rope(x, g):
        x = x * lax.rsqrt(jnp.sum(x * x, axis=-1, keepdims=True) / MLA_QK + NORM_EPS) * g
        return (x * cos + pltpu.roll(x, MLA_HEAD_PAD - ROPE_FREQ, 1) * s1
                + pltpu.roll(x, ROPE_FREQ, 1) * s2)

    for h in range(heads):
        cols = slice(h * MLA_HEAD_PAD, (h + 1) * MLA_HEAD_PAD)
        q_out[:, cols] = (head_norm_rope(q[:, cols], qg_ref[...]) * q_scale).astype(BF16)
        kh = jnp.concatenate([kv[:, h * MLA_NOPE:(h + 1) * MLA_NOPE], k_rope], axis=1)
        k_out[:, cols] = head_norm_rope(kh, kg_ref[...]).astype(BF16)
    v_out[...] = kv[:, heads * MLA_NOPE:].astype(BF16)


def _mla_prep(z_mla, p, rope):
    t, zw = z_mla.shape
    heads = p["heads"]
    q_rank, kv_rank = p["q_up"].shape[0], p["kv_up"].shape[0]
    bm = _pick(t, (256, 128, 64))
    qw = heads * MLA_HEAD_PAD
    row = lambda w: pl.BlockSpec((bm, w), lambda i: (i, 0))
    const = lambda a: pl.BlockSpec(a.shape, lambda i: (0, 0))
    args = (z_mla, p["q_norm"], p["q_up"], p["kv_norm"], p["kv_up"], p["q_gain"], p["k_gain"])
    return pl.pallas_call(
        functools.partial(_mla_prep_kernel, q_rank=q_rank, kv_rank=kv_rank, heads=heads,
                          q_scale=ATTN_Q_SCALE),
        out_shape=[jax.ShapeDtypeStruct((t, qw), BF16), jax.ShapeDtypeStruct((t, qw), BF16),
                   jax.ShapeDtypeStruct((t, heads * MLA_V_DIM), BF16)],
        grid=(t // bm,),
        in_specs=[row(zw)] + [const(a) for a in args[1:]] + [row(MLA_HEAD_PAD)] * 3,
        out_specs=[row(qw), row(qw), row(heads * MLA_V_DIM)],
        compiler_params=_params(("parallel",)),
        name="mla_prep",
    )(*args, *rope)


def _attn_kernel(q_ref, k_ref, vt_ref, o_ref, *, n_lat, bq, hps, with_ctx):
    def attend(key0, n_keys):
        bk = _pick(n_keys, (768, 512, 256, 128))
        qcols = lambda c: slice(c * MLA_HEAD_PAD, (c + 1) * MLA_HEAD_PAD)
        q = [q_ref[:, qcols(c)] for c in range(hps)]
        keys = lambda j: slice(key0 + j * bk, key0 + (j + 1) * bk)
        scores = lambda j: [_dot_nt(k_ref[keys(j), qcols(c)], q[c]) for c in range(hps)]
        s_next = scores(0)
        m = [jnp.full((1, bq), -jnp.inf, F32)] * hps
        acc = [jnp.zeros((V_ROWS, bq), F32)] * hps
        for j in range(n_keys // bk):
            s = s_next
            if (j + 1) * bk < n_keys:
                s_next = scores(j + 1)
            for c in range(hps):
                m_new = jnp.maximum(m[c], jnp.max(s[c], axis=0, keepdims=True))
                p = jnp.exp2(s[c] - m_new).astype(BF16)
                acc[c] = (jnp.exp2(m[c] - m_new) * acc[c]
                          + _dot(vt_ref[c * V_ROWS:(c + 1) * V_ROWS, keys(j)], p))
                m[c] = m_new
        for c in range(hps):
            o_ref[:, c * MLA_V_DIM:(c + 1) * MLA_V_DIM] = (
                acc[c][:MLA_V_DIM] / acc[c][MLA_V_DIM:MLA_V_DIM + 1]).T.astype(o_ref.dtype)

    n_all = k_ref.shape[0]
    if not with_ctx:
        attend(0, n_all)
        return
    is_latent = pl.program_id(1) < n_lat // bq

    @pl.when(is_latent)
    def _():
        attend(0, n_all)

    @pl.when(jnp.logical_not(is_latent))
    def _():
        attend(n_lat, n_all - n_lat)


def _attention(q, k, vt, heads, n_lat, with_ctx):
    t = k.shape[0]
    n_q = t if with_ctx else n_lat
    bq = _pick(math.gcd(n_lat, t - n_lat), (256, 128, 64))
    hps = 4
    return pl.pallas_call(
        functools.partial(_attn_kernel, n_lat=n_lat, bq=bq, hps=hps, with_ctx=with_ctx),
        out_shape=jax.ShapeDtypeStruct((n_q, heads * MLA_V_DIM), BF16),
        grid=(heads // hps, n_q // bq),
        in_specs=[pl.BlockSpec((bq, hps * MLA_HEAD_PAD), lambda h, i: (i, h)),
                  _resident((t, hps * MLA_HEAD_PAD), lambda h, i: (0, h)),
                  _resident((hps * V_ROWS, t), lambda h, i: (h, 0))],
        out_specs=pl.BlockSpec((bq, hps * MLA_V_DIM), lambda h, i: (i, h)),
        compiler_params=_params(("parallel", "parallel")),
        name="attention",
    )(q, k, vt)


def _pad_cols(w, width):
    return jnp.pad(w, ((0, 0), (0, width - w.shape[1])))


def _rope_tables(n_lat, n_ctx):
    t = jnp.arange(n_lat, dtype=jnp.int32)
    pos = jnp.stack([t // GRID_W, t % GRID_W], axis=-1).astype(F32)
    inv_freq = ROPE_THETA ** (-jnp.arange(ROPE_FREQ, dtype=F32) / ROPE_FREQ)
    ang = pos[:, :, None] * inv_freq
    cos, sin = jnp.cos(ang), jnp.sin(ang)
    zero = jnp.zeros_like(sin)
    lay = lambda first, second: jnp.stack([first, second], axis=2).reshape(n_lat, MLA_ROPE)
    tabs = []
    for rope_part, fill in ((lay(cos, cos), 1.0), (lay(-sin, zero), 0.0), (lay(zero, sin), 0.0)):
        tab = jnp.concatenate([jnp.full((n_lat, MLA_NOPE), fill, F32), rope_part,
                               jnp.full((n_lat, MLA_HEAD_PAD - MLA_QK), fill, F32)], axis=1)
        tabs.append(jnp.concatenate([tab, jnp.full((n_ctx, MLA_HEAD_PAD), fill, F32)], axis=0))
    return tuple(tabs)


def _layer_params(l, w_in, rw_mu, rw_w0, rw_w_up, rw_a0, rw_a_up, rw_g_up, rw_k_k, rw_k_a, rw_r_k,
                  rw_ln_w, rw_ln_b, mla_q_norm, mla_q_up, mla_kv_norm, mla_kv_up, mla_q_gain,
                  mla_k_gain):
    hw = rw_k_k.shape[1]
    ld, la, lg = rw_w_up.shape[2], rw_a_up.shape[2], rw_g_up.shape[1]
    q_rank, kv_rank = mla_q_up.shape[1], mla_kv_up.shape[1]
    heads = mla_kv_up.shape[2] // (MLA_NOPE + MLA_V_DIM)
    rw_cols = 3 * hw + ld + la + lg
    assert ld <= LANE and la <= LANE and lg % LANE == 0 and rw_cols % 4 == 0

    def rw_layout(a, fill=0.0):
        o = 3 * hw
        pad = lambda x: jnp.pad(x, ((0, 0), (0, LANE - x.shape[1])), constant_values=fill)
        return jnp.concatenate([a[:, :o], pad(a[:, o:o + ld]), pad(a[:, o + ld:o + ld + la]),
                                a[:, o + ld + la:rw_cols]], axis=1)

    w = w_in[l]
    mla_w = _pad_cols(w[:, rw_cols:], -(-(q_rank + kv_rank + LANE) // 512) * 512)
    code = np.arange(rw_cols, dtype=np.int32)[None, :] // (rw_cols // 4)
    bd = (np.arange(LANE)[:, None] // RW_HEAD == np.arange(LANE)[None, :] // RW_HEAD)
    pad_rows = lambda x: jnp.pad(x, ((0, 0), (0, LANE - x.shape[1]), (0, 0)))
    return {
        "hw": hw, "heads": heads,
        "w_rw": rw_layout(w[:, :rw_cols]).astype(BF16),
        "w_mla": mla_w.astype(BF16),
        "mu": rw_layout(rw_mu[l][None, :]),
        "code": rw_layout(jnp.asarray(code)),
        "k_k": rw_k_k[l][None, :], "k_a": rw_k_a[l][None, :], "r_k": rw_r_k[l].reshape(1, hw),
        "w0": rw_w0[l], "a0": rw_a0[l],
        "wup": pad_rows(rw_w_up[l]).astype(BF16), "aup": pad_rows(rw_a_up[l]).astype(BF16),
        "gup": rw_g_up[l].astype(BF16),
        "bd": jnp.asarray(bd, F32),
        "ln_w": rw_ln_w[l][None, :], "ln_b": rw_ln_b[l][None, :],
        "q_norm": mla_q_norm[l][None, :], "kv_norm": mla_kv_norm[l][None, :],
        "q_up": jnp.pad(mla_q_up[l].reshape(q_rank, heads, MLA_QK),
                        ((0, 0), (0, 0), (0, MLA_HEAD_PAD - MLA_QK))
                        ).reshape(q_rank, heads * MLA_HEAD_PAD).astype(BF16),
        "kv_up": mla_kv_up[l].reshape(kv_rank, heads, 2, MLA_NOPE).transpose(0, 2, 1, 3)
                             .reshape(kv_rank, 2 * heads * MLA_NOPE).astype(BF16),
        "q_gain": _pad_cols(mla_q_gain[l][None, :], MLA_HEAD_PAD),
        "k_gain": _pad_cols(mla_k_gain[l][None, :], MLA_HEAD_PAD),
    }


def _ffn_half_step(xt, mod, g, w_in, w_out, n_lat):
    h = _norm_mod(xt, g, mod[:, 0], mod[:, 1], n_lat)
    act = _swiglu_matmul(h, w_in)
    return _residual_matmul([act], w_out, xt, FFN_RES * mod[:, 2], n_lat)


def kernel(x, c, ctx, c_ctx, w_mod, b_mod, norm_g, ffn_w_in, ffn_w_out, w_in, w_out, rw_mu, rw_w0,
           rw_w_up, rw_a0, rw_a_up, rw_g_up, rw_k_k, rw_k_a, rw_r_k, rw_ln_w, rw_ln_b, mla_q_norm,
           mla_q_up, mla_kv_norm, mla_kv_up, mla_q_gain, mla_k_gain):
    batch, n_lat, d = x.shape
    n_ctx = ctx.shape[1]
    depth = w_mod.shape[0]
    assert batch == 1 and c.shape[0] == 1 and ctx.shape[0] == 1
    assert n_lat % GRID_W == 0 and n_ctx % GRID_W == 0

    xt = jnp.concatenate([x[0], ctx[0]], axis=0)
    cc = jnp.zeros((8, d), F32).at[0].set(c[0]).at[1].set(c_ctx)
    rope = _rope_tables(n_lat, n_ctx)

    for l in range(depth):
        last = l == depth - 1
        p = _layer_params(l, w_in, rw_mu, rw_w0, rw_w_up, rw_a0, rw_a_up, rw_g_up, rw_k_k, rw_k_a,
                          rw_r_k, rw_ln_w, rw_ln_b, mla_q_norm, mla_q_up, mla_kv_norm, mla_kv_up,
                          mla_q_gain, mla_k_gain)
        mod = _modulation(cc, w_mod[l], b_mod[l])[:2].reshape(2, N_MOD, d)

        xt = _ffn_half_step(xt, mod[:, 0:3], norm_g[l, 0], ffn_w_in[l, 0], ffn_w_out[l, 0], n_lat)

        hz = _norm_mod(xt, norm_g[l, 1], mod[:, 3], mod[:, 4], n_lat)
        z_rw = _matmul(hz, p["w_rw"], bn_candidates=(512, 256, 128))
        z_mla = _matmul(hz, p["w_mla"])

        st = _rw_streams(z_rw, p, n_lat)
        y_f = _wkv(st, 0, n_lat, reverse=False)
        y_b = _wkv(st, 1, n_lat, reverse=True)
        o_rw = _rw_output(y_f, y_b, st, p)

        q, k, v = _mla_prep(z_mla, p, rope)
        vt = jnp.concatenate([v.T.reshape(p["heads"], MLA_V_DIM, -1),
                              jnp.ones((p["heads"], V_ROWS - MLA_V_DIM, v.shape[0]), BF16)], axis=1)
        o_mla = _attention(q, k, vt.reshape(p["heads"] * V_ROWS, -1), p["heads"], n_lat,
                           with_ctx=not last)

        if last:
            xt, o_rw = xt[:n_lat], o_rw[:n_lat]
        xt = _residual_matmul([o_rw, o_mla], w_out[l], xt, mod[:, 5], n_lat)

        xt = _ffn_half_step(xt, mod[:, 6:9], norm_g[l, 2], ffn_w_in[l, 1], ffn_w_out[l, 1], n_lat)
    return xt[:n_lat][None]
```

```python
import functools
import math

import numpy as np
import jax
import jax.numpy as jnp
from jax import lax
from jax.experimental import pallas as pl
from jax.experimental.pallas import tpu as pltpu

F32 = jnp.float32
BF16 = jnp.bfloat16
HI = lax.Precision.HIGHEST

GRID_W = 64
NORM_EPS = 1e-6
FFN_RES = 0.5
RW_HEAD = 64
RW_GN_EPS = 64e-5
RW_DECAY_SCALE = math.exp(-0.5)
MLA_V_DIM = 128
MLA_NOPE = 128
MLA_ROPE = 64
MLA_QK = MLA_NOPE + MLA_ROPE
MLA_HEAD_PAD = 256
ROPE_FREQ = MLA_ROPE // 4
ROPE_THETA = 10000.0
N_MOD = 9
LANE = 128
CHUNK = 64
ATTN_Q_SCALE = math.log2(math.e) / math.sqrt(MLA_QK)
V_ROWS = MLA_V_DIM + 16
VMEM_LIMIT = 60 * 1024 * 1024


def _pick(n, candidates):
    for c in candidates:
        if n % c == 0:
            return c
    raise ValueError(f"no block size for {n} among {candidates}")


def _params(sem):
    return pltpu.CompilerParams(dimension_semantics=sem, vmem_limit_bytes=VMEM_LIMIT)


def _dot(a, b, precision=None):
    return jnp.dot(a, b, preferred_element_type=F32, precision=precision)


def _dot_nt(a, b, precision=None):
    return lax.dot_general(a, b, (((1,), (1,)), ((), ())), preferred_element_type=F32,
                           precision=precision)


def _dot_tn(a, b, precision=None):
    return lax.dot_general(a, b, (((0,), (0,)), ((), ())), preferred_element_type=F32,
                           precision=precision)


def _bdot(a, b):
    return _dot(a.astype(BF16), b.astype(BF16))


def _bdot_nt(a, b):
    return _dot_nt(a.astype(BF16), b.astype(BF16))


def _wspec(lead, block, index_map):
    return pl.BlockSpec((None,) * len(lead) + block, lambda *g: lead + index_map(*g))


def _row_select(row0, n_rows, n_lat, ref):
    rows = row0 + lax.broadcasted_iota(jnp.int32, (n_rows, 1), 0)
    return jnp.where(rows < n_lat, ref[0:1, :], ref[1:2, :])


def _mod_kernel(c_ref, w_ref, b_ref, o_ref):
    c = c_ref[...]
    a = (c * jax.nn.sigmoid(c)).astype(BF16)
    o_ref[...] = _dot(a, w_ref[...].astype(BF16)) + b_ref[...]


def _modulation(cc, w_mod, b_mod, l):
    _, d, n = w_mod.shape
    bn = _pick(n, (512, 256, 128))
    return pl.pallas_call(
        _mod_kernel,
        out_shape=jax.ShapeDtypeStruct((8, n), F32),
        grid=(n // bn,),
        in_specs=[pl.BlockSpec((8, d), lambda j: (0, 0)),
                  _wspec((l,), (d, bn), lambda j: (0, j)),
                  _wspec((l,), (1, bn), lambda j: (0, j))],
        out_specs=pl.BlockSpec((8, bn), lambda j: (0, j)),
        compiler_params=_params(("parallel",)),
        name="modulation",
    )(cc, w_mod, b_mod.reshape(b_mod.shape[0], 1, n))


def _norm_mod_kernel(x_ref, g_ref, sh_ref, sc_ref, o_ref, *, n_lat, bm):
    x = x_ref[...]
    y = x * lax.rsqrt(jnp.mean(x * x, axis=-1, keepdims=True) + NORM_EPS) * g_ref[...]
    row0 = pl.program_id(0) * bm
    sc = _row_select(row0, bm, n_lat, sc_ref)
    sh = _row_select(row0, bm, n_lat, sh_ref)
    o_ref[...] = (y * (1.0 + sc) + sh).astype(o_ref.dtype)


def _norm_mod(x, g, shift, scale, n_lat):
    m, d = x.shape
    bm = _pick(m, (384, 256, 128, 64, 32, 16))
    return pl.pallas_call(
        functools.partial(_norm_mod_kernel, n_lat=n_lat, bm=bm),
        out_shape=jax.ShapeDtypeStruct((m, d), BF16),
        grid=(m // bm,),
        in_specs=[pl.BlockSpec((bm, d), lambda i: (i, 0)),
                  pl.BlockSpec((1, d), lambda i: (0, 0)),
                  pl.BlockSpec((2, d), lambda i: (0, 0)),
                  pl.BlockSpec((2, d), lambda i: (0, 0))],
        out_specs=pl.BlockSpec((bm, d), lambda i: (i, 0)),
        compiler_params=_params(("parallel",)),
        name="norm_mod",
    )(x, g.reshape(1, d), shift, scale)


def _resident(block_shape, index_map):
    return pl.BlockSpec(block_shape, index_map, pipeline_mode=pl.Buffered(1))


def _mm_kernel(a_ref, w_ref, o_ref):
    o_ref[...] = _dot(a_ref[...], w_ref[...].astype(BF16)).astype(o_ref.dtype)


def _matmul(a, w, lead):
    m, k = a.shape
    n = w.shape[-1]
    bm = _pick(m, (1056, 1024, 512, 320, 256, 128, 64))
    bn = _pick(n, (512, 256, 128))
    return pl.pallas_call(
        _mm_kernel,
        out_shape=jax.ShapeDtypeStruct((m, n), F32),
        grid=(m // bm, n // bn),
        in_specs=[_resident((bm, k), lambda i, j: (i, 0)),
                  _wspec(lead, (k, bn), lambda i, j: (0, j))],
        out_specs=pl.BlockSpec((bm, bn), lambda i, j: (i, j)),
        compiler_params=_params(("parallel", "arbitrary")),
        name="matmul",
    )(a, w)


def _swiglu_kernel(a_ref, wg_ref, wu_ref, o_ref):
    a = a_ref[...]
    gate = _dot(a, wg_ref[...].astype(BF16))
    up = _dot(a, wu_ref[...].astype(BF16))
    o_ref[...] = (gate * jax.nn.sigmoid(gate) * up).astype(o_ref.dtype)


def _swiglu_matmul(a, w, lead):
    m, k = a.shape
    f = w.shape[-1] // 2
    bm = _pick(m, (1056, 1024, 512, 320, 256, 128, 64))
    bn = _pick(f, (256, 128))
    nb = f // bn
    return pl.pallas_call(
        _swiglu_kernel,
        out_shape=jax.ShapeDtypeStruct((m, f), BF16),
        grid=(m // bm, nb),
        in_specs=[_resident((bm, k), lambda i, j: (i, 0)),
                  _wspec(lead, (k, bn), lambda i, j: (0, j)),
                  _wspec(lead, (k, bn), lambda i, j: (0, j + nb))],
        out_specs=pl.BlockSpec((bm, bn), lambda i, j: (i, j)),
        compiler_params=_params(("parallel", "arbitrary")),
        name="swiglu_matmul",
    )(a, w, w)


def _residual_kernel(*refs, n_a, n_lat, bm):
    a_refs = refs[:n_a]
    w_refs = refs[n_a:2 * n_a]
    x_ref, g_ref, o_ref = refs[2 * n_a:]
    acc = _dot(a_refs[0][...], w_refs[0][...].astype(BF16))
    for a_ref, w_ref in zip(a_refs[1:], w_refs[1:]):
        acc = acc + _dot(a_ref[...], w_ref[...].astype(BF16))
    gate = _row_select(pl.program_id(0) * bm, bm, n_lat, g_ref)
    o_ref[...] = x_ref[...] + gate * acc


def _residual_matmul(a_list, w, lead, x, gate, n_lat):
    m, n = x.shape
    n_a = len(a_list)
    k = a_list[0].shape[1]
    assert all(a.shape[1] == k for a in a_list) and w.shape[-2] == n_a * k
    bm = _pick(m, (1056, 1024, 512, 320, 256, 128, 64))
    bn = _pick(n, (256, 128))
    a_specs = [_resident((bm, k), lambda i, j: (i, 0)) for _ in a_list]
    w_specs = [_wspec(lead, (k, bn), functools.partial(lambda i, j, p: (p, j), p=p))
               for p in range(n_a)]
    return pl.pallas_call(
        functools.partial(_residual_kernel, n_a=n_a, n_lat=n_lat, bm=bm),
        out_shape=jax.ShapeDtypeStruct((m, n), F32),
        grid=(m // bm, n // bn),
        in_specs=a_specs + w_specs + [pl.BlockSpec((bm, bn), lambda i, j: (i, j)),
                                      pl.BlockSpec((2, bn), lambda i, j: (0, j))],
        out_specs=pl.BlockSpec((bm, bn), lambda i, j: (i, j)),
        compiler_params=_params(("parallel", "arbitrary")),
        name="residual_matmul",
    )(*a_list, *([w] * n_a), x, gate)


def _segsum(x, bd):
    parts = [_dot(x[:, j:j + LANE], bd, precision=HI) for j in range(0, x.shape[1], LANE)]
    return jnp.concatenate(parts, axis=1)


def _streams_kernel(zc_ref, zp_ref, zn_ref, mu_ref, code_ref, kkw_ref, ka_ref, rk_ref, w0_ref,
                    a0_ref, wup_ref, aup_ref, gup_ref, bd_ref,
                    r_out, v_out, kk_out, lw0_out, lw1_out, b0_out, b1_out, kd0_out, kd1_out,
                    g_out, bonus_out, *, n_lat_blocks, n_blocks, hw, ld, la, lg):
    i = pl.program_id(0)
    z = zc_ref[...]
    zp = zp_ref[...]
    zn = zn_ref[...]
    code = code_ref[...]
    rows = lax.broadcasted_iota(jnp.int32, (GRID_W, 1), 0)
    first = rows == 0
    last = rows == GRID_W - 1
    prev_tok = pltpu.roll(z, 1, 0)
    next_tok = pltpu.roll(z, GRID_W - 1, 0)

    left = jnp.where(first, 0.0, prev_tok)
    right = jnp.where(last, 0.0, next_tok)
    up = zp * (i > 0).astype(F32)
    down = zn * (i < n_lat_blocks - 1).astype(F32)
    sh_lat = jnp.where(code == 0, left, jnp.where(code == 1, right, jnp.where(code == 2, up, down)))
    has_prev = (i > n_lat_blocks).astype(F32)
    has_next = (i < n_blocks - 1).astype(F32)
    prev_c = jnp.where(first, zp[GRID_W - 1:GRID_W, :] * has_prev, prev_tok)
    next_c = jnp.where(last, zn[0:1, :] * has_next, next_tok)
    sh_ctx = jnp.where(code < 2, prev_c, next_c)
    is_ctx = (i >= n_lat_blocks).astype(F32)
    shifted = sh_lat + (sh_ctx - sh_lat) * is_ctx
    zs = z + (shifted - z) * mu_ref[...]

    r = zs[:, 0:hw]
    k = zs[:, hw:2 * hw]
    v = zs[:, 2 * hw:3 * hw]
    wd = zs[:, 3 * hw:3 * hw + LANE]
    ad = zs[:, 3 * hw + ld:3 * hw + ld + LANE]
    gd = zs[:, 3 * hw + ld + la:3 * hw + ld + la + lg]
    bd = bd_ref[...]

    kk = k * kkw_ref[...]
    kk = kk / jnp.maximum(jnp.sqrt(_segsum(kk * kk, bd)), 1e-12)
    tw = jnp.tanh(wd).astype(BF16)
    adb = ad.astype(BF16)
    ka = ka_ref[...]
    kds = []
    for d, (lw_out, b_out, kd_out) in enumerate(((lw0_out, b0_out, kd0_out),
                                                 (lw1_out, b1_out, kd1_out))):
        w_logit = w0_ref[d:d + 1, :] + _dot(tw, wup_ref[d])
        lw_out[...] = -RW_DECAY_SCALE * jax.nn.sigmoid(w_logit)
        a = jax.nn.sigmoid(a0_ref[d:d + 1, :] + _dot(adb, aup_ref[d]))
        kd = k * (1.0 + (a - 1.0) * ka)
        kd_out[...] = kd
        b_out[...] = kk * a
        kds.append(kd)
    g_out[...] = _dot(jax.nn.sigmoid(gd).astype(BF16), gup_ref[...])
    kmean = (kds[0] + kds[1]) / 2.0
    bonus_out[...] = _segsum(r * kmean * rk_ref[...], bd) * v
    r_out[...] = r
    v_out[...] = v
    kk_out[...] = kk


def _rw_streams(z, p, n_lat):
    t = z.shape[0]
    hw, zw = p["hw"], p["zw"]
    lg = p["gup"].shape[0]
    n_blocks = t // GRID_W
    n_lat_blocks = n_lat // GRID_W
    row = lambda f: pl.BlockSpec((GRID_W, zw), f)
    const2 = lambda shape: pl.BlockSpec(shape, lambda i: (0, 0))
    const3 = lambda shape: pl.BlockSpec(shape, lambda i: (0, 0, 0))
    out_spec = pl.BlockSpec((GRID_W, hw), lambda i: (i, 0))
    outs = pl.pallas_call(
        functools.partial(_streams_kernel, n_lat_blocks=n_lat_blocks, n_blocks=n_blocks, hw=hw,
                          ld=p["ld"], la=p["la"], lg=lg),
        out_shape=[jax.ShapeDtypeStruct((t, hw), F32)] * 11,
        grid=(n_blocks,),
        in_specs=[row(lambda i: (i, 0)),
                  row(lambda i: (jnp.maximum(i - 1, 0), 0)),
                  row(lambda i: (jnp.minimum(i + 1, n_blocks - 1), 0)),
                  const2((1, zw)), const2((1, zw)),
                  const2((1, hw)), const2((1, hw)), const2((1, hw)),
                  const2((2, hw)), const2((2, hw)),
                  const3((2, LANE, hw)), const3((2, LANE, hw)), const2((lg, hw)),
                  const2((LANE, LANE))],
        out_specs=[out_spec] * 11,
        compiler_params=_params(("parallel",)),
        name="rw_streams",
    )(z, z, z, p["mu"], p["code"], p["k_k"], p["k_a"], p["r_k"], p["w0"], p["a0"],
      p["wup"], p["aup"], p["gup"], p["bd"])
    names = ("r", "v", "kk", "lw0", "lw1", "b0", "b1", "kd0", "kd1", "g", "bonus")
    return dict(zip(names, outs))


def _wkv_kernel(lw_ref, kk_ref, b_ref, kd_ref, r_ref, v_ref, y_ref, st_ref, *, heads, tb, reverse):
    @pl.when(pl.program_id(1) == 0)
    def _():
        st_ref[...] = jnp.zeros(st_ref.shape, F32)

    n_chunks = tb // CHUNK
    t_idx = lax.broadcasted_iota(jnp.int32, (CHUNK, CHUNK), 0)
    s_idx = lax.broadcasted_iota(jnp.int32, (CHUNK, CHUNK), 1)
    strict = (s_idx > t_idx) if reverse else (s_idx < t_idx)
    incl = (s_idx >= t_idx) if reverse else (s_idx <= t_idx)
    eye = (s_idx == t_idx).astype(F32)

    bt = lax.broadcasted_iota(jnp.int32, (tb, tb), 0)
    bs = lax.broadcasted_iota(jnp.int32, (tb, tb), 1)
    before = (bs >= bt) if reverse else (bs <= bt)
    tri = jnp.where(((bt // CHUNK) == (bs // CHUNK)) & before, 1.0, 0.0).astype(BF16)
    lw = lw_ref[...]
    lw_hi = lw.astype(BF16)
    lw_lo = (lw - lw_hi.astype(F32)).astype(BF16)
    cum = _dot(tri, lw_hi) + _dot(tri, lw_lo)
    edge = 0 if reverse else CHUNK - 1
    tot = jnp.concatenate(
        [jnp.broadcast_to(cum[cc * CHUNK + edge:cc * CHUNK + edge + 1], (CHUNK, cum.shape[1]))
         for cc in range(n_chunks)], axis=0)
    kk, b, kd = kk_ref[...], b_ref[...], kd_ref[...]
    e_neg = jnp.exp(-cum)
    e_end = jnp.exp(tot - cum)
    alpha = kk * jnp.exp(cum - lw)
    rho = r_ref[...] * jnp.exp(cum)
    ar_all = (alpha.astype(BF16), rho.astype(BF16))
    bk_all = ((b * e_neg).astype(BF16), (kd * e_neg).astype(BF16))
    beta_e = (b * e_end).astype(BF16)
    kappa_e = (kd * e_end).astype(BF16)
    g_end = jnp.exp(tot)
    v_all = v_ref[...].astype(BF16)

    pairs = [(cc, h) for cc in range(n_chunks) for h in range(heads)]
    rs = lambda cc: slice(cc * CHUNK, (cc + 1) * CHUNK)
    ls = lambda h: slice(h * RW_HEAD, (h + 1) * RW_HEAD)
    cut = lambda x, cc, h: x[rs(cc), ls(h)]
    stack = lambda xs, cc, h: jnp.concatenate([cut(x, cc, h) for x in xs], axis=0)
    m = [_dot_nt(stack(ar_all, cc, h), stack(bk_all, cc, h)) for cc, h in pairs]
    pw = [jnp.where(strict, -x[:CHUNK, :CHUNK], 0.0) for x in m]
    m_akrk = [jnp.concatenate([jnp.where(strict, x[:CHUNK, CHUNK:], 0.0),
                               jnp.where(incl, x[CHUNK:, CHUNK:], 0.0)], axis=0).astype(BF16)
              for x in m]
    m_rb = [jnp.where(incl, x[CHUNK:, :CHUNK], 0.0).astype(BF16) for x in m]
    t_inv = [eye + x for x in pw]
    for _ in range(int(math.log2(CHUNK)) - 1):
        pwb = [x.astype(BF16) for x in pw]
        pw = [_dot(x, x) for x in pwb]
        t_inv = [t + _dot(t.astype(BF16), x.astype(BF16)) for t, x in zip(t_inv, pw)]
    mv = [_dot(x, cut(v_all, cc, h)) for x, (cc, h) in zip(m_akrk, pairs)]
    w12 = [_dot(t.astype(BF16),
                jnp.concatenate([x[:CHUNK].astype(BF16), cut(ar_all[0], cc, h)], axis=1)
                ).astype(BF16)
           for t, x, (cc, h) in zip(t_inv, mv, pairs)]
    mw = [_dot(x, w) for x, w in zip(m_rb, w12)]
    bw = [_dot_tn(cut(beta_e, cc, h), w) for w, (cc, h) in zip(w12, pairs)]
    kv = [_dot_tn(cut(kappa_e, cc, h), cut(v_all, cc, h)) for cc, h in pairs]
    pre = {}
    for i, (cc, h) in enumerate(pairs):
        y1 = mv[i][CHUNK:] - mw[i][:, :RW_HEAD]
        r2 = cut(rho, cc, h) - mw[i][:, RW_HEAD:]
        g_m = kv[i] - bw[i][:, :RW_HEAD]
        p_m = eye * g_end[cc * CHUNK:cc * CHUNK + 1, ls(h)] - bw[i][:, RW_HEAD:]
        pre[cc, h] = (y1, jnp.concatenate([r2, p_m], axis=0).astype(BF16), g_m)

    st = [st_ref[h] for h in range(heads)]
    ys = {}
    for cc in (range(n_chunks - 1, -1, -1) if reverse else range(n_chunks)):
        prod = [_dot(pre[cc, h][1], st[h].astype(BF16)) for h in range(heads)]
        for h in range(heads):
            ys[cc, h] = pre[cc, h][0] + prod[h][:CHUNK]
            st[h] = prod[h][CHUNK:] + pre[cc, h][2]
    for h in range(heads):
        st_ref[h] = st[h]
    for cc in range(n_chunks):
        y_ref[rs(cc), :] = jnp.concatenate([ys[cc, h] for h in range(heads)], axis=1)


def _wkv(st, d, n_lat, reverse):
    lw, b, kd = st[f"lw{d}"], st[f"b{d}"], st[f"kd{d}"]
    t, hw = lw.shape
    heads = _pick(hw // RW_HEAD, (8, 4, 2))
    tb = _pick(math.gcd(n_lat, t - n_lat), (256, 128, 64))
    nb = t // tb
    n_lat_blocks = n_lat // tb
    if reverse:
        blk = lambda h, c: (nb - 1 - c, h)
    else:
        blk = lambda h, c: ((c + n_lat_blocks) % nb, h)
    spec = pl.BlockSpec((tb, heads * RW_HEAD), blk)
    return pl.pallas_call(
        functools.partial(_wkv_kernel, heads=heads, tb=tb, reverse=reverse),
        out_shape=jax.ShapeDtypeStruct((t, hw), F32),
        grid=(hw // (heads * RW_HEAD), nb),
        in_specs=[spec] * 6,
        out_specs=spec,
        scratch_shapes=[pltpu.VMEM((heads, RW_HEAD, RW_HEAD), F32)],
        compiler_params=_params(("parallel", "arbitrary")),
        name="wkv_bwd" if reverse else "wkv_fwd",
    )(lw, st["kk"], b, kd, st["r"], st["v"])


def _rw_out_kernel(yf_ref, yb_ref, bonus_ref, g_ref, lnw_ref, lnb_ref, bd_ref, o_ref):
    y = yf_ref[...] + yb_ref[...]
    bd = bd_ref[...]
    mean = _segsum(y, bd) / RW_HEAD
    yc = y - mean
    var = _segsum(yc * yc, bd) / RW_HEAD
    o = yc * lax.rsqrt(var + RW_GN_EPS) * lnw_ref[...] + lnb_ref[...] + bonus_ref[...]
    o_ref[...] = (o * g_ref[...]).astype(o_ref.dtype)


def _rw_output(y_f, y_b, st, p):
    t, hw = y_f.shape
    bm = _pick(t, (256, 128, 64))
    row = pl.BlockSpec((bm, hw), lambda i: (i, 0))
    vec = pl.BlockSpec((1, hw), lambda i: (0, 0))
    return pl.pallas_call(
        _rw_out_kernel,
        out_shape=jax.ShapeDtypeStruct((t, hw), BF16),
        grid=(t // bm,),
        in_specs=[row, row, row, row, vec, vec, pl.BlockSpec((LANE, LANE), lambda i: (0, 0))],
        out_specs=row,
        compiler_params=_params(("parallel",)),
        name="rw_output",
    )(y_f, y_b, st["bonus"], st["g"], p["ln_w"], p["ln_b"], p["bd"])


def _mla_prep_kernel(z_ref, qn_ref, qup_ref, kvn_ref, kvup_ref, qg_ref, kg_ref, c_ref, s1_ref,
                     s2_ref, q_out, k_out, v_out, *, off, q_rank, kv_rank, heads, q_scale):
    z = z_ref[...]

    def rms(x, g):
        return x * lax.rsqrt(jnp.mean(x * x, axis=-1, keepdims=True) + NORM_EPS) * g

    q = _dot(rms(z[:, off:off + q_rank], qn_ref[...]).astype(BF16), qup_ref[...])
    kv_lat = z[:, off + q_rank:off + q_rank + kv_rank]
    kv = _dot(rms(kv_lat, kvn_ref[...]).astype(BF16), kvup_ref[...])
    k_rope = z[:, off + q_rank + kv_rank:off + q_rank + kv_rank + MLA_ROPE]
    k_rope = jnp.concatenate([k_rope, jnp.zeros((z.shape[0], LANE - MLA_ROPE), F32)], axis=1)
    cos, s1, s2 = c_ref[...], s1_ref[...], s2_ref[...]

    def head_norm_rope(x, g):
        x = x * lax.rsqrt(jnp.sum(x * x, axis=-1, keepdims=True) / MLA_QK + NORM_EPS) * g
        return (x * cos + pltpu.roll(x, MLA_HEAD_PAD - ROPE_FREQ, 1) * s1
                + pltpu.roll(x, ROPE_FREQ, 1) * s2)

    for h in range(heads):
        cols = slice(h * MLA_HEAD_PAD, (h + 1) * MLA_HEAD_PAD)
        q_out[:, cols] = (head_norm_rope(q[:, cols], qg_ref[...]) * q_scale).astype(BF16)
        kh = jnp.concatenate([kv[:, h * MLA_NOPE:(h + 1) * MLA_NOPE], k_rope], axis=1)
        k_out[:, cols] = head_norm_rope(kh, kg_ref[...]).astype(BF16)
    v_out[...] = kv[:, heads * MLA_NOPE:].astype(BF16)


def _mla_prep(z, rw_cols, p, rope):
    t, n = z.shape
    heads = p["heads"]
    q_rank, kv_rank = p["q_up"].shape[0], p["kv_up"].shape[0]
    zw = min(w for w in range(LANE, n + 1, LANE) if n % w == 0 and n - w <= rw_cols)
    bm = _pick(t, (256, 128, 64))
    qw = heads * MLA_HEAD_PAD
    row = lambda w: pl.BlockSpec((bm, w), lambda i: (i, 0))
    const = lambda a: pl.BlockSpec(a.shape, lambda i: (0, 0))
    args = (z, p["q_norm"], p["q_up"], p["kv_norm"], p["kv_up"], p["q_gain"], p["k_gain"])
    return pl.pallas_call(
        functools.partial(_mla_prep_kernel, off=rw_cols - (n - zw), q_rank=q_rank, kv_rank=kv_rank,
                          heads=heads, q_scale=ATTN_Q_SCALE),
        out_shape=[jax.ShapeDtypeStruct((t, qw), BF16), jax.ShapeDtypeStruct((t, qw), BF16),
                   jax.ShapeDtypeStruct((t, heads * MLA_V_DIM), BF16)],
        grid=(t // bm,),
        in_specs=([pl.BlockSpec((bm, zw), lambda i: (i, n // zw - 1))]
                  + [const(a) for a in args[1:]] + [row(MLA_HEAD_PAD)] * 3),
        out_specs=[row(qw), row(qw), row(heads * MLA_V_DIM)],
        compiler_params=_params(("parallel",)),
        name="mla_prep",
    )(*args, *rope)


def _attn_kernel(q_ref, k_ref, vt_ref, o_ref, *, n_lat, bq, hps, with_ctx):
    def attend(key0, n_keys):
        bk = _pick(n_keys, (768, 512, 256, 128))
        qcols = lambda c: slice(c * MLA_HEAD_PAD, (c + 1) * MLA_HEAD_PAD)
        q = [q_ref[:, qcols(c)] for c in range(hps)]
        keys = lambda j: slice(key0 + j * bk, key0 + (j + 1) * bk)
        scores = lambda j: [_dot_nt(k_ref[keys(j), qcols(c)], q[c]) for c in range(hps)]
        s_next = scores(0)
        m = [jnp.full((1, bq), -jnp.inf, F32)] * hps
        acc = [jnp.zeros((V_ROWS, bq), F32)] * hps
        for j in range(n_keys // bk):
            s = s_next
            if (j + 1) * bk < n_keys:
                s_next = scores(j + 1)
            for c in range(hps):
                m_new = jnp.maximum(m[c], jnp.max(s[c], axis=0, keepdims=True))
                p = jnp.exp2(s[c] - m_new).astype(BF16)
                acc[c] = (jnp.exp2(m[c] - m_new) * acc[c]
                          + _dot(vt_ref[c * V_ROWS:(c + 1) * V_ROWS, keys(j)], p))
                m[c] = m_new
        for c in range(hps):
            o_ref[:, c * MLA_V_DIM:(c + 1) * MLA_V_DIM] = (
                acc[c][:MLA_V_DIM] / acc[c][MLA_V_DIM:MLA_V_DIM + 1]).T.astype(o_ref.dtype)

    n_all = k_ref.shape[0]
    if not with_ctx:
        attend(0, n_all)
        return
    is_latent = pl.program_id(1) < n_lat // bq

    @pl.when(is_latent)
    def _():
        attend(0, n_all)

    @pl.when(jnp.logical_not(is_latent))
    def _():
        attend(n_lat, n_all - n_lat)


def _attention(q, k, vt, heads, n_lat, with_ctx):
    t = k.shape[0]
    n_q = t if with_ctx else n_lat
    bq = _pick(math.gcd(n_lat, t - n_lat), (256, 128, 64))
    hps = 4
    return pl.pallas_call(
        functools.partial(_attn_kernel, n_lat=n_lat, bq=bq, hps=hps, with_ctx=with_ctx),
        out_shape=jax.ShapeDtypeStruct((n_q, heads * MLA_V_DIM), BF16),
        grid=(heads // hps, n_q // bq),
        in_specs=[pl.BlockSpec((bq, hps * MLA_HEAD_PAD), lambda h, i: (i, h)),
                  _resident((t, hps * MLA_HEAD_PAD), lambda h, i: (0, h)),
                  _resident((hps * V_ROWS, t), lambda h, i: (h, 0))],
        out_specs=pl.BlockSpec((bq, hps * MLA_V_DIM), lambda h, i: (i, h)),
        compiler_params=_params(("parallel", "parallel")),
        name="attention",
    )(q, k, vt)


def _pad_cols(w, width):
    return jnp.pad(w, ((0, 0), (0, width - w.shape[1])))


def _rope_tables(n_lat, n_ctx):
    t = jnp.arange(n_lat, dtype=jnp.int32)
    pos = jnp.stack([t // GRID_W, t % GRID_W], axis=-1).astype(F32)
    inv_freq = ROPE_THETA ** (-jnp.arange(ROPE_FREQ, dtype=F32) / ROPE_FREQ)
    ang = pos[:, :, None] * inv_freq
    cos, sin = jnp.cos(ang), jnp.sin(ang)
    zero = jnp.zeros_like(sin)
    lay = lambda first, second: jnp.stack([first, second], axis=2).reshape(n_lat, MLA_ROPE)
    tabs = []
    for rope_part, fill in ((lay(cos, cos), 1.0), (lay(-sin, zero), 0.0), (lay(zero, sin), 0.0)):
        tab = jnp.concatenate([jnp.full((n_lat, MLA_NOPE), fill, F32), rope_part,
                               jnp.full((n_lat, MLA_HEAD_PAD - MLA_QK), fill, F32)], axis=1)
        tabs.append(jnp.concatenate([tab, jnp.full((n_ctx, MLA_HEAD_PAD), fill, F32)], axis=0))
    return tuple(tabs)


def _layer_params(l, rw_mu, rw_w0, rw_w_up, rw_a0, rw_a_up, rw_g_up, rw_k_k, rw_k_a, rw_r_k,
                  rw_ln_w, rw_ln_b, mla_q_norm, mla_q_up, mla_kv_norm, mla_kv_up, mla_q_gain,
                  mla_k_gain):
    hw = rw_k_k.shape[1]
    ld, la, lg = rw_w_up.shape[2], rw_a_up.shape[2], rw_g_up.shape[1]
    q_rank, kv_rank = mla_q_up.shape[1], mla_kv_up.shape[1]
    heads = mla_kv_up.shape[2] // (MLA_NOPE + MLA_V_DIM)
    rw_cols = 3 * hw + ld + la + lg
    zw = -(-rw_cols // LANE) * LANE
    assert ld <= LANE and la <= LANE and rw_cols % 4 == 0 and 3 * hw + ld + LANE <= zw
    code = np.minimum(np.arange(zw, dtype=np.int32) // (rw_cols // 4), 3)[None, :]
    bd = (np.arange(LANE)[:, None] // RW_HEAD == np.arange(LANE)[None, :] // RW_HEAD)
    pad_rows = lambda x: jnp.pad(x, ((0, 0), (0, LANE - x.shape[1]), (0, 0)))
    return {
        "hw": hw, "heads": heads, "rw_cols": rw_cols, "zw": zw, "ld": ld, "la": la,
        "mu": _pad_cols(rw_mu[l][None, :], zw),
        "code": jnp.asarray(code),
        "k_k": rw_k_k[l][None, :], "k_a": rw_k_a[l][None, :], "r_k": rw_r_k[l].reshape(1, hw),
        "w0": rw_w0[l], "a0": rw_a0[l],
        "wup": pad_rows(rw_w_up[l]).astype(BF16), "aup": pad_rows(rw_a_up[l]).astype(BF16),
        "gup": rw_g_up[l].astype(BF16),
        "bd": jnp.asarray(bd, F32),
        "ln_w": rw_ln_w[l][None, :], "ln_b": rw_ln_b[l][None, :],
        "q_norm": mla_q_norm[l][None, :], "kv_norm": mla_kv_norm[l][None, :],
        "q_up": jnp.pad(mla_q_up[l].reshape(q_rank, heads, MLA_QK),
                        ((0, 0), (0, 0), (0, MLA_HEAD_PAD - MLA_QK))
                        ).reshape(q_rank, heads * MLA_HEAD_PAD).astype(BF16),
        "kv_up": mla_kv_up[l].reshape(kv_rank, heads, 2, MLA_NOPE).transpose(0, 2, 1, 3)
                             .reshape(kv_rank, 2 * heads * MLA_NOPE).astype(BF16),
        "q_gain": _pad_cols(mla_q_gain[l][None, :], MLA_HEAD_PAD),
        "k_gain": _pad_cols(mla_k_gain[l][None, :], MLA_HEAD_PAD),
    }


def _ffn_half_step(xt, mod, g, ffn_w_in, ffn_w_out, lead, n_lat):
    h = _norm_mod(xt, g, mod[:, 0], mod[:, 1], n_lat)
    act = _swiglu_matmul(h, ffn_w_in, lead)
    return _residual_matmul([act], ffn_w_out, lead, xt, FFN_RES * mod[:, 2], n_lat)


def kernel(x, c, ctx, c_ctx, w_mod, b_mod, norm_g, ffn_w_in, ffn_w_out, w_in, w_out, rw_mu, rw_w0,
           rw_w_up, rw_a0, rw_a_up, rw_g_up, rw_k_k, rw_k_a, rw_r_k, rw_ln_w, rw_ln_b, mla_q_norm,
           mla_q_up, mla_kv_norm, mla_kv_up, mla_q_gain, mla_k_gain):
    batch, n_lat, d = x.shape
    n_ctx = ctx.shape[1]
    depth = w_mod.shape[0]
    assert batch == 1 and c.shape[0] == 1 and ctx.shape[0] == 1
    assert n_lat % GRID_W == 0 and n_ctx % GRID_W == 0

    xt = jnp.concatenate([x[0], ctx[0]], axis=0)
    cc = jnp.zeros((8, d), F32).at[0].set(c[0]).at[1].set(c_ctx)
    rope = _rope_tables(n_lat, n_ctx)

    for l in range(depth):
        last = l == depth - 1
        p = _layer_params(l, rw_mu, rw_w0, rw_w_up, rw_a0, rw_a_up, rw_g_up, rw_k_k, rw_k_a,
                          rw_r_k, rw_ln_w, rw_ln_b, mla_q_norm, mla_q_up, mla_kv_norm, mla_kv_up,
                          mla_q_gain, mla_k_gain)
        mod = _modulation(cc, w_mod, b_mod, l)[:2].reshape(2, N_MOD, d)

        xt = _ffn_half_step(xt, mod[:, 0:3], norm_g[l, 0], ffn_w_in, ffn_w_out, (l, 0), n_lat)

        hz = _norm_mod(xt, norm_g[l, 1], mod[:, 3], mod[:, 4], n_lat)
        z = _matmul(hz, w_in, (l,))

        st = _rw_streams(z, p, n_lat)
        y_f = _wkv(st, 0, n_lat, reverse=False)
        y_b = _wkv(st, 1, n_lat, reverse=True)
        o_rw = _rw_output(y_f, y_b, st, p)

        q, k, v = _mla_prep(z, p["rw_cols"], p, rope)
        vt = jnp.concatenate([v.T.reshape(p["heads"], MLA_V_DIM, -1),
                              jnp.ones((p["heads"], V_ROWS - MLA_V_DIM, v.shape[0]), BF16)], axis=1)
        o_mla = _attention(q, k, vt.reshape(p["heads"] * V_ROWS, -1), p["heads"], n_lat,
                           with_ctx=not last)

        if last:
            xt, o_rw = xt[:n_lat], o_rw[:n_lat]
        xt = _residual_matmul([o_rw, o_mla], w_out, (l,), xt, mod[:, 5], n_lat)

        xt = _ffn_half_step(xt, mod[:, 6:9], norm_g[l, 2], ffn_w_in, ffn_w_out, (l, 1), n_lat)
    return xt[:n_lat][None]
```

```python
import functools
import math

import numpy as np
import jax
import jax.numpy as jnp
from jax import lax
from jax.experimental import pallas as pl
from jax.experimental.pallas import tpu as pltpu

F32 = jnp.float32
BF16 = jnp.bfloat16
HI = lax.Precision.HIGHEST

GRID_W = 64
NORM_EPS = 1e-6
FFN_RES = 0.5
RW_HEAD = 64
RW_GN_EPS = 64e-5
RW_DECAY_SCALE = math.exp(-0.5)
MLA_V_DIM = 128
MLA_NOPE = 128
MLA_ROPE = 64
MLA_QK = MLA_NOPE + MLA_ROPE
MLA_HEAD_PAD = 256
ROPE_FREQ = MLA_ROPE // 4
ROPE_THETA = 10000.0
N_MOD = 9
LANE = 128
CHUNK = 64
ATTN_Q_SCALE = math.log2(math.e) / math.sqrt(MLA_QK)
V_ROWS = MLA_V_DIM + 16
VMEM_LIMIT = 60 * 1024 * 1024


def _pick(n, candidates):
    for c in candidates:
        if n % c == 0:
            return c
    raise ValueError(f"no block size for {n} among {candidates}")


def _params(sem):
    return pltpu.CompilerParams(dimension_semantics=sem, vmem_limit_bytes=VMEM_LIMIT)


def _dot(a, b, precision=None):
    return jnp.dot(a, b, preferred_element_type=F32, precision=precision)


def _dot_nt(a, b, precision=None):
    return lax.dot_general(a, b, (((1,), (1,)), ((), ())), preferred_element_type=F32,
                           precision=precision)


def _dot_tn(a, b, precision=None):
    return lax.dot_general(a, b, (((0,), (0,)), ((), ())), preferred_element_type=F32,
                           precision=precision)


def _bdot(a, b):
    return _dot(a.astype(BF16), b.astype(BF16))


def _bdot_nt(a, b):
    return _dot_nt(a.astype(BF16), b.astype(BF16))


def _wspec(lead, block, index_map):
    return pl.BlockSpec((None,) * len(lead) + block, lambda *g: lead + index_map(*g))


def _row_select(row0, n_rows, n_lat, ref):
    rows = row0 + lax.broadcasted_iota(jnp.int32, (n_rows, 1), 0)
    return jnp.where(rows < n_lat, ref[0:1, :], ref[1:2, :])


def _mod_kernel(c_ref, w_ref, b_ref, o_ref):
    c = c_ref[...]
    a = (c * jax.nn.sigmoid(c)).astype(BF16)
    o_ref[...] = _dot(a, w_ref[...].astype(BF16)) + b_ref[...]


def _modulation(cc, w_mod, b_mod, l):
    _, d, n = w_mod.shape
    bn = _pick(n, (512, 256, 128))
    return pl.pallas_call(
        _mod_kernel,
        out_shape=jax.ShapeDtypeStruct((8, n), F32),
        grid=(n // bn,),
        in_specs=[pl.BlockSpec((8, d), lambda j: (0, 0)),
                  _wspec((l,), (d, bn), lambda j: (0, j)),
                  _wspec((l,), (1, bn), lambda j: (0, j))],
        out_specs=pl.BlockSpec((8, bn), lambda j: (0, j)),
        compiler_params=_params(("parallel",)),
        name="modulation",
    )(cc, w_mod, b_mod.reshape(b_mod.shape[0], 1, n))


def _norm_mod_kernel(x_ref, g_ref, sh_ref, sc_ref, o_ref, *, n_lat, bm):
    x = x_ref[...]
    y = x * lax.rsqrt(jnp.mean(x * x, axis=-1, keepdims=True) + NORM_EPS) * g_ref[...]
    row0 = pl.program_id(0) * bm
    sc = _row_select(row0, bm, n_lat, sc_ref)
    sh = _row_select(row0, bm, n_lat, sh_ref)
    o_ref[...] = (y * (1.0 + sc) + sh).astype(o_ref.dtype)


def _norm_mod(x, g, shift, scale, n_lat):
    m, d = x.shape
    bm = _pick(m, (384, 256, 128, 64, 32, 16))
    return pl.pallas_call(
        functools.partial(_norm_mod_kernel, n_lat=n_lat, bm=bm),
        out_shape=jax.ShapeDtypeStruct((m, d), BF16),
        grid=(m // bm,),
        in_specs=[pl.BlockSpec((bm, d), lambda i: (i, 0)),
                  pl.BlockSpec((1, d), lambda i: (0, 0)),
                  pl.BlockSpec((2, d), lambda i: (0, 0)),
                  pl.BlockSpec((2, d), lambda i: (0, 0))],
        out_specs=pl.BlockSpec((bm, d), lambda i: (i, 0)),
        compiler_params=_params(("parallel",)),
        name="norm_mod",
    )(x, g.reshape(1, d), shift, scale)


def _resident(block_shape, index_map):
    return pl.BlockSpec(block_shape, index_map, pipeline_mode=pl.Buffered(1))


def _mm_kernel(a_ref, w_ref, o_ref):
    o_ref[...] = _dot(a_ref[...], w_ref[...].astype(BF16)).astype(o_ref.dtype)


def _matmul(a, w, lead):
    m, k = a.shape
    n = w.shape[-1]
    bm = _pick(m, (1056, 1024, 512, 320, 256, 128, 64))
    bn = _pick(n, (512, 256, 128))
    return pl.pallas_call(
        _mm_kernel,
        out_shape=jax.ShapeDtypeStruct((m, n), F32),
        grid=(m // bm, n // bn),
        in_specs=[_resident((bm, k), lambda i, j: (i, 0)),
                  _wspec(lead, (k, bn), lambda i, j: (0, j))],
        out_specs=pl.BlockSpec((bm, bn), lambda i, j: (i, j)),
        compiler_params=_params(("parallel", "arbitrary")),
        name="matmul",
    )(a, w)


def _swiglu_kernel(a_ref, wg_ref, wu_ref, o_ref):
    a = a_ref[...]
    gate = _dot(a, wg_ref[...].astype(BF16))
    up = _dot(a, wu_ref[...].astype(BF16))
    o_ref[...] = (gate * jax.nn.sigmoid(gate) * up).astype(o_ref.dtype)


def _swiglu_matmul(a, w, lead):
    m, k = a.shape
    f = w.shape[-1] // 2
    bm = _pick(m, (1056, 1024, 512, 320, 256, 128, 64))
    bn = _pick(f, (256, 128))
    nb = f // bn
    return pl.pallas_call(
        _swiglu_kernel,
        out_shape=jax.ShapeDtypeStruct((m, f), BF16),
        grid=(m // bm, nb),
        in_specs=[_resident((bm, k), lambda i, j: (i, 0)),
                  _wspec(lead, (k, bn), lambda i, j: (0, j)),
                  _wspec(lead, (k, bn), lambda i, j: (0, j + nb))],
        out_specs=pl.BlockSpec((bm, bn), lambda i, j: (i, j)),
        compiler_params=_params(("parallel", "arbitrary")),
        name="swiglu_matmul",
    )(a, w, w)


def _residual_kernel(*refs, n_a, n_lat, bm):
    a_refs = refs[:n_a]
    w_refs = refs[n_a:2 * n_a]
    x_ref, g_ref, o_ref = refs[2 * n_a:]
    acc = _dot(a_refs[0][...], w_refs[0][...].astype(BF16))
    for a_ref, w_ref in zip(a_refs[1:], w_refs[1:]):
        acc = acc + _dot(a_ref[...], w_ref[...].astype(BF16))
    gate = _row_select(pl.program_id(0) * bm, bm, n_lat, g_ref)
    o_ref[...] = x_ref[...] + gate * acc


def _residual_matmul(a_list, w, lead, x, gate, n_lat, rows=None):
    m, n = (rows or x.shape[0]), x.shape[1]
    n_a = len(a_list)
    k = a_list[0].shape[1]
    assert all(a.shape[1] == k for a in a_list) and w.shape[-2] == n_a * k
    bm = _pick(m, (1056, 1024, 512, 320, 256, 128, 64))
    bn = _pick(n, (256, 128))
    a_specs = [_resident((bm, k), lambda i, j: (i, 0)) for _ in a_list]
    w_specs = [_wspec(lead, (k, bn), functools.partial(lambda i, j, p: (p, j), p=p))
               for p in range(n_a)]
    return pl.pallas_call(
        functools.partial(_residual_kernel, n_a=n_a, n_lat=n_lat, bm=bm),
        out_shape=jax.ShapeDtypeStruct((m, n), F32),
        grid=(m // bm, n // bn),
        in_specs=a_specs + w_specs + [pl.BlockSpec((bm, bn), lambda i, j: (i, j)),
                                      pl.BlockSpec((2, bn), lambda i, j: (0, j))],
        out_specs=pl.BlockSpec((bm, bn), lambda i, j: (i, j)),
        compiler_params=_params(("parallel", "arbitrary")),
        name="residual_matmul",
    )(*a_list, *([w] * n_a), x, gate)


def _segsum(x, bd):
    parts = [_dot(x[:, j:j + LANE], bd, precision=HI) for j in range(0, x.shape[1], LANE)]
    return jnp.concatenate(parts, axis=1)


def _streams_kernel(zc_ref, zp_ref, zn_ref, mu_ref, code_ref, kkw_ref, ka_ref, rk_ref, w0_ref,
                    a0_ref, wup_ref, aup_ref, gup_ref, bd_ref,
                    r_out, v_out, kk_out, lw0_out, lw1_out, b0_out, b1_out, kd0_out, kd1_out,
                    g_out, bonus_out, *, rb, n_lat_blocks, n_blocks, hw, ld, la, lg):
    i = pl.program_id(0)
    z = zc_ref[...]
    zp = zp_ref[...]
    zn = zn_ref[...]
    code = code_ref[...]
    rows = lax.broadcasted_iota(jnp.int32, (rb, 1), 0)
    prev_tok = pltpu.roll(z, 1, 0)
    next_tok = pltpu.roll(z, rb - 1, 0)

    left = jnp.where(rows % GRID_W == 0, 0.0, prev_tok)
    right = jnp.where(rows % GRID_W == GRID_W - 1, 0.0, next_tok)
    up = jnp.concatenate([zp * (i > 0).astype(F32), z[:rb - GRID_W]], axis=0)
    down = jnp.concatenate([z[GRID_W:], zn * (i < n_lat_blocks - 1).astype(F32)], axis=0)
    sh_lat = jnp.where(code == 0, left, jnp.where(code == 1, right, jnp.where(code == 2, up, down)))
    has_prev = (i > n_lat_blocks).astype(F32)
    has_next = (i < n_blocks - 1).astype(F32)
    prev_c = jnp.where(rows == 0, zp[GRID_W - 1:GRID_W, :] * has_prev, prev_tok)
    next_c = jnp.where(rows == rb - 1, zn[0:1, :] * has_next, next_tok)
    sh_ctx = jnp.where(code < 2, prev_c, next_c)
    is_ctx = (i >= n_lat_blocks).astype(F32)
    shifted = sh_lat + (sh_ctx - sh_lat) * is_ctx
    zs = z + (shifted - z) * mu_ref[...]

    r = zs[:, 0:hw]
    k = zs[:, hw:2 * hw]
    v = zs[:, 2 * hw:3 * hw]
    wd = zs[:, 3 * hw:3 * hw + LANE]
    ad = zs[:, 3 * hw + ld:3 * hw + ld + LANE]
    gd = zs[:, 3 * hw + ld + la:3 * hw + ld + la + lg]
    bd = bd_ref[...]

    kk = k * kkw_ref[...]
    kk = kk / jnp.maximum(jnp.sqrt(_segsum(kk * kk, bd)), 1e-12)
    tw = jnp.tanh(wd).astype(BF16)
    adb = ad.astype(BF16)
    ka = ka_ref[...]
    kds = []
    for d, (lw_out, b_out, kd_out) in enumerate(((lw0_out, b0_out, kd0_out),
                                                 (lw1_out, b1_out, kd1_out))):
        w_logit = w0_ref[d:d + 1, :] + _dot(tw, wup_ref[d])
        lw_out[...] = -RW_DECAY_SCALE * jax.nn.sigmoid(w_logit)
        a = jax.nn.sigmoid(a0_ref[d:d + 1, :] + _dot(adb, aup_ref[d]))
        kd = k * (1.0 + (a - 1.0) * ka)
        kd_out[...] = kd.astype(kd_out.dtype)
        b_out[...] = (kk * a).astype(b_out.dtype)
        kds.append(kd)
    g_out[...] = _dot(jax.nn.sigmoid(gd).astype(BF16), gup_ref[...]).astype(g_out.dtype)
    kmean = (kds[0] + kds[1]) / 2.0
    bonus_out[...] = (_segsum(r * kmean * rk_ref[...], bd) * v).astype(bonus_out.dtype)
    r_out[...] = r.astype(r_out.dtype)
    v_out[...] = v.astype(v_out.dtype)
    kk_out[...] = kk.astype(kk_out.dtype)


def _rw_streams(z, p, n_lat):
    t = z.shape[0]
    hw, zw = p["hw"], p["zw"]
    lg = p["gup"].shape[0]
    rb = _pick(math.gcd(n_lat, t - n_lat), (256, 128, 64))
    n_blocks = t // rb
    n_lat_blocks = n_lat // rb
    halo = lambda f: pl.BlockSpec((GRID_W, zw), f)
    per = rb // GRID_W
    const2 = lambda shape: pl.BlockSpec(shape, lambda i: (0, 0))
    const3 = lambda shape: pl.BlockSpec(shape, lambda i: (0, 0, 0))
    out_spec = pl.BlockSpec((rb, hw), lambda i: (i, 0))
    out_dtypes = {"lw0": F32, "lw1": F32}
    names = ("r", "v", "kk", "lw0", "lw1", "b0", "b1", "kd0", "kd1", "g", "bonus")
    outs = pl.pallas_call(
        functools.partial(_streams_kernel, rb=rb, n_lat_blocks=n_lat_blocks, n_blocks=n_blocks,
                          hw=hw, ld=p["ld"], la=p["la"], lg=lg),
        out_shape=[jax.ShapeDtypeStruct((t, hw), out_dtypes.get(n, BF16)) for n in names],
        grid=(n_blocks,),
        in_specs=[pl.BlockSpec((rb, zw), lambda i: (i, 0)),
                  halo(lambda i: (jnp.maximum(i * per - 1, 0), 0)),
                  halo(lambda i: (jnp.minimum((i + 1) * per, t // GRID_W - 1), 0)),
                  const2((1, zw)), const2((1, zw)),
                  const2((1, hw)), const2((1, hw)), const2((1, hw)),
                  const2((2, hw)), const2((2, hw)),
                  const3((2, LANE, hw)), const3((2, LANE, hw)), const2((lg, hw)),
                  const2((LANE, LANE))],
        out_specs=[out_spec] * 11,
        compiler_params=_params(("parallel",)),
        name="rw_streams",
    )(z, z, z, p["mu"], p["code"], p["k_k"], p["k_a"], p["r_k"], p["w0"], p["a0"],
      p["wup"], p["aup"], p["gup"], p["bd"])
    return dict(zip(names, outs))


def _wkv_kernel(lw_ref, kk_ref, b_ref, kd_ref, r_ref, v_ref, y_ref, st_ref, *, heads, tb, reverse):
    @pl.when(pl.program_id(1) == 0)
    def _():
        st_ref[...] = jnp.zeros(st_ref.shape, F32)

    n_chunks = tb // CHUNK
    t_idx = lax.broadcasted_iota(jnp.int32, (CHUNK, CHUNK), 0)
    s_idx = lax.broadcasted_iota(jnp.int32, (CHUNK, CHUNK), 1)
    strict = (s_idx > t_idx) if reverse else (s_idx < t_idx)
    incl = (s_idx >= t_idx) if reverse else (s_idx <= t_idx)
    eye = (s_idx == t_idx).astype(F32)

    bt = lax.broadcasted_iota(jnp.int32, (tb, tb), 0)
    bs = lax.broadcasted_iota(jnp.int32, (tb, tb), 1)
    before = (bs >= bt) if reverse else (bs <= bt)
    tri = jnp.where(((bt // CHUNK) == (bs // CHUNK)) & before, 1.0, 0.0).astype(BF16)
    lw = lw_ref[...]
    lw_hi = lw.astype(BF16)
    lw_lo = (lw - lw_hi.astype(F32)).astype(BF16)
    cum = _dot(tri, lw_hi) + _dot(tri, lw_lo)
    edge = 0 if reverse else CHUNK - 1
    tot = jnp.concatenate(
        [jnp.broadcast_to(cum[cc * CHUNK + edge:cc * CHUNK + edge + 1], (CHUNK, cum.shape[1]))
         for cc in range(n_chunks)], axis=0)
    kk, b, kd = (x[...].astype(F32) for x in (kk_ref, b_ref, kd_ref))
    e_neg = jnp.exp(-cum)
    e_end = jnp.exp(tot - cum)
    alpha = kk * jnp.exp(cum - lw)
    rho = r_ref[...].astype(F32) * jnp.exp(cum)
    ar_all = (alpha.astype(BF16), rho.astype(BF16))
    bk_all = ((b * e_neg).astype(BF16), (kd * e_neg).astype(BF16))
    beta_e = (b * e_end).astype(BF16)
    kappa_e = (kd * e_end).astype(BF16)
    g_end = jnp.exp(tot)
    v_all = v_ref[...].astype(BF16)

    pairs = [(cc, h) for cc in range(n_chunks) for h in range(heads)]
    rs = lambda cc: slice(cc * CHUNK, (cc + 1) * CHUNK)
    ls = lambda h: slice(h * RW_HEAD, (h + 1) * RW_HEAD)
    cut = lambda x, cc, h: x[rs(cc), ls(h)]
    stack = lambda xs, cc, h: jnp.concatenate([cut(x, cc, h) for x in xs], axis=0)
    m = [_dot_nt(stack(ar_all, cc, h), stack(bk_all, cc, h)) for cc, h in pairs]
    pw = [jnp.where(strict, -x[:CHUNK, :CHUNK], 0.0) for x in m]
    m_akrk = [jnp.concatenate([jnp.where(strict, x[:CHUNK, CHUNK:], 0.0),
                               jnp.where(incl, x[CHUNK:, CHUNK:], 0.0)], axis=0).astype(BF16)
              for x in m]
    m_rb = [jnp.where(incl, x[CHUNK:, :CHUNK], 0.0).astype(BF16) for x in m]
    t_inv = [eye + x for x in pw]
    for _ in range(int(math.log2(CHUNK)) - 1):
        pwb = [x.astype(BF16) for x in pw]
        pw = [_dot(x, x) for x in pwb]
        t_inv = [t + _dot(t.astype(BF16), x.astype(BF16)) for t, x in zip(t_inv, pw)]
    mv = [_dot(x, cut(v_all, cc, h)) for x, (cc, h) in zip(m_akrk, pairs)]
    w12 = [_dot(t.astype(BF16),
                jnp.concatenate([x[:CHUNK].astype(BF16), cut(ar_all[0], cc, h)], axis=1)
                ).astype(BF16)
           for t, x, (cc, h) in zip(t_inv, mv, pairs)]
    mw = [_dot(x, w) for x, w in zip(m_rb, w12)]
    bw = [_dot_tn(cut(beta_e, cc, h), w) for w, (cc, h) in zip(w12, pairs)]
    kv = [_dot_tn(cut(kappa_e, cc, h), cut(v_all, cc, h)) for cc, h in pairs]
    pre = {}
    for i, (cc, h) in enumerate(pairs):
        y1 = mv[i][CHUNK:] - mw[i][:, :RW_HEAD]
        r2 = cut(rho, cc, h) - mw[i][:, RW_HEAD:]
        g_m = kv[i] - bw[i][:, :RW_HEAD]
        p_m = eye * g_end[cc * CHUNK:cc * CHUNK + 1, ls(h)] - bw[i][:, RW_HEAD:]
        pre[cc, h] = (y1, jnp.concatenate([r2, p_m], axis=0).astype(BF16), g_m)

    st = [st_ref[h] for h in range(heads)]
    ys = {}
    for cc in (range(n_chunks - 1, -1, -1) if reverse else range(n_chunks)):
        prod = [_dot(pre[cc, h][1], st[h].astype(BF16)) for h in range(heads)]
        for h in range(heads):
            ys[cc, h] = pre[cc, h][0] + prod[h][:CHUNK]
            st[h] = prod[h][CHUNK:] + pre[cc, h][2]
    for h in range(heads):
        st_ref[h] = st[h]
    for cc in range(n_chunks):
        y_ref[rs(cc), :] = jnp.concatenate([ys[cc, h] for h in range(heads)], axis=1)


def _wkv(st, d, n_lat, reverse):
    lw, b, kd = st[f"lw{d}"], st[f"b{d}"], st[f"kd{d}"]
    t, hw = lw.shape
    heads = _pick(hw // RW_HEAD, (8, 4, 2))
    tb = _pick(math.gcd(n_lat, t - n_lat), (256, 128, 64))
    nb = t // tb
    n_lat_blocks = n_lat // tb
    if reverse:
        blk = lambda h, c: (nb - 1 - c, h)
    else:
        blk = lambda h, c: ((c + n_lat_blocks) % nb, h)
    spec = pl.BlockSpec((tb, heads * RW_HEAD), blk)
    return pl.pallas_call(
        functools.partial(_wkv_kernel, heads=heads, tb=tb, reverse=reverse),
        out_shape=jax.ShapeDtypeStruct((t, hw), F32),
        grid=(hw // (heads * RW_HEAD), nb),
        in_specs=[spec] * 6,
        out_specs=spec,
        scratch_shapes=[pltpu.VMEM((heads, RW_HEAD, RW_HEAD), F32)],
        compiler_params=_params(("parallel", "arbitrary")),
        name="wkv_bwd" if reverse else "wkv_fwd",
    )(lw, st["kk"], b, kd, st["r"], st["v"])


def _rw_out_kernel(yf_ref, yb_ref, bonus_ref, g_ref, lnw_ref, lnb_ref, bd_ref, o_ref):
    y = yf_ref[...] + yb_ref[...]
    bd = bd_ref[...]
    mean = _segsum(y, bd) / RW_HEAD
    yc = y - mean
    var = _segsum(yc * yc, bd) / RW_HEAD
    o = yc * lax.rsqrt(var + RW_GN_EPS) * lnw_ref[...] + lnb_ref[...] + bonus_ref[...].astype(F32)
    o_ref[...] = (o * g_ref[...].astype(F32)).astype(o_ref.dtype)


def _rw_output(y_f, y_b, st, p):
    t, hw = y_f.shape
    bm = _pick(t, (256, 128, 64))
    row = pl.BlockSpec((bm, hw), lambda i: (i, 0))
    vec = pl.BlockSpec((1, hw), lambda i: (0, 0))
    return pl.pallas_call(
        _rw_out_kernel,
        out_shape=jax.ShapeDtypeStruct((t, hw), BF16),
        grid=(t // bm,),
        in_specs=[row, row, row, row, vec, vec, pl.BlockSpec((LANE, LANE), lambda i: (0, 0))],
        out_specs=row,
        compiler_params=_params(("parallel",)),
        name="rw_output",
    )(y_f, y_b, st["bonus"], st["g"], p["ln_w"], p["ln_b"], p["bd"])


def _mla_prep_kernel(z_ref, qn_ref, qup_ref, kvn_ref, kvup_ref, qg_ref, kg_ref, c_ref, s1_ref,
                     s2_ref, q_out, k_out, v_out, *, off, q_rank, kv_rank, heads, q_scale):
    z = z_ref[...]

    def rms(x, g):
        return x * lax.rsqrt(jnp.mean(x * x, axis=-1, keepdims=True) + NORM_EPS) * g

    q = _dot(rms(z[:, off:off + q_rank], qn_ref[...]).astype(BF16), qup_ref[...])
    kv_lat = z[:, off + q_rank:off + q_rank + kv_rank]
    kv = _dot(rms(kv_lat, kvn_ref[...]).astype(BF16), kvup_ref[...])
    k_rope = z[:, off + q_rank + kv_rank:off + q_rank + kv_rank + MLA_ROPE]
    k_rope = jnp.concatenate([k_rope, jnp.zeros((z.shape[0], LANE - MLA_ROPE), F32)], axis=1)
    cos, s1, s2 = c_ref[...], s1_ref[...], s2_ref[...]

    def head_norm_rope(x, g):
        x = x * lax.rsqrt(jnp.sum(x * x, axis=-1, keepdims=True) / MLA_QK + NORM_EPS) * g
        return (x * cos + pltpu.roll(x, MLA_HEAD_PAD - ROPE_FREQ, 1) * s1
                + pltpu.roll(x, ROPE_FREQ, 1) * s2)

    for h in range(heads):
        cols = slice(h * MLA_HEAD_PAD, (h + 1) * MLA_HEAD_PAD)
        q_out[:, cols] = (head_norm_rope(q[:, cols], qg_ref[...]) * q_scale).astype(BF16)
        kh = jnp.concatenate([kv[:, h * MLA_NOPE:(h + 1) * MLA_NOPE], k_rope], axis=1)
        k_out[:, cols] = head_norm_rope(kh, kg_ref[...]).astype(BF16)
    v_out[...] = kv[:, heads * MLA_NOPE:].astype(BF16)


def _mla_prep(z, rw_cols, p, rope):
    t, n = z.shape
    heads = p["heads"]
    q_rank, kv_rank = p["q_up"].shape[0], p["kv_up"].shape[0]
    zw = min(w for w in range(LANE, n + 1, LANE) if n % w == 0 and n - w <= rw_cols)
    bm = _pick(t, (256, 128, 64))
    qw = heads * MLA_HEAD_PAD
    row = lambda w: pl.BlockSpec((bm, w), lambda i: (i, 0))
    const = lambda a: pl.BlockSpec(a.shape, lambda i: (0, 0))
    args = (z, p["q_norm"], p["q_up"], p["kv_norm"], p["kv_up"], p["q_gain"], p["k_gain"])
    return pl.pallas_call(
        functools.partial(_mla_prep_kernel, off=rw_cols - (n - zw), q_rank=q_rank, kv_rank=kv_rank,
                          heads=heads, q_scale=ATTN_Q_SCALE),
        out_shape=[jax.ShapeDtypeStruct((t, qw), BF16), jax.ShapeDtypeStruct((t, qw), BF16),
                   jax.ShapeDtypeStruct((t, heads * MLA_V_DIM), BF16)],
        grid=(t // bm,),
        in_specs=([pl.BlockSpec((bm, zw), lambda i: (i, n // zw - 1))]
                  + [const(a) for a in args[1:]] + [row(MLA_HEAD_PAD)] * 3),
        out_specs=[row(qw), row(qw), row(heads * MLA_V_DIM)],
        compiler_params=_params(("parallel",)),
        name="mla_prep",
    )(*args, *rope)


def _attn_kernel(q_ref, k_ref, vt_ref, o_ref, *, n_lat, bq, hps, with_ctx):
    def attend(key0, n_keys):
        bk = _pick(n_keys, (1408, 768, 512, 256, 128))
        qcols = lambda c: slice(c * MLA_HEAD_PAD, (c + 1) * MLA_HEAD_PAD)
        q = [q_ref[:, qcols(c)] for c in range(hps)]
        keys = lambda j: slice(key0 + j * bk, key0 + (j + 1) * bk)
        scores = lambda j: [_dot_nt(k_ref[keys(j), qcols(c)], q[c]) for c in range(hps)]
        s_next = scores(0)
        m = [jnp.full((1, bq), -jnp.inf, F32)] * hps
        acc = [jnp.zeros((V_ROWS, bq), F32)] * hps
        for j in range(n_keys // bk):
            s = s_next
            if (j + 1) * bk < n_keys:
                s_next = scores(j + 1)
            for c in range(hps):
                m_new = jnp.maximum(m[c], jnp.max(s[c], axis=0, keepdims=True))
                p = jnp.exp2(s[c] - m_new).astype(BF16)
                acc[c] = (jnp.exp2(m[c] - m_new) * acc[c]
                          + _dot(vt_ref[c * V_ROWS:(c + 1) * V_ROWS, keys(j)], p))
                m[c] = m_new
        for c in range(hps):
            o_ref[:, c * MLA_V_DIM:(c + 1) * MLA_V_DIM] = (
                acc[c][:MLA_V_DIM] / acc[c][MLA_V_DIM:MLA_V_DIM + 1]).T.astype(o_ref.dtype)

    n_all = k_ref.shape[0]
    if not with_ctx:
        attend(0, n_all)
        return
    is_latent = pl.program_id(1) < n_lat // bq

    @pl.when(is_latent)
    def _():
        attend(0, n_all)

    @pl.when(jnp.logical_not(is_latent))
    def _():
        attend(n_lat, n_all - n_lat)


def _attention(q, k, vt, heads, n_lat, with_ctx):
    t = k.shape[0]
    n_q = t if with_ctx else n_lat
    bq = _pick(math.gcd(n_lat, t - n_lat), (256, 128, 64))
    hps = 4
    return pl.pallas_call(
        functools.partial(_attn_kernel, n_lat=n_lat, bq=bq, hps=hps, with_ctx=with_ctx),
        out_shape=jax.ShapeDtypeStruct((n_q, heads * MLA_V_DIM), BF16),
        grid=(heads // hps, n_q // bq),
        in_specs=[pl.BlockSpec((bq, hps * MLA_HEAD_PAD), lambda h, i: (i, h)),
                  _resident((t, hps * MLA_HEAD_PAD), lambda h, i: (0, h)),
                  _resident((hps * V_ROWS, t), lambda h, i: (h, 0))],
        out_specs=pl.BlockSpec((bq, hps * MLA_V_DIM), lambda h, i: (i, h)),
        compiler_params=_params(("parallel", "parallel")),
        name="attention",
    )(q, k, vt)


def _pad_cols(w, width):
    return jnp.pad(w, ((0, 0), (0, width - w.shape[1])))


def _rope_tables(n_lat, n_ctx):
    t = jnp.arange(n_lat, dtype=jnp.int32)
    pos = jnp.stack([t // GRID_W, t % GRID_W], axis=-1).astype(F32)
    inv_freq = ROPE_THETA ** (-jnp.arange(ROPE_FREQ, dtype=F32) / ROPE_FREQ)
    ang = pos[:, :, None] * inv_freq
    cos, sin = jnp.cos(ang), jnp.sin(ang)
    zero = jnp.zeros_like(sin)
    lay = lambda first, second: jnp.stack([first, second], axis=2).reshape(n_lat, MLA_ROPE)
    tabs = []
    for rope_part, fill in ((lay(cos, cos), 1.0), (lay(-sin, zero), 0.0), (lay(zero, sin), 0.0)):
        tab = jnp.concatenate([jnp.full((n_lat, MLA_NOPE), fill, F32), rope_part,
                               jnp.full((n_lat, MLA_HEAD_PAD - MLA_QK), fill, F32)], axis=1)
        tabs.append(jnp.concatenate([tab, jnp.full((n_ctx, MLA_HEAD_PAD), fill, F32)], axis=0))
    return tuple(tabs)


def _layer_params(l, rw_mu, rw_w0, rw_w_up, rw_a0, rw_a_up, rw_g_up, rw_k_k, rw_k_a, rw_r_k,
                  rw_ln_w, rw_ln_b, mla_q_norm, mla_q_up, mla_kv_norm, mla_kv_up, mla_q_gain,
                  mla_k_gain):
    hw = rw_k_k.shape[1]
    ld, la, lg = rw_w_up.shape[2], rw_a_up.shape[2], rw_g_up.shape[1]
    q_rank, kv_rank = mla_q_up.shape[1], mla_kv_up.shape[1]
    heads = mla_kv_up.shape[2] // (MLA_NOPE + MLA_V_DIM)
    rw_cols = 3 * hw + ld + la + lg
    zw = -(-rw_cols // LANE) * LANE
    assert ld <= LANE and la <= LANE and rw_cols % 4 == 0 and 3 * hw + ld + LANE <= zw
    code = np.minimum(np.arange(zw, dtype=np.int32) // (rw_cols // 4), 3)[None, :]
    bd = (np.arange(LANE)[:, None] // RW_HEAD == np.arange(LANE)[None, :] // RW_HEAD)
    pad_rows = lambda x: jnp.pad(x, ((0, 0), (0, LANE - x.shape[1]), (0, 0)))
    return {
        "hw": hw, "heads": heads, "rw_cols": rw_cols, "zw": zw, "ld": ld, "la": la,
        "mu": _pad_cols(rw_mu[l][None, :], zw),
        "code": jnp.asarray(code),
        "k_k": rw_k_k[l][None, :], "k_a": rw_k_a[l][None, :], "r_k": rw_r_k[l].reshape(1, hw),
        "w0": rw_w0[l], "a0": rw_a0[l],
        "wup": pad_rows(rw_w_up[l]).astype(BF16), "aup": pad_rows(rw_a_up[l]).astype(BF16),
        "gup": rw_g_up[l].astype(BF16),
        "bd": jnp.asarray(bd, F32),
        "ln_w": rw_ln_w[l][None, :], "ln_b": rw_ln_b[l][None, :],
        "q_norm": mla_q_norm[l][None, :], "kv_norm": mla_kv_norm[l][None, :],
        "q_up": jnp.pad(mla_q_up[l].reshape(q_rank, heads, MLA_QK),
                        ((0, 0), (0, 0), (0, MLA_HEAD_PAD - MLA_QK))
                        ).reshape(q_rank, heads * MLA_HEAD_PAD).astype(BF16),
        "kv_up": mla_kv_up[l].reshape(kv_rank, heads, 2, MLA_NOPE).transpose(0, 2, 1, 3)
                             .reshape(kv_rank, 2 * heads * MLA_NOPE).astype(BF16),
        "q_gain": _pad_cols(mla_q_gain[l][None, :], MLA_HEAD_PAD),
        "k_gain": _pad_cols(mla_k_gain[l][None, :], MLA_HEAD_PAD),
    }


def _ffn_half_step(xt, mod, g, ffn_w_in, ffn_w_out, lead, n_lat):
    h = _norm_mod(xt, g, mod[:, 0], mod[:, 1], n_lat)
    act = _swiglu_matmul(h, ffn_w_in, lead)
    return _residual_matmul([act], ffn_w_out, lead, xt, FFN_RES * mod[:, 2], n_lat)


def kernel(x, c, ctx, c_ctx, w_mod, b_mod, norm_g, ffn_w_in, ffn_w_out, w_in, w_out, rw_mu, rw_w0,
           rw_w_up, rw_a0, rw_a_up, rw_g_up, rw_k_k, rw_k_a, rw_r_k, rw_ln_w, rw_ln_b, mla_q_norm,
           mla_q_up, mla_kv_norm, mla_kv_up, mla_q_gain, mla_k_gain):
    batch, n_lat, d = x.shape
    n_ctx = ctx.shape[1]
    depth = w_mod.shape[0]
    assert batch == 1 and c.shape[0] == 1 and ctx.shape[0] == 1
    assert n_lat % GRID_W == 0 and n_ctx % GRID_W == 0

    xt = jnp.concatenate([x[0], ctx[0]], axis=0)
    cc = jnp.zeros((8, d), F32).at[0].set(c[0]).at[1].set(c_ctx)
    rope = _rope_tables(n_lat, n_ctx)

    for l in range(depth):
        last = l == depth - 1
        p = _layer_params(l, rw_mu, rw_w0, rw_w_up, rw_a0, rw_a_up, rw_g_up, rw_k_k, rw_k_a,
                          rw_r_k, rw_ln_w, rw_ln_b, mla_q_norm, mla_q_up, mla_kv_norm, mla_kv_up,
                          mla_q_gain, mla_k_gain)
        mod = _modulation(cc, w_mod, b_mod, l)[:2].reshape(2, N_MOD, d)

        xt = _ffn_half_step(xt, mod[:, 0:3], norm_g[l, 0], ffn_w_in, ffn_w_out, (l, 0), n_lat)

        hz = _norm_mod(xt, norm_g[l, 1], mod[:, 3], mod[:, 4], n_lat)
        z = _matmul(hz, w_in, (l,))

        st = _rw_streams(z, p, n_lat)
        y_f = _wkv(st, 0, n_lat, reverse=False)
        y_b = _wkv(st, 1, n_lat, reverse=True)
        o_rw = _rw_output(y_f, y_b, st, p)

        q, k, v = _mla_prep(z, p["rw_cols"], p, rope)
        vt = jnp.concatenate([v.T.reshape(p["heads"], MLA_V_DIM, -1),
                              jnp.ones((p["heads"], V_ROWS - MLA_V_DIM, v.shape[0]), BF16)], axis=1)
        o_mla = _attention(q, k, vt.reshape(p["heads"] * V_ROWS, -1), p["heads"], n_lat,
                           with_ctx=not last)

        xt = _residual_matmul([o_rw, o_mla], w_out, (l,), xt, mod[:, 5], n_lat,
                              rows=n_lat if last else None)

        xt = _ffn_half_step(xt, mod[:, 6:9], norm_g[l, 2], ffn_w_in, ffn_w_out, (l, 1), n_lat)
    return xt[:n_lat][None]
```

```python
import functools
import math

import numpy as np
import jax
import jax.numpy as jnp
from jax import lax
from jax.experimental import pallas as pl
from jax.experimental.pallas import tpu as pltpu

F32 = jnp.float32
BF16 = jnp.bfloat16
HI = lax.Precision.HIGHEST

GRID_W = 64
NORM_EPS = 1e-6
FFN_RES = 0.5
RW_HEAD = 64
RW_GN_EPS = 64e-5
RW_DECAY_SCALE = math.exp(-0.5)
MLA_V_DIM = 128
MLA_NOPE = 128
MLA_ROPE = 64
MLA_QK = MLA_NOPE + MLA_ROPE
MLA_HEAD_PAD = 256
ROPE_FREQ = MLA_ROPE // 4
ROPE_THETA = 10000.0
N_MOD = 9
LANE = 128
CHUNK = 64
ATTN_Q_SCALE = math.log2(math.e) / math.sqrt(MLA_QK)
V_ROWS = MLA_V_DIM + 16
VMEM_LIMIT = 60 * 1024 * 1024


def _pick(n, candidates):
    for c in candidates:
        if n % c == 0:
            return c
    raise ValueError(f"no block size for {n} among {candidates}")


def _params(sem):
    return pltpu.CompilerParams(dimension_semantics=sem, vmem_limit_bytes=VMEM_LIMIT)


def _dot(a, b, precision=None):
    return jnp.dot(a, b, preferred_element_type=F32, precision=precision)


def _dot_nt(a, b, precision=None):
    return lax.dot_general(a, b, (((1,), (1,)), ((), ())), preferred_element_type=F32,
                           precision=precision)


def _dot_tn(a, b, precision=None):
    return lax.dot_general(a, b, (((0,), (0,)), ((), ())), preferred_element_type=F32,
                           precision=precision)


def _bdot(a, b):
    return _dot(a.astype(BF16), b.astype(BF16))


def _bdot_nt(a, b):
    return _dot_nt(a.astype(BF16), b.astype(BF16))


def _wspec(lead, block, index_map):
    return pl.BlockSpec((None,) * len(lead) + block, lambda *g: lead + index_map(*g))


def _row_select(row0, n_rows, n_lat, ref):
    rows = row0 + lax.broadcasted_iota(jnp.int32, (n_rows, 1), 0)
    return jnp.where(rows < n_lat, ref[0:1, :], ref[1:2, :])


def _mod_kernel(c_ref, w_ref, b_ref, o_ref):
    c = c_ref[...]
    a = (c * jax.nn.sigmoid(c)).astype(BF16)
    o_ref[...] = _dot(a, w_ref[...].astype(BF16)) + b_ref[...]


def _modulation(cc, w_mod, b_mod, l):
    _, d, n = w_mod.shape
    bn = _pick(n, (512, 256, 128))
    return pl.pallas_call(
        _mod_kernel,
        out_shape=jax.ShapeDtypeStruct((8, n), F32),
        grid=(n // bn,),
        in_specs=[pl.BlockSpec((8, d), lambda j: (0, 0)),
                  _wspec((l,), (d, bn), lambda j: (0, j)),
                  _wspec((l,), (1, bn), lambda j: (0, j))],
        out_specs=pl.BlockSpec((8, bn), lambda j: (0, j)),
        compiler_params=_params(("parallel",)),
        name="modulation",
    )(cc, w_mod, b_mod.reshape(b_mod.shape[0], 1, n))


def _norm_mod_kernel(x_ref, g_ref, sh_ref, sc_ref, o_ref, *, n_lat, bm):
    x = x_ref[...]
    y = x * lax.rsqrt(jnp.mean(x * x, axis=-1, keepdims=True) + NORM_EPS) * g_ref[...]
    row0 = pl.program_id(0) * bm
    sc = _row_select(row0, bm, n_lat, sc_ref)
    sh = _row_select(row0, bm, n_lat, sh_ref)
    o_ref[...] = (y * (1.0 + sc) + sh).astype(o_ref.dtype)


def _norm_mod(x, g, shift, scale, n_lat):
    m, d = x.shape
    bm = _pick(m, (384, 256, 128, 64, 32, 16))
    return pl.pallas_call(
        functools.partial(_norm_mod_kernel, n_lat=n_lat, bm=bm),
        out_shape=jax.ShapeDtypeStruct((m, d), BF16),
        grid=(m // bm,),
        in_specs=[pl.BlockSpec((bm, d), lambda i: (i, 0)),
                  pl.BlockSpec((1, d), lambda i: (0, 0)),
                  pl.BlockSpec((2, d), lambda i: (0, 0)),
                  pl.BlockSpec((2, d), lambda i: (0, 0))],
        out_specs=pl.BlockSpec((bm, d), lambda i: (i, 0)),
        compiler_params=_params(("parallel",)),
        name="norm_mod",
    )(x, g.reshape(1, d), shift, scale)


def _resident(block_shape, index_map):
    return pl.BlockSpec(block_shape, index_map, pipeline_mode=pl.Buffered(1))


def _mm_kernel(a_ref, w_ref, o_ref):
    o_ref[...] = _dot(a_ref[...], w_ref[...].astype(BF16)).astype(o_ref.dtype)


def _matmul(a, w, lead):
    m, k = a.shape
    n = w.shape[-1]
    bm = _pick(m, (1056, 1024, 512, 320, 256, 128, 64))
    bn = _pick(n, (512, 256, 128))
    return pl.pallas_call(
        _mm_kernel,
        out_shape=jax.ShapeDtypeStruct((m, n), F32),
        grid=(m // bm, n // bn),
        in_specs=[_resident((bm, k), lambda i, j: (i, 0)),
                  _wspec(lead, (k, bn), lambda i, j: (0, j))],
        out_specs=pl.BlockSpec((bm, bn), lambda i, j: (i, j)),
        compiler_params=_params(("parallel", "arbitrary")),
        name="matmul",
    )(a, w)


def _swiglu_kernel(a_ref, wg_ref, wu_ref, o_ref):
    a = a_ref[...]
    gate = _dot(a, wg_ref[...].astype(BF16))
    up = _dot(a, wu_ref[...].astype(BF16))
    o_ref[...] = (gate * jax.nn.sigmoid(gate) * up).astype(o_ref.dtype)


def _swiglu_matmul(a, w, lead):
    m, k = a.shape
    f = w.shape[-1] // 2
    bm = _pick(m, (1056, 1024, 512, 320, 256, 128, 64))
    bn = _pick(f, (512, 256, 128))
    nb = f // bn
    return pl.pallas_call(
        _swiglu_kernel,
        out_shape=jax.ShapeDtypeStruct((m, f), BF16),
        grid=(m // bm, nb),
        in_specs=[_resident((bm, k), lambda i, j: (i, 0)),
                  _wspec(lead, (k, bn), lambda i, j: (0, j)),
                  _wspec(lead, (k, bn), lambda i, j: (0, j + nb))],
        out_specs=pl.BlockSpec((bm, bn), lambda i, j: (i, j)),
        compiler_params=_params(("parallel", "arbitrary")),
        name="swiglu_matmul",
    )(a, w, w)


def _residual_kernel(*refs, n_a, n_lat, bm):
    a_refs = refs[:n_a]
    w_refs = refs[n_a:2 * n_a]
    x_ref, g_ref, o_ref = refs[2 * n_a:]
    acc = _dot(a_refs[0][...], w_refs[0][...].astype(BF16))
    for a_ref, w_ref in zip(a_refs[1:], w_refs[1:]):
        acc = acc + _dot(a_ref[...], w_ref[...].astype(BF16))
    gate = _row_select(pl.program_id(0) * bm, bm, n_lat, g_ref)
    o_ref[...] = x_ref[...] + gate * acc


def _residual_matmul(a_list, w, lead, x, gate, n_lat, rows=None):
    m, n = (rows or x.shape[0]), x.shape[1]
    n_a = len(a_list)
    k = a_list[0].shape[1]
    assert all(a.shape[1] == k for a in a_list) and w.shape[-2] == n_a * k
    bm = _pick(m, (1056, 1024, 512, 320, 256, 128, 64))
    bn = _pick(n, (256, 128))
    a_specs = [_resident((bm, k), lambda i, j: (i, 0)) for _ in a_list]
    w_specs = [_wspec(lead, (k, bn), functools.partial(lambda i, j, p: (p, j), p=p))
               for p in range(n_a)]
    return pl.pallas_call(
        functools.partial(_residual_kernel, n_a=n_a, n_lat=n_lat, bm=bm),
        out_shape=jax.ShapeDtypeStruct((m, n), F32),
        grid=(m // bm, n // bn),
        in_specs=a_specs + w_specs + [pl.BlockSpec((bm, bn), lambda i, j: (i, j)),
                                      pl.BlockSpec((2, bn), lambda i, j: (0, j))],
        out_specs=pl.BlockSpec((bm, bn), lambda i, j: (i, j)),
        compiler_params=_params(("parallel", "arbitrary")),
        name="residual_matmul",
    )(*a_list, *([w] * n_a), x, gate)


def _segsum(x, bd):
    hi = x.astype(BF16)
    r1 = x - hi.astype(F32)
    mid = r1.astype(BF16)
    lo = (r1 - mid.astype(F32)).astype(BF16)
    parts = [_dot(hi[:, j:j + LANE], bd) + _dot(mid[:, j:j + LANE], bd) + _dot(lo[:, j:j + LANE], bd)
             for j in range(0, x.shape[1], LANE)]
    return jnp.concatenate(parts, axis=1)


def _streams_kernel(zc_ref, zp_ref, zn_ref, mu_ref, code_ref, kkw_ref, ka_ref, rk_ref, w0_ref,
                    a0_ref, wup_ref, aup_ref, gup_ref, bd_ref,
                    r_out, v_out, kk_out, lw0_out, lw1_out, b0_out, b1_out, kd0_out, kd1_out,
                    g_out, bonus_out, *, rb, n_lat_blocks, n_blocks, hw, ld, la, lg):
    i = pl.program_id(0)
    z = zc_ref[...]
    zp = zp_ref[...]
    zn = zn_ref[...]
    code = code_ref[...]
    rows = lax.broadcasted_iota(jnp.int32, (rb, 1), 0)
    prev_tok = pltpu.roll(z, 1, 0)
    next_tok = pltpu.roll(z, rb - 1, 0)

    left = jnp.where(rows % GRID_W == 0, 0.0, prev_tok)
    right = jnp.where(rows % GRID_W == GRID_W - 1, 0.0, next_tok)
    up = jnp.concatenate([zp * (i > 0).astype(F32), z[:rb - GRID_W]], axis=0)
    down = jnp.concatenate([z[GRID_W:], zn * (i < n_lat_blocks - 1).astype(F32)], axis=0)
    sh_lat = jnp.where(code == 0, left, jnp.where(code == 1, right, jnp.where(code == 2, up, down)))
    has_prev = (i > n_lat_blocks).astype(F32)
    has_next = (i < n_blocks - 1).astype(F32)
    prev_c = jnp.where(rows == 0, zp[GRID_W - 1:GRID_W, :] * has_prev, prev_tok)
    next_c = jnp.where(rows == rb - 1, zn[0:1, :] * has_next, next_tok)
    sh_ctx = jnp.where(code < 2, prev_c, next_c)
    is_ctx = (i >= n_lat_blocks).astype(F32)
    shifted = sh_lat + (sh_ctx - sh_lat) * is_ctx
    zs = z + (shifted - z) * mu_ref[...]

    r = zs[:, 0:hw]
    k = zs[:, hw:2 * hw]
    v = zs[:, 2 * hw:3 * hw]
    wd = zs[:, 3 * hw:3 * hw + LANE]
    ad = zs[:, 3 * hw + ld:3 * hw + ld + LANE]
    gd = zs[:, 3 * hw + ld + la:3 * hw + ld + la + lg]
    bd = bd_ref[...]

    kk = k * kkw_ref[...]
    kk = kk / jnp.maximum(jnp.sqrt(_segsum(kk * kk, bd)), 1e-12)
    tw = jnp.tanh(wd).astype(BF16)
    adb = ad.astype(BF16)
    ka = ka_ref[...]
    kds = []
    for d, (lw_out, b_out, kd_out) in enumerate(((lw0_out, b0_out, kd0_out),
                                                 (lw1_out, b1_out, kd1_out))):
        w_logit = w0_ref[d:d + 1, :] + _dot(tw, wup_ref[d])
        lw_out[...] = -RW_DECAY_SCALE * jax.nn.sigmoid(w_logit)
        a = jax.nn.sigmoid(a0_ref[d:d + 1, :] + _dot(adb, aup_ref[d]))
        kd = k * (1.0 + (a - 1.0) * ka)
        kd_out[...] = kd.astype(kd_out.dtype)
        b_out[...] = (kk * a).astype(b_out.dtype)
        kds.append(kd)
    g_out[...] = _dot(jax.nn.sigmoid(gd).astype(BF16), gup_ref[...]).astype(g_out.dtype)
    kmean = (kds[0] + kds[1]) / 2.0
    bonus_out[...] = (_segsum(r * kmean * rk_ref[...], bd) * v).astype(bonus_out.dtype)
    r_out[...] = r.astype(r_out.dtype)
    v_out[...] = v.astype(v_out.dtype)
    kk_out[...] = kk.astype(kk_out.dtype)


def _rw_streams(z, p, n_lat):
    t = z.shape[0]
    hw, zw = p["hw"], p["zw"]
    lg = p["gup"].shape[0]
    rb = _pick(math.gcd(n_lat, t - n_lat), (256, 128, 64))
    n_blocks = t // rb
    n_lat_blocks = n_lat // rb
    halo = lambda f: pl.BlockSpec((GRID_W, zw), f)
    per = rb // GRID_W
    const2 = lambda shape: pl.BlockSpec(shape, lambda i: (0, 0))
    const3 = lambda shape: pl.BlockSpec(shape, lambda i: (0, 0, 0))
    out_spec = pl.BlockSpec((rb, hw), lambda i: (i, 0))
    out_dtypes = {"lw0": F32, "lw1": F32}
    names = ("r", "v", "kk", "lw0", "lw1", "b0", "b1", "kd0", "kd1", "g", "bonus")
    outs = pl.pallas_call(
        functools.partial(_streams_kernel, rb=rb, n_lat_blocks=n_lat_blocks, n_blocks=n_blocks,
                          hw=hw, ld=p["ld"], la=p["la"], lg=lg),
        out_shape=[jax.ShapeDtypeStruct((t, hw), out_dtypes.get(n, BF16)) for n in names],
        grid=(n_blocks,),
        in_specs=[pl.BlockSpec((rb, zw), lambda i: (i, 0)),
                  halo(lambda i: (jnp.maximum(i * per - 1, 0), 0)),
                  halo(lambda i: (jnp.minimum((i + 1) * per, t // GRID_W - 1), 0)),
                  const2((1, zw)), const2((1, zw)),
                  const2((1, hw)), const2((1, hw)), const2((1, hw)),
                  const2((2, hw)), const2((2, hw)),
                  const3((2, LANE, hw)), const3((2, LANE, hw)), const2((lg, hw)),
                  const2((LANE, LANE))],
        out_specs=[out_spec] * 11,
        compiler_params=_params(("parallel",)),
        name="rw_streams",
    )(z, z, z, p["mu"], p["code"], p["k_k"], p["k_a"], p["r_k"], p["w0"], p["a0"],
      p["wup"], p["aup"], p["gup"], p["bd"])
    return dict(zip(names, outs))


def _wkv_kernel(lw_ref, kk_ref, b_ref, kd_ref, r_ref, v_ref, y_ref, st_ref, *, heads, tb, reverse):
    @pl.when(pl.program_id(1) == 0)
    def _():
        st_ref[...] = jnp.zeros(st_ref.shape, F32)

    n_chunks = tb // CHUNK
    t_idx = lax.broadcasted_iota(jnp.int32, (CHUNK, CHUNK), 0)
    s_idx = lax.broadcasted_iota(jnp.int32, (CHUNK, CHUNK), 1)
    strict = (s_idx > t_idx) if reverse else (s_idx < t_idx)
    incl = (s_idx >= t_idx) if reverse else (s_idx <= t_idx)
    eye = (s_idx == t_idx).astype(F32)

    bt = lax.broadcasted_iota(jnp.int32, (tb, tb), 0)
    bs = lax.broadcasted_iota(jnp.int32, (tb, tb), 1)
    before = (bs >= bt) if reverse else (bs <= bt)
    tri = jnp.where(((bt // CHUNK) == (bs // CHUNK)) & before, 1.0, 0.0).astype(BF16)
    lw = lw_ref[...]
    lw_hi = lw.astype(BF16)
    lw_lo = (lw - lw_hi.astype(F32)).astype(BF16)
    cum = _dot(tri, lw_hi) + _dot(tri, lw_lo)
    edge = 0 if reverse else CHUNK - 1
    tot = jnp.concatenate(
        [jnp.broadcast_to(cum[cc * CHUNK + edge:cc * CHUNK + edge + 1], (CHUNK, cum.shape[1]))
         for cc in range(n_chunks)], axis=0)
    kk, b, kd = (x[...].astype(F32) for x in (kk_ref, b_ref, kd_ref))
    e_neg = jnp.exp(-cum)
    e_end = jnp.exp(tot - cum)
    alpha = kk * jnp.exp(cum - lw)
    rho = r_ref[...].astype(F32) * jnp.exp(cum)
    ar_all = (alpha.astype(BF16), rho.astype(BF16))
    bk_all = ((b * e_neg).astype(BF16), (kd * e_neg).astype(BF16))
    beta_e = (b * e_end).astype(BF16)
    kappa_e = (kd * e_end).astype(BF16)
    g_end = jnp.exp(tot)
    v_all = v_ref[...].astype(BF16)

    pairs = [(cc, h) for cc in range(n_chunks) for h in range(heads)]
    rs = lambda cc: slice(cc * CHUNK, (cc + 1) * CHUNK)
    ls = lambda h: slice(h * RW_HEAD, (h + 1) * RW_HEAD)
    cut = lambda x, cc, h: x[rs(cc), ls(h)]
    stack = lambda xs, cc, h: jnp.concatenate([cut(x, cc, h) for x in xs], axis=0)
    m = [_dot_nt(stack(ar_all, cc, h), stack(bk_all, cc, h)) for cc, h in pairs]
    pw = [jnp.where(strict, -x[:CHUNK, :CHUNK], 0.0) for x in m]
    m_akrk = [jnp.concatenate([jnp.where(strict, x[:CHUNK, CHUNK:], 0.0),
                               jnp.where(incl, x[CHUNK:, CHUNK:], 0.0)], axis=0).astype(BF16)
              for x in m]
    m_rb = [jnp.where(incl, x[CHUNK:, :CHUNK], 0.0).astype(BF16) for x in m]
    t_inv = [eye + x for x in pw]
    for _ in range(int(math.log2(CHUNK)) - 1):
        pwb = [x.astype(BF16) for x in pw]
        pw = [_dot(x, x) for x in pwb]
        t_inv = [t + _dot(t.astype(BF16), x.astype(BF16)) for t, x in zip(t_inv, pw)]
    mv = [_dot(x, cut(v_all, cc, h)) for x, (cc, h) in zip(m_akrk, pairs)]
    w12 = [_dot(t.astype(BF16),
                jnp.concatenate([x[:CHUNK].astype(BF16), cut(ar_all[0], cc, h)], axis=1)
                ).astype(BF16)
           for t, x, (cc, h) in zip(t_inv, mv, pairs)]
    mw = [_dot(x, w) for x, w in zip(m_rb, w12)]
    bw = [_dot_tn(cut(beta_e, cc, h), w) for w, (cc, h) in zip(w12, pairs)]
    kv = [_dot_tn(cut(kappa_e, cc, h), cut(v_all, cc, h)) for cc, h in pairs]
    pre = {}
    for i, (cc, h) in enumerate(pairs):
        y1 = mv[i][CHUNK:] - mw[i][:, :RW_HEAD]
        r2 = cut(rho, cc, h) - mw[i][:, RW_HEAD:]
        g_m = kv[i] - bw[i][:, :RW_HEAD]
        p_m = eye * g_end[cc * CHUNK:cc * CHUNK + 1, ls(h)] - bw[i][:, RW_HEAD:]
        pre[cc, h] = (y1, jnp.concatenate([r2, p_m], axis=0).astype(BF16), g_m)

    st = [st_ref[h] for h in range(heads)]
    ys = {}
    for cc in (range(n_chunks - 1, -1, -1) if reverse else range(n_chunks)):
        prod = [_dot(pre[cc, h][1], st[h].astype(BF16)) for h in range(heads)]
        for h in range(heads):
            ys[cc, h] = pre[cc, h][0] + prod[h][:CHUNK]
            st[h] = prod[h][CHUNK:] + pre[cc, h][2]
    for h in range(heads):
        st_ref[h] = st[h]
    for cc in range(n_chunks):
        y_ref[rs(cc), :] = jnp.concatenate([ys[cc, h] for h in range(heads)], axis=1)


def _wkv(st, d, n_lat, reverse):
    lw, b, kd = st[f"lw{d}"], st[f"b{d}"], st[f"kd{d}"]
    t, hw = lw.shape
    heads = _pick(hw // RW_HEAD, (8, 4, 2))
    tb = _pick(math.gcd(n_lat, t - n_lat), (256, 128, 64))
    nb = t // tb
    n_lat_blocks = n_lat // tb
    if reverse:
        blk = lambda h, c: (nb - 1 - c, h)
    else:
        blk = lambda h, c: ((c + n_lat_blocks) % nb, h)
    spec = pl.BlockSpec((tb, heads * RW_HEAD), blk)
    return pl.pallas_call(
        functools.partial(_wkv_kernel, heads=heads, tb=tb, reverse=reverse),
        out_shape=jax.ShapeDtypeStruct((t, hw), F32),
        grid=(hw // (heads * RW_HEAD), nb),
        in_specs=[spec] * 6,
        out_specs=spec,
        scratch_shapes=[pltpu.VMEM((heads, RW_HEAD, RW_HEAD), F32)],
        compiler_params=_params(("parallel", "arbitrary")),
        name="wkv_bwd" if reverse else "wkv_fwd",
    )(lw, st["kk"], b, kd, st["r"], st["v"])


def _rw_out_kernel(yf_ref, yb_ref, bonus_ref, g_ref, lnw_ref, lnb_ref, bd_ref, o_ref):
    y = yf_ref[...] + yb_ref[...]
    bd = bd_ref[...]
    mean = _segsum(y, bd) / RW_HEAD
    yc = y - mean
    var = _segsum(yc * yc, bd) / RW_HEAD
    o = yc * lax.rsqrt(var + RW_GN_EPS) * lnw_ref[...] + lnb_ref[...] + bonus_ref[...].astype(F32)
    o_ref[...] = (o * g_ref[...].astype(F32)).astype(o_ref.dtype)


def _rw_output(y_f, y_b, st, p):
    t, hw = y_f.shape
    bm = _pick(t, (256, 128, 64))
    row = pl.BlockSpec((bm, hw), lambda i: (i, 0))
    vec = pl.BlockSpec((1, hw), lambda i: (0, 0))
    return pl.pallas_call(
        _rw_out_kernel,
        out_shape=jax.ShapeDtypeStruct((t, hw), BF16),
        grid=(t // bm,),
        in_specs=[row, row, row, row, vec, vec, pl.BlockSpec((LANE, LANE), lambda i: (0, 0))],
        out_specs=row,
        compiler_params=_params(("parallel",)),
        name="rw_output",
    )(y_f, y_b, st["bonus"], st["g"], p["ln_w"], p["ln_b"], p["bd"])


def _mla_prep_kernel(z_ref, qn_ref, qup_ref, kvn_ref, kvup_ref, qg_ref, kg_ref, c_ref, s1_ref,
                     s2_ref, q_out, k_out, v_out, *, off, q_rank, kv_rank, heads, q_scale):
    z = z_ref[...]

    def rms(x, g):
        return x * lax.rsqrt(jnp.mean(x * x, axis=-1, keepdims=True) + NORM_EPS) * g

    q = _dot(rms(z[:, off:off + q_rank], qn_ref[...]).astype(BF16), qup_ref[...])
    kv_lat = z[:, off + q_rank:off + q_rank + kv_rank]
    kv = _dot(rms(kv_lat, kvn_ref[...]).astype(BF16), kvup_ref[...])
    k_rope = z[:, off + q_rank + kv_rank:off + q_rank + kv_rank + MLA_ROPE]
    k_rope = jnp.concatenate([k_rope, jnp.zeros((z.shape[0], LANE - MLA_ROPE), F32)], axis=1)
    cos, s1, s2 = c_ref[...], s1_ref[...], s2_ref[...]
    sumsq = lambda x: jnp.sum(x * x, axis=-1, keepdims=True)
    inv_rms = lambda ss: lax.rsqrt(ss / MLA_QK + NORM_EPS)

    def rotary(x):
        return (x * cos + pltpu.roll(x, LANE - ROPE_FREQ, 1) * s1
                + pltpu.roll(x, ROPE_FREQ, 1) * s2)

    qg_n, qg_r = qg_ref[:, :MLA_NOPE], qg_ref[:, MLA_NOPE:]
    kg_n, kg_r = kg_ref[:, :MLA_NOPE], kg_ref[:, MLA_NOPE:]
    kr_ss = sumsq(k_rope)
    kr_rot = rotary(k_rope * kg_r)
    for h in range(heads):
        nope = slice(h * MLA_HEAD_PAD, h * MLA_HEAD_PAD + MLA_NOPE)
        rope = slice(h * MLA_HEAD_PAD + MLA_NOPE, (h + 1) * MLA_HEAD_PAD)
        qn, qr = q[:, nope], q[:, rope]
        q_inv = inv_rms(sumsq(qn) + sumsq(qr))
        q_out[:, nope] = (qn * q_inv * qg_n * q_scale).astype(BF16)
        q_out[:, rope] = (rotary(qr * q_inv * qg_r) * q_scale).astype(BF16)
        kn = kv[:, h * MLA_NOPE:(h + 1) * MLA_NOPE]
        k_inv = inv_rms(sumsq(kn) + kr_ss)
        k_out[:, nope] = (kn * k_inv * kg_n).astype(BF16)
        k_out[:, rope] = (kr_rot * k_inv).astype(BF16)
    v_out[...] = kv[:, heads * MLA_NOPE:].astype(BF16)


def _mla_prep(z, rw_cols, p, rope):
    t, n = z.shape
    heads = p["heads"]
    q_rank, kv_rank = p["q_up"].shape[0], p["kv_up"].shape[0]
    zw = min(w for w in range(LANE, n + 1, LANE) if n % w == 0 and n - w <= rw_cols)
    bm = _pick(t, (256, 128, 64))
    qw = heads * MLA_HEAD_PAD
    row = lambda w: pl.BlockSpec((bm, w), lambda i: (i, 0))
    const = lambda a: pl.BlockSpec(a.shape, lambda i: (0, 0))
    args = (z, p["q_norm"], p["q_up"], p["kv_norm"], p["kv_up"], p["q_gain"], p["k_gain"])
    return pl.pallas_call(
        functools.partial(_mla_prep_kernel, off=rw_cols - (n - zw), q_rank=q_rank, kv_rank=kv_rank,
                          heads=heads, q_scale=ATTN_Q_SCALE),
        out_shape=[jax.ShapeDtypeStruct((t, qw), BF16), jax.ShapeDtypeStruct((t, qw), BF16),
                   jax.ShapeDtypeStruct((t, heads * MLA_V_DIM), BF16)],
        grid=(t // bm,),
        in_specs=([pl.BlockSpec((bm, zw), lambda i: (i, n // zw - 1))]
                  + [const(a) for a in args[1:]] + [row(LANE)] * 3),
        out_specs=[row(qw), row(qw), row(heads * MLA_V_DIM)],
        compiler_params=_params(("parallel",)),
        name="mla_prep",
    )(*args, *rope)


def _attn_kernel(q_ref, k_ref, vt_ref, o_ref, *, n_lat, bq, hps, with_ctx):
    def attend(key0, n_keys):
        bk = _pick(n_keys, (1408, 768, 512, 256, 128))
        qcols = lambda c: slice(c * MLA_HEAD_PAD, (c + 1) * MLA_HEAD_PAD)
        q = [q_ref[:, qcols(c)] for c in range(hps)]
        keys = lambda j: slice(key0 + j * bk, key0 + (j + 1) * bk)
        scores = lambda j: [_dot_nt(k_ref[keys(j), qcols(c)], q[c]) for c in range(hps)]
        s_next = scores(0)
        m = [jnp.full((1, bq), -jnp.inf, F32)] * hps
        acc = [jnp.zeros((V_ROWS, bq), F32)] * hps
        for j in range(n_keys // bk):
            s = s_next
            if (j + 1) * bk < n_keys:
                s_next = scores(j + 1)
            for c in range(hps):
                m_new = jnp.maximum(m[c], jnp.max(s[c], axis=0, keepdims=True))
                p = jnp.exp2(s[c] - m_new).astype(BF16)
                acc[c] = (jnp.exp2(m[c] - m_new) * acc[c]
                          + _dot(vt_ref[c * V_ROWS:(c + 1) * V_ROWS, keys(j)], p))
                m[c] = m_new
        for c in range(hps):
            o_ref[:, c * MLA_V_DIM:(c + 1) * MLA_V_DIM] = (
                acc[c][:MLA_V_DIM] / acc[c][MLA_V_DIM:MLA_V_DIM + 1]).T.astype(o_ref.dtype)

    n_all = k_ref.shape[0]
    if not with_ctx:
        attend(0, n_all)
        return
    is_latent = pl.program_id(1) < n_lat // bq

    @pl.when(is_latent)
    def _():
        attend(0, n_all)

    @pl.when(jnp.logical_not(is_latent))
    def _():
        attend(n_lat, n_all - n_lat)


def _attention(q, k, vt, heads, n_lat, with_ctx):
    t = k.shape[0]
    n_q = t if with_ctx else n_lat
    bq = _pick(math.gcd(n_lat, t - n_lat), (256, 128, 64))
    hps = 4
    return pl.pallas_call(
        functools.partial(_attn_kernel, n_lat=n_lat, bq=bq, hps=hps, with_ctx=with_ctx),
        out_shape=jax.ShapeDtypeStruct((n_q, heads * MLA_V_DIM), BF16),
        grid=(heads // hps, n_q // bq),
        in_specs=[pl.BlockSpec((bq, hps * MLA_HEAD_PAD), lambda h, i: (i, h)),
                  _resident((t, hps * MLA_HEAD_PAD), lambda h, i: (0, h)),
                  _resident((hps * V_ROWS, t), lambda h, i: (h, 0))],
        out_specs=pl.BlockSpec((bq, hps * MLA_V_DIM), lambda h, i: (i, h)),
        compiler_params=_params(("parallel", "parallel")),
        name="attention",
    )(q, k, vt)


def _pad_cols(w, width):
    return jnp.pad(w, ((0, 0), (0, width - w.shape[1])))


def _rope_tables(n_lat, n_ctx):
    t = jnp.arange(n_lat, dtype=jnp.int32)
    pos = jnp.stack([t // GRID_W, t % GRID_W], axis=-1).astype(F32)
    inv_freq = ROPE_THETA ** (-jnp.arange(ROPE_FREQ, dtype=F32) / ROPE_FREQ)
    ang = pos[:, :, None] * inv_freq
    cos, sin = jnp.cos(ang), jnp.sin(ang)
    zero = jnp.zeros_like(sin)
    lay = lambda first, second: jnp.stack([first, second], axis=2).reshape(n_lat, MLA_ROPE)
    tabs = []
    for rope_part, fill in ((lay(cos, cos), 1.0), (lay(-sin, zero), 0.0), (lay(zero, sin), 0.0)):
        tab = jnp.concatenate([rope_part, jnp.full((n_lat, LANE - MLA_ROPE), fill, F32)], axis=1)
        tabs.append(jnp.concatenate([tab, jnp.full((n_ctx, LANE), fill, F32)], axis=0))
    return tuple(tabs)


def _layer_params(l, rw_mu, rw_w0, rw_w_up, rw_a0, rw_a_up, rw_g_up, rw_k_k, rw_k_a, rw_r_k,
                  rw_ln_w, rw_ln_b, mla_q_norm, mla_q_up, mla_kv_norm, mla_kv_up, mla_q_gain,
                  mla_k_gain):
    hw = rw_k_k.shape[1]
    ld, la, lg = rw_w_up.shape[2], rw_a_up.shape[2], rw_g_up.shape[1]
    q_rank, kv_rank = mla_q_up.shape[1], mla_kv_up.shape[1]
    heads = mla_kv_up.shape[2] // (MLA_NOPE + MLA_V_DIM)
    rw_cols = 3 * hw + ld + la + lg
    zw = -(-rw_cols // LANE) * LANE
    assert ld <= LANE and la <= LANE and rw_cols % 4 == 0 and 3 * hw + ld + LANE <= zw
    code = np.minimum(np.arange(zw, dtype=np.int32) // (rw_cols // 4), 3)[None, :]
    bd = (np.arange(LANE)[:, None] // RW_HEAD == np.arange(LANE)[None, :] // RW_HEAD)
    pad_rows = lambda x: jnp.pad(x, ((0, 0), (0, LANE - x.shape[1]), (0, 0)))
    return {
        "hw": hw, "heads": heads, "rw_cols": rw_cols, "zw": zw, "ld": ld, "la": la,
        "mu": _pad_cols(rw_mu[l][None, :], zw),
        "code": jnp.asarray(code),
        "k_k": rw_k_k[l][None, :], "k_a": rw_k_a[l][None, :], "r_k": rw_r_k[l].reshape(1, hw),
        "w0": rw_w0[l], "a0": rw_a0[l],
        "wup": pad_rows(rw_w_up[l]).astype(BF16), "aup": pad_rows(rw_a_up[l]).astype(BF16),
        "gup": rw_g_up[l].astype(BF16),
        "bd": jnp.asarray(bd, BF16),
        "ln_w": rw_ln_w[l][None, :], "ln_b": rw_ln_b[l][None, :],
        "q_norm": mla_q_norm[l][None, :], "kv_norm": mla_kv_norm[l][None, :],
        "q_up": jnp.pad(mla_q_up[l].reshape(q_rank, heads, MLA_QK),
                        ((0, 0), (0, 0), (0, MLA_HEAD_PAD - MLA_QK))
                        ).reshape(q_rank, heads * MLA_HEAD_PAD).astype(BF16),
        "kv_up": mla_kv_up[l].reshape(kv_rank, heads, 2, MLA_NOPE).transpose(0, 2, 1, 3)
                             .reshape(kv_rank, 2 * heads * MLA_NOPE).astype(BF16),
        "q_gain": _pad_cols(mla_q_gain[l][None, :], MLA_HEAD_PAD),
        "k_gain": _pad_cols(mla_k_gain[l][None, :], MLA_HEAD_PAD),
    }


def _ffn_half_step(xt, mod, g, ffn_w_in, ffn_w_out, lead, n_lat):
    h = _norm_mod(xt, g, mod[:, 0], mod[:, 1], n_lat)
    act = _swiglu_matmul(h, ffn_w_in, lead)
    return _residual_matmul([act], ffn_w_out, lead, xt, FFN_RES * mod[:, 2], n_lat)


def kernel(x, c, ctx, c_ctx, w_mod, b_mod, norm_g, ffn_w_in, ffn_w_out, w_in, w_out, rw_mu, rw_w0,
           rw_w_up, rw_a0, rw_a_up, rw_g_up, rw_k_k, rw_k_a, rw_r_k, rw_ln_w, rw_ln_b, mla_q_norm,
           mla_q_up, mla_kv_norm, mla_kv_up, mla_q_gain, mla_k_gain):
    batch, n_lat, d = x.shape
    n_ctx = ctx.shape[1]
    depth = w_mod.shape[0]
    assert batch == 1 and c.shape[0] == 1 and ctx.shape[0] == 1
    assert n_lat % GRID_W == 0 and n_ctx % GRID_W == 0

    xt = jnp.concatenate([x[0], ctx[0]], axis=0)
    cc = jnp.zeros((8, d), F32).at[0].set(c[0]).at[1].set(c_ctx)
    rope = _rope_tables(n_lat, n_ctx)

    for l in range(depth):
        last = l == depth - 1
        p = _layer_params(l, rw_mu, rw_w0, rw_w_up, rw_a0, rw_a_up, rw_g_up, rw_k_k, rw_k_a,
                          rw_r_k, rw_ln_w, rw_ln_b, mla_q_norm, mla_q_up, mla_kv_norm, mla_kv_up,
                          mla_q_gain, mla_k_gain)
        mod = _modulation(cc, w_mod, b_mod, l)[:2].reshape(2, N_MOD, d)

        xt = _ffn_half_step(xt, mod[:, 0:3], norm_g[l, 0], ffn_w_in, ffn_w_out, (l, 0), n_lat)

        hz = _norm_mod(xt, norm_g[l, 1], mod[:, 3], mod[:, 4], n_lat)
        z = _matmul(hz, w_in, (l,))

        st = _rw_streams(z, p, n_lat)
        y_f = _wkv(st, 0, n_lat, reverse=False)
        y_b = _wkv(st, 1, n_lat, reverse=True)
        o_rw = _rw_output(y_f, y_b, st, p)

        q, k, v = _mla_prep(z, p["rw_cols"], p, rope)
        vt = jnp.concatenate([v.T.reshape(p["heads"], MLA_V_DIM, -1),
                              jnp.ones((p["heads"], V_ROWS - MLA_V_DIM, v.shape[0]), BF16)], axis=1)
        o_mla = _attention(q, k, vt.reshape(p["heads"] * V_ROWS, -1), p["heads"], n_lat,
                           with_ctx=not last)

        xt = _residual_matmul([o_rw, o_mla], w_out, (l,), xt, mod[:, 5], n_lat,
                              rows=n_lat if last else None)

        xt = _ffn_half_step(xt, mod[:, 6:9], norm_g[l, 2], ffn_w_in, ffn_w_out, (l, 1), n_lat)
    return xt[:n_lat][None]
```

```python
import functools
import math

import numpy as np
import jax
import jax.numpy as jnp
from jax import lax
from jax.experimental import pallas as pl
from jax.experimental.pallas import tpu as pltpu

F32 = jnp.float32
BF16 = jnp.bfloat16
HI = lax.Precision.HIGHEST

GRID_W = 64
NORM_EPS = 1e-6
FFN_RES = 0.5
RW_HEAD = 64
RW_GN_EPS = 64e-5
RW_DECAY_SCALE = math.exp(-0.5)
MLA_V_DIM = 128
MLA_NOPE = 128
MLA_ROPE = 64
MLA_QK = MLA_NOPE + MLA_ROPE
MLA_HEAD_PAD = 256
ROPE_FREQ = MLA_ROPE // 4
ROPE_THETA = 10000.0
N_MOD = 9
LANE = 128
CHUNK = 64
ATTN_Q_SCALE = math.log2(math.e) / math.sqrt(MLA_QK)
V_ROWS = MLA_V_DIM + 16
VMEM_LIMIT = 60 * 1024 * 1024


def _pick(n, candidates):
    for c in candidates:
        if n % c == 0:
            return c
    raise ValueError(f"no block size for {n} among {candidates}")


def _params(sem):
    return pltpu.CompilerParams(dimension_semantics=sem, vmem_limit_bytes=VMEM_LIMIT)


def _dot(a, b, precision=None):
    return jnp.dot(a, b, preferred_element_type=F32, precision=precision)


def _dot_nt(a, b, precision=None):
    return lax.dot_general(a, b, (((1,), (1,)), ((), ())), preferred_element_type=F32,
                           precision=precision)


def _dot_tn(a, b, precision=None):
    return lax.dot_general(a, b, (((0,), (0,)), ((), ())), preferred_element_type=F32,
                           precision=precision)


def _bdot(a, b):
    return _dot(a.astype(BF16), b.astype(BF16))


def _bdot_nt(a, b):
    return _dot_nt(a.astype(BF16), b.astype(BF16))


def _wspec(lead, block, index_map):
    return pl.BlockSpec((None,) * len(lead) + block, lambda *g: lead + index_map(*g))


def _row_select(row0, n_rows, n_lat, ref):
    rows = row0 + lax.broadcasted_iota(jnp.int32, (n_rows, 1), 0)
    return jnp.where(rows < n_lat, ref[0:1, :], ref[1:2, :])


def _mod_kernel(c_ref, w_ref, b_ref, o_ref):
    c = c_ref[...]
    a = (c * jax.nn.sigmoid(c)).astype(BF16)
    o_ref[...] = _dot(a, w_ref[...].astype(BF16)) + b_ref[...]


def _modulation(cc, w_mod, b_mod, l):
    _, d, n = w_mod.shape
    bn = _pick(n, (512, 256, 128))
    return pl.pallas_call(
        _mod_kernel,
        out_shape=jax.ShapeDtypeStruct((8, n), F32),
        grid=(n // bn,),
        in_specs=[pl.BlockSpec((8, d), lambda j: (0, 0)),
                  _wspec((l,), (d, bn), lambda j: (0, j)),
                  _wspec((l,), (1, bn), lambda j: (0, j))],
        out_specs=pl.BlockSpec((8, bn), lambda j: (0, j)),
        compiler_params=_params(("parallel",)),
        name="modulation",
    )(cc, w_mod, b_mod.reshape(b_mod.shape[0], 1, n))


def _norm_mod_kernel(x_ref, g_ref, sh_ref, sc_ref, o_ref, *, n_lat, bm):
    x = x_ref[...]
    y = x * lax.rsqrt(jnp.mean(x * x, axis=-1, keepdims=True) + NORM_EPS) * g_ref[...]
    kind = pl.ds((pl.program_id(0) * bm >= n_lat).astype(jnp.int32), 1)
    o_ref[...] = (y * (1.0 + sc_ref[kind, :]) + sh_ref[kind, :]).astype(o_ref.dtype)


def _norm_mod(x, g, shift, scale, n_lat):
    m, d = x.shape
    bm = _pick(math.gcd(n_lat, m - n_lat) if m > n_lat else m, (256, 128, 64, 32, 16))
    return pl.pallas_call(
        functools.partial(_norm_mod_kernel, n_lat=n_lat, bm=bm),
        out_shape=jax.ShapeDtypeStruct((m, d), BF16),
        grid=(m // bm,),
        in_specs=[pl.BlockSpec((bm, d), lambda i: (i, 0)),
                  pl.BlockSpec((1, d), lambda i: (0, 0)),
                  pl.BlockSpec((2, d), lambda i: (0, 0)),
                  pl.BlockSpec((2, d), lambda i: (0, 0))],
        out_specs=pl.BlockSpec((bm, d), lambda i: (i, 0)),
        compiler_params=_params(("parallel",)),
        name="norm_mod",
    )(x, g.reshape(1, d), shift, scale)


def _resident(block_shape, index_map):
    return pl.BlockSpec(block_shape, index_map, pipeline_mode=pl.Buffered(1))


def _mm_kernel(a_ref, w_ref, o_ref):
    o_ref[...] = _dot(a_ref[...], w_ref[...].astype(BF16)).astype(o_ref.dtype)


def _matmul(a, w, lead):
    m, k = a.shape
    n = w.shape[-1]
    bm = _pick(m, (1056, 1024, 512, 320, 256, 128, 64))
    bn = _pick(n, (512, 256, 128))
    return pl.pallas_call(
        _mm_kernel,
        out_shape=jax.ShapeDtypeStruct((m, n), F32),
        grid=(m // bm, n // bn),
        in_specs=[_resident((bm, k), lambda i, j: (i, 0)),
                  _wspec(lead, (k, bn), lambda i, j: (0, j))],
        out_specs=pl.BlockSpec((bm, bn), lambda i, j: (i, j)),
        compiler_params=_params(("parallel", "arbitrary")),
        name="matmul",
    )(a, w)


def _swiglu_kernel(a_ref, wg_ref, wu_ref, o_ref):
    a = a_ref[...]
    gate = _dot(a, wg_ref[...].astype(BF16))
    up = _dot(a, wu_ref[...].astype(BF16))
    o_ref[...] = (gate * jax.nn.sigmoid(gate) * up).astype(o_ref.dtype)


def _swiglu_matmul(a, w, lead):
    m, k = a.shape
    f = w.shape[-1] // 2
    bm = _pick(m, (1056, 1024, 512, 320, 256, 128, 64))
    bn = _pick(f, (512, 256, 128))
    nb = f // bn
    return pl.pallas_call(
        _swiglu_kernel,
        out_shape=jax.ShapeDtypeStruct((m, f), BF16),
        grid=(m // bm, nb),
        in_specs=[_resident((bm, k), lambda i, j: (i, 0)),
                  _wspec(lead, (k, bn), lambda i, j: (0, j)),
                  _wspec(lead, (k, bn), lambda i, j: (0, j + nb))],
        out_specs=pl.BlockSpec((bm, bn), lambda i, j: (i, j)),
        compiler_params=_params(("parallel", "arbitrary")),
        name="swiglu_matmul",
    )(a, w, w)


def _residual_kernel(*refs, n_a, n_lat, bm):
    a_refs = refs[:n_a]
    w_refs = refs[n_a:2 * n_a]
    x_ref, g_ref, o_ref = refs[2 * n_a:]
    acc = _dot(a_refs[0][...], w_refs[0][...].astype(BF16))
    for a_ref, w_ref in zip(a_refs[1:], w_refs[1:]):
        acc = acc + _dot(a_ref[...], w_ref[...].astype(BF16))
    gate = _row_select(pl.program_id(0) * bm, bm, n_lat, g_ref)
    o_ref[...] = x_ref[...] + gate * acc


def _residual_matmul(a_list, w, lead, x, gate, n_lat, rows=None):
    m, n = (rows or x.shape[0]), x.shape[1]
    n_a = len(a_list)
    k = a_list[0].shape[1]
    assert all(a.shape[1] == k for a in a_list) and w.shape[-2] == n_a * k
    bm = _pick(m, (1056, 1024, 512, 320, 256, 128, 64))
    bn = _pick(n, (256, 128))
    a_specs = [_resident((bm, k), lambda i, j: (i, 0)) for _ in a_list]
    w_specs = [_wspec(lead, (k, bn), functools.partial(lambda i, j, p: (p, j), p=p))
               for p in range(n_a)]
    return pl.pallas_call(
        functools.partial(_residual_kernel, n_a=n_a, n_lat=n_lat, bm=bm),
        out_shape=jax.ShapeDtypeStruct((m, n), F32),
        grid=(m // bm, n // bn),
        in_specs=a_specs + w_specs + [pl.BlockSpec((bm, bn), lambda i, j: (i, j)),
                                      pl.BlockSpec((2, bn), lambda i, j: (0, j))],
        out_specs=pl.BlockSpec((bm, bn), lambda i, j: (i, j)),
        compiler_params=_params(("parallel", "arbitrary")),
        name="residual_matmul",
    )(*a_list, *([w] * n_a), x, gate)


def _segsum(x, bd):
    hi = x.astype(BF16)
    r1 = x - hi.astype(F32)
    mid = r1.astype(BF16)
    lo = (r1 - mid.astype(F32)).astype(BF16)
    parts = [_dot(hi[:, j:j + LANE], bd) + _dot(mid[:, j:j + LANE], bd) + _dot(lo[:, j:j + LANE], bd)
             for j in range(0, x.shape[1], LANE)]
    return jnp.concatenate(parts, axis=1)


def _streams_kernel(zc_ref, zp_ref, zn_ref, mu_ref, code_ref, kkw_ref, ka_ref, rk_ref, w0_ref,
                    a0_ref, wup_ref, aup_ref, gup_ref, bd_ref,
                    r_out, v_out, kk_out, lw0_out, lw1_out, b0_out, b1_out, kd0_out, kd1_out,
                    g_out, bonus_out, zs_ref, *, rb, n_lat_blocks, n_blocks, hw, ld, la, lg,
                    rw_cols):
    i = pl.program_id(0)
    rows = lax.broadcasted_iota(jnp.int32, (rb, 1), 0)
    quarter = rw_cols // 4
    n_tiles = zc_ref.shape[1] // LANE

    def token_shift(neighbour):
        for c in range(n_tiles):
            cols = slice(c * LANE, (c + 1) * LANE)
            z = zc_ref[:, cols]
            kinds = sorted({min(col // quarter, 3) for col in (c * LANE, (c + 1) * LANE - 1)})
            shifted = neighbour(kinds[-1], cols)
            for kind in reversed(kinds[:-1]):
                shifted = jnp.where(code_ref[:, cols] <= kind, neighbour(kind, cols), shifted)
            zs_ref[:, cols] = z + (shifted - z) * mu_ref[:, cols]

    prev_tok = lambda cols: pltpu.roll(zc_ref[:, cols], 1, 0)
    next_tok = lambda cols: pltpu.roll(zc_ref[:, cols], rb - 1, 0)

    @pl.when(i < n_lat_blocks)
    def _():
        def neighbour(kind, cols):
            if kind == 0:
                return jnp.where(rows % GRID_W == 0, 0.0, prev_tok(cols))
            if kind == 1:
                return jnp.where(rows % GRID_W == GRID_W - 1, 0.0, next_tok(cols))
            if kind == 2:
                above = zp_ref[:, cols] * (i > 0).astype(F32)
                return jnp.concatenate([above, zc_ref[:rb - GRID_W, cols]], axis=0)
            below = zn_ref[:, cols] * (i < n_lat_blocks - 1).astype(F32)
            return jnp.concatenate([zc_ref[GRID_W:, cols], below], axis=0)
        token_shift(neighbour)

    @pl.when(i >= n_lat_blocks)
    def _():
        def neighbour(kind, cols):
            if kind < 2:
                before = zp_ref[GRID_W - 1:GRID_W, cols] * (i > n_lat_blocks).astype(F32)
                return jnp.where(rows == 0, before, prev_tok(cols))
            after = zn_ref[0:1, cols] * (i < n_blocks - 1).astype(F32)
            return jnp.where(rows == rb - 1, after, next_tok(cols))
        token_shift(neighbour)

    zs = zs_ref[...]

    r = zs[:, 0:hw]
    k = zs[:, hw:2 * hw]
    v = zs[:, 2 * hw:3 * hw]
    wd = zs[:, 3 * hw:3 * hw + LANE]
    ad = zs[:, 3 * hw + ld:3 * hw + ld + LANE]
    gd = zs[:, 3 * hw + ld + la:3 * hw + ld + la + lg]
    bd = bd_ref[...]

    kk = k * kkw_ref[...]
    kk = kk / jnp.maximum(jnp.sqrt(_segsum(kk * kk, bd)), 1e-12)
    tw = jnp.tanh(wd).astype(BF16)
    adb = ad.astype(BF16)
    ka = ka_ref[...]
    kds = []
    for d, (lw_out, b_out, kd_out) in enumerate(((lw0_out, b0_out, kd0_out),
                                                 (lw1_out, b1_out, kd1_out))):
        w_logit = w0_ref[d:d + 1, :] + _dot(tw, wup_ref[d])
        lw_out[...] = -RW_DECAY_SCALE * jax.nn.sigmoid(w_logit)
        a = jax.nn.sigmoid(a0_ref[d:d + 1, :] + _dot(adb, aup_ref[d]))
        kd = k * (1.0 + (a - 1.0) * ka)
        kd_out[...] = kd.astype(kd_out.dtype)
        b_out[...] = (kk * a).astype(b_out.dtype)
        kds.append(kd)
    g_out[...] = _dot(jax.nn.sigmoid(gd).astype(BF16), gup_ref[...]).astype(g_out.dtype)
    kmean = (kds[0] + kds[1]) / 2.0
    bonus_out[...] = (_segsum(r * kmean * rk_ref[...], bd) * v).astype(bonus_out.dtype)
    r_out[...] = r.astype(r_out.dtype)
    v_out[...] = v.astype(v_out.dtype)
    kk_out[...] = kk.astype(kk_out.dtype)


def _rw_streams(z, p, n_lat):
    t = z.shape[0]
    hw, zw = p["hw"], p["zw"]
    lg = p["gup"].shape[0]
    rb = _pick(math.gcd(n_lat, t - n_lat), (256, 128, 64))
    n_blocks = t // rb
    n_lat_blocks = n_lat // rb
    halo = lambda f: pl.BlockSpec((GRID_W, zw), f)
    per = rb // GRID_W
    const2 = lambda shape: pl.BlockSpec(shape, lambda i: (0, 0))
    const3 = lambda shape: pl.BlockSpec(shape, lambda i: (0, 0, 0))
    out_spec = pl.BlockSpec((rb, hw), lambda i: (i, 0))
    out_dtypes = {"lw0": F32, "lw1": F32}
    names = ("r", "v", "kk", "lw0", "lw1", "b0", "b1", "kd0", "kd1", "g", "bonus")
    outs = pl.pallas_call(
        functools.partial(_streams_kernel, rb=rb, n_lat_blocks=n_lat_blocks, n_blocks=n_blocks,
                          hw=hw, ld=p["ld"], la=p["la"], lg=lg, rw_cols=p["rw_cols"]),
        out_shape=[jax.ShapeDtypeStruct((t, hw), out_dtypes.get(n, BF16)) for n in names],
        grid=(n_blocks,),
        in_specs=[pl.BlockSpec((rb, zw), lambda i: (i, 0)),
                  halo(lambda i: (jnp.maximum(i * per - 1, 0), 0)),
                  halo(lambda i: (jnp.minimum((i + 1) * per, t // GRID_W - 1), 0)),
                  const2((1, zw)), const2((1, zw)),
                  const2((1, hw)), const2((1, hw)), const2((1, hw)),
                  const2((2, hw)), const2((2, hw)),
                  const3((2, LANE, hw)), const3((2, LANE, hw)), const2((lg, hw)),
                  const2((LANE, LANE))],
        out_specs=[out_spec] * 11,
        scratch_shapes=[pltpu.VMEM((rb, zw), F32)],
        compiler_params=_params(("parallel",)),
        name="rw_streams",
    )(z, z, z, p["mu"], p["code"], p["k_k"], p["k_a"], p["r_k"], p["w0"], p["a0"],
      p["wup"], p["aup"], p["gup"], p["bd"])
    return dict(zip(names, outs))


def _wkv_kernel(lw_ref, kk_ref, b_ref, kd_ref, r_ref, v_ref, y_ref, st_ref, *, heads, tb, reverse):
    @pl.when(pl.program_id(1) == 0)
    def _():
        st_ref[...] = jnp.zeros(st_ref.shape, F32)

    n_chunks = tb // CHUNK
    t_idx = lax.broadcasted_iota(jnp.int32, (CHUNK, CHUNK), 0)
    s_idx = lax.broadcasted_iota(jnp.int32, (CHUNK, CHUNK), 1)
    strict = (s_idx > t_idx) if reverse else (s_idx < t_idx)
    incl = (s_idx >= t_idx) if reverse else (s_idx <= t_idx)
    eye = (s_idx == t_idx).astype(F32)

    bt = lax.broadcasted_iota(jnp.int32, (tb, tb), 0)
    bs = lax.broadcasted_iota(jnp.int32, (tb, tb), 1)
    before = (bs >= bt) if reverse else (bs <= bt)
    tri = jnp.where(((bt // CHUNK) == (bs // CHUNK)) & before, 1.0, 0.0).astype(BF16)
    lw = lw_ref[...]
    lw_hi = lw.astype(BF16)
    lw_lo = (lw - lw_hi.astype(F32)).astype(BF16)
    cum = _dot(tri, lw_hi) + _dot(tri, lw_lo)
    edge = 0 if reverse else CHUNK - 1
    tot = jnp.concatenate(
        [jnp.broadcast_to(cum[cc * CHUNK + edge:cc * CHUNK + edge + 1], (CHUNK, cum.shape[1]))
         for cc in range(n_chunks)], axis=0)
    kk, b, kd = (x[...].astype(F32) for x in (kk_ref, b_ref, kd_ref))
    e_neg = jnp.exp(-cum)
    e_end = jnp.exp(tot - cum)
    alpha = kk * jnp.exp(cum - lw)
    rho = r_ref[...].astype(F32) * jnp.exp(cum)
    ar_all = (alpha.astype(BF16), rho.astype(BF16))
    bk_all = ((b * e_neg).astype(BF16), (kd * e_neg).astype(BF16))
    beta_e = (b * e_end).astype(BF16)
    kappa_e = (kd * e_end).astype(BF16)
    g_end = jnp.exp(tot)
    v_all = v_ref[...].astype(BF16)

    pairs = [(cc, h) for cc in range(n_chunks) for h in range(heads)]
    rs = lambda cc: slice(cc * CHUNK, (cc + 1) * CHUNK)
    ls = lambda h: slice(h * RW_HEAD, (h + 1) * RW_HEAD)
    cut = lambda x, cc, h: x[rs(cc), ls(h)]
    stack = lambda xs, cc, h: jnp.concatenate([cut(x, cc, h) for x in xs], axis=0)
    m = [_dot_nt(stack(ar_all, cc, h), stack(bk_all, cc, h)) for cc, h in pairs]
    pw = [jnp.where(strict, -x[:CHUNK, :CHUNK], 0.0) for x in m]
    m_akrk = [jnp.concatenate([jnp.where(strict, x[:CHUNK, CHUNK:], 0.0),
                               jnp.where(incl, x[CHUNK:, CHUNK:], 0.0)], axis=0).astype(BF16)
              for x in m]
    m_rb = [jnp.where(incl, x[CHUNK:, :CHUNK], 0.0).astype(BF16) for x in m]
    t_inv = [eye + x for x in pw]
    for _ in range(int(math.log2(CHUNK)) - 1):
        pwb = [x.astype(BF16) for x in pw]
        pw = [_dot(x, x) for x in pwb]
        t_inv = [t + _dot(t.astype(BF16), x.astype(BF16)) for t, x in zip(t_inv, pw)]
    mv = [_dot(x, cut(v_all, cc, h)) for x, (cc, h) in zip(m_akrk, pairs)]
    w12 = [_dot(t.astype(BF16),
                jnp.concatenate([x[:CHUNK].astype(BF16), cut(ar_all[0], cc, h)], axis=1)
                ).astype(BF16)
           for t, x, (cc, h) in zip(t_inv, mv, pairs)]
    mw = [_dot(x, w) for x, w in zip(m_rb, w12)]
    bw = [_dot_tn(cut(beta_e, cc, h), w) for w, (cc, h) in zip(w12, pairs)]
    kv = [_dot_tn(cut(kappa_e, cc, h), cut(v_all, cc, h)) for cc, h in pairs]
    pre = {}
    for i, (cc, h) in enumerate(pairs):
        y1 = mv[i][CHUNK:] - mw[i][:, :RW_HEAD]
        r2 = cut(rho, cc, h) - mw[i][:, RW_HEAD:]
        g_m = kv[i] - bw[i][:, :RW_HEAD]
        p_m = eye * g_end[cc * CHUNK:cc * CHUNK + 1, ls(h)] - bw[i][:, RW_HEAD:]
        pre[cc, h] = (y1, jnp.concatenate([r2, p_m], axis=0).astype(BF16), g_m)

    st = [st_ref[h] for h in range(heads)]
    ys = {}
    for cc in (range(n_chunks - 1, -1, -1) if reverse else range(n_chunks)):
        prod = [_dot(pre[cc, h][1], st[h].astype(BF16)) for h in range(heads)]
        for h in range(heads):
            ys[cc, h] = pre[cc, h][0] + prod[h][:CHUNK]
            st[h] = prod[h][CHUNK:] + pre[cc, h][2]
    for h in range(heads):
        st_ref[h] = st[h]
    for cc in range(n_chunks):
        y_ref[rs(cc), :] = jnp.concatenate([ys[cc, h] for h in range(heads)], axis=1)


def _wkv(st, d, n_lat, reverse):
    lw, b, kd = st[f"lw{d}"], st[f"b{d}"], st[f"kd{d}"]
    t, hw = lw.shape
    heads = _pick(hw // RW_HEAD, (16, 8, 4, 2))
    tb = _pick(math.gcd(n_lat, t - n_lat), (256, 128, 64))
    nb = t // tb
    n_lat_blocks = n_lat // tb
    if reverse:
        blk = lambda h, c: (nb - 1 - c, h)
    else:
        blk = lambda h, c: ((c + n_lat_blocks) % nb, h)
    spec = pl.BlockSpec((tb, heads * RW_HEAD), blk)
    return pl.pallas_call(
        functools.partial(_wkv_kernel, heads=heads, tb=tb, reverse=reverse),
        out_shape=jax.ShapeDtypeStruct((t, hw), F32),
        grid=(hw // (heads * RW_HEAD), nb),
        in_specs=[spec] * 6,
        out_specs=spec,
        scratch_shapes=[pltpu.VMEM((heads, RW_HEAD, RW_HEAD), F32)],
        compiler_params=_params(("parallel", "arbitrary")),
        name="wkv_bwd" if reverse else "wkv_fwd",
    )(lw, st["kk"], b, kd, st["r"], st["v"])


def _rw_out_kernel(yf_ref, yb_ref, bonus_ref, g_ref, lnw_ref, lnb_ref, bd_ref, o_ref):
    y = yf_ref[...] + yb_ref[...]
    bd = bd_ref[...]
    mean = _segsum(y, bd) / RW_HEAD
    yc = y - mean
    var = _segsum(yc * yc, bd) / RW_HEAD
    o = yc * lax.rsqrt(var + RW_GN_EPS) * lnw_ref[...] + lnb_ref[...] + bonus_ref[...].astype(F32)
    o_ref[...] = (o * g_ref[...].astype(F32)).astype(o_ref.dtype)


def _rw_output(y_f, y_b, st, p):
    t, hw = y_f.shape
    bm = _pick(t, (256, 128, 64))
    row = pl.BlockSpec((bm, hw), lambda i: (i, 0))
    vec = pl.BlockSpec((1, hw), lambda i: (0, 0))
    return pl.pallas_call(
        _rw_out_kernel,
        out_shape=jax.ShapeDtypeStruct((t, hw), BF16),
        grid=(t // bm,),
        in_specs=[row, row, row, row, vec, vec, pl.BlockSpec((LANE, LANE), lambda i: (0, 0))],
        out_specs=row,
        compiler_params=_params(("parallel",)),
        name="rw_output",
    )(y_f, y_b, st["bonus"], st["g"], p["ln_w"], p["ln_b"], p["bd"])


def _mla_prep_kernel(z_ref, qn_ref, qup_ref, kvn_ref, kvup_ref, qg_ref, kg_ref, c_ref, s1_ref,
                     s2_ref, q_out, k_out, v_out, *, off, q_rank, kv_rank, heads, q_scale):
    z = z_ref[...]

    def rms(x, g):
        return x * lax.rsqrt(jnp.mean(x * x, axis=-1, keepdims=True) + NORM_EPS) * g

    q = _dot(rms(z[:, off:off + q_rank], qn_ref[...]).astype(BF16), qup_ref[...])
    kv_lat = z[:, off + q_rank:off + q_rank + kv_rank]
    kv = _dot(rms(kv_lat, kvn_ref[...]).astype(BF16), kvup_ref[...])
    k_rope = z[:, off + q_rank + kv_rank:off + q_rank + kv_rank + MLA_ROPE]
    k_rope = jnp.concatenate([k_rope, jnp.zeros((z.shape[0], LANE - MLA_ROPE), F32)], axis=1)
    cos, s1, s2 = c_ref[...], s1_ref[...], s2_ref[...]
    sumsq = lambda x: jnp.sum(x * x, axis=-1, keepdims=True)
    inv_rms = lambda ss: lax.rsqrt(ss / MLA_QK + NORM_EPS)

    def rotary(x):
        return (x * cos + pltpu.roll(x, LANE - ROPE_FREQ, 1) * s1
                + pltpu.roll(x, ROPE_FREQ, 1) * s2)

    qg_n, qg_r = qg_ref[:, :MLA_NOPE], qg_ref[:, MLA_NOPE:]
    kg_n, kg_r = kg_ref[:, :MLA_NOPE], kg_ref[:, MLA_NOPE:]
    kr_ss = sumsq(k_rope)
    kr_rot = rotary(k_rope * kg_r)
    for h in range(heads):
        nope = slice(h * MLA_HEAD_PAD, h * MLA_HEAD_PAD + MLA_NOPE)
        rope = slice(h * MLA_HEAD_PAD + MLA_NOPE, (h + 1) * MLA_HEAD_PAD)
        qn, qr = q[:, nope], q[:, rope]
        q_inv = inv_rms(sumsq(qn) + sumsq(qr))
        q_out[:, nope] = (qn * q_inv * qg_n * q_scale).astype(BF16)
        q_out[:, rope] = (rotary(qr * q_inv * qg_r) * q_scale).astype(BF16)
        kn = kv[:, h * MLA_NOPE:(h + 1) * MLA_NOPE]
        k_inv = inv_rms(sumsq(kn) + kr_ss)
        k_out[:, nope] = (kn * k_inv * kg_n).astype(BF16)
        k_out[:, rope] = (kr_rot * k_inv).astype(BF16)
    v_out[...] = kv[:, heads * MLA_NOPE:].astype(BF16)


def _mla_prep(z, rw_cols, p, rope):
    t, n = z.shape
    heads = p["heads"]
    q_rank, kv_rank = p["q_up"].shape[0], p["kv_up"].shape[0]
    zw = min(w for w in range(LANE, n + 1, LANE) if n % w == 0 and n - w <= rw_cols)
    bm = _pick(t, (256, 128, 64))
    qw = heads * MLA_HEAD_PAD
    row = lambda w: pl.BlockSpec((bm, w), lambda i: (i, 0))
    const = lambda a: pl.BlockSpec(a.shape, lambda i: (0, 0))
    args = (z, p["q_norm"], p["q_up"], p["kv_norm"], p["kv_up"], p["q_gain"], p["k_gain"])
    return pl.pallas_call(
        functools.partial(_mla_prep_kernel, off=rw_cols - (n - zw), q_rank=q_rank, kv_rank=kv_rank,
                          heads=heads, q_scale=ATTN_Q_SCALE),
        out_shape=[jax.ShapeDtypeStruct((t, qw), BF16), jax.ShapeDtypeStruct((t, qw), BF16),
                   jax.ShapeDtypeStruct((t, heads * MLA_V_DIM), BF16)],
        grid=(t // bm,),
        in_specs=([pl.BlockSpec((bm, zw), lambda i: (i, n // zw - 1))]
                  + [const(a) for a in args[1:]] + [row(LANE)] * 3),
        out_specs=[row(qw), row(qw), row(heads * MLA_V_DIM)],
        compiler_params=_params(("parallel",)),
        name="mla_prep",
    )(*args, *rope)


def _attn_kernel(q_ref, k_ref, vt_ref, o_ref, *, n_lat, bq, hps, with_ctx):
    def attend(key0, n_keys):
        bk = _pick(n_keys, (1408, 768, 512, 256, 128))
        qcols = lambda c: slice(c * MLA_HEAD_PAD, (c + 1) * MLA_HEAD_PAD)
        q = [q_ref[:, qcols(c)] for c in range(hps)]
        keys = lambda j: slice(key0 + j * bk, key0 + (j + 1) * bk)
        scores = lambda j: [_dot_nt(k_ref[keys(j), qcols(c)], q[c]) for c in range(hps)]
        s_next = scores(0)
        m = [jnp.full((1, bq), -jnp.inf, F32)] * hps
        acc = [jnp.zeros((V_ROWS, bq), F32)] * hps
        for j in range(n_keys // bk):
            s = s_next
            if (j + 1) * bk < n_keys:
                s_next = scores(j + 1)
            for c in range(hps):
                m_new = jnp.maximum(m[c], jnp.max(s[c], axis=0, keepdims=True))
                p = jnp.exp2(s[c] - m_new).astype(BF16)
                acc[c] = (jnp.exp2(m[c] - m_new) * acc[c]
                          + _dot(vt_ref[c * V_ROWS:(c + 1) * V_ROWS, keys(j)], p))
                m[c] = m_new
        for c in range(hps):
            o_ref[:, c * MLA_V_DIM:(c + 1) * MLA_V_DIM] = (
                acc[c][:MLA_V_DIM] / acc[c][MLA_V_DIM:MLA_V_DIM + 1]).T.astype(o_ref.dtype)

    n_all = k_ref.shape[0]
    if not with_ctx:
        attend(0, n_all)
        return
    is_latent = pl.program_id(1) < n_lat // bq

    @pl.when(is_latent)
    def _():
        attend(0, n_all)

    @pl.when(jnp.logical_not(is_latent))
    def _():
        attend(n_lat, n_all - n_lat)


def _attention(q, k, vt, heads, n_lat, with_ctx):
    t = k.shape[0]
    n_q = t if with_ctx else n_lat
    bq = _pick(math.gcd(n_lat, t - n_lat), (256, 128, 64))
    hps = 4
    return pl.pallas_call(
        functools.partial(_attn_kernel, n_lat=n_lat, bq=bq, hps=hps, with_ctx=with_ctx),
        out_shape=jax.ShapeDtypeStruct((n_q, heads * MLA_V_DIM), BF16),
        grid=(heads // hps, n_q // bq),
        in_specs=[pl.BlockSpec((bq, hps * MLA_HEAD_PAD), lambda h, i: (i, h)),
                  _resident((t, hps * MLA_HEAD_PAD), lambda h, i: (0, h)),
                  _resident((hps * V_ROWS, t), lambda h, i: (h, 0))],
        out_specs=pl.BlockSpec((bq, hps * MLA_V_DIM), lambda h, i: (i, h)),
        compiler_params=_params(("parallel", "parallel")),
        name="attention",
    )(q, k, vt)


def _pad_cols(w, width):
    return jnp.pad(w, ((0, 0), (0, width - w.shape[1])))


def _rope_tables(n_lat, n_ctx):
    t = jnp.arange(n_lat, dtype=jnp.int32)
    pos = jnp.stack([t // GRID_W, t % GRID_W], axis=-1).astype(F32)
    inv_freq = ROPE_THETA ** (-jnp.arange(ROPE_FREQ, dtype=F32) / ROPE_FREQ)
    ang = pos[:, :, None] * inv_freq
    cos, sin = jnp.cos(ang), jnp.sin(ang)
    zero = jnp.zeros_like(sin)
    lay = lambda first, second: jnp.stack([first, second], axis=2).reshape(n_lat, MLA_ROPE)
    tabs = []
    for rope_part, fill in ((lay(cos, cos), 1.0), (lay(-sin, zero), 0.0), (lay(zero, sin), 0.0)):
        tab = jnp.concatenate([rope_part, jnp.full((n_lat, LANE - MLA_ROPE), fill, F32)], axis=1)
        tabs.append(jnp.concatenate([tab, jnp.full((n_ctx, LANE), fill, F32)], axis=0))
    return tuple(tabs)


def _layer_params(l, rw_mu, rw_w0, rw_w_up, rw_a0, rw_a_up, rw_g_up, rw_k_k, rw_k_a, rw_r_k,
                  rw_ln_w, rw_ln_b, mla_q_norm, mla_q_up, mla_kv_norm, mla_kv_up, mla_q_gain,
                  mla_k_gain):
    hw = rw_k_k.shape[1]
    ld, la, lg = rw_w_up.shape[2], rw_a_up.shape[2], rw_g_up.shape[1]
    q_rank, kv_rank = mla_q_up.shape[1], mla_kv_up.shape[1]
    heads = mla_kv_up.shape[2] // (MLA_NOPE + MLA_V_DIM)
    rw_cols = 3 * hw + ld + la + lg
    zw = -(-rw_cols // LANE) * LANE
    assert ld <= LANE and la <= LANE and rw_cols % 4 == 0 and 3 * hw + ld + LANE <= zw
    code = np.minimum(np.arange(zw, dtype=np.int32) // (rw_cols // 4), 3)[None, :]
    bd = (np.arange(LANE)[:, None] // RW_HEAD == np.arange(LANE)[None, :] // RW_HEAD)
    pad_rows = lambda x: jnp.pad(x, ((0, 0), (0, LANE - x.shape[1]), (0, 0)))
    return {
        "hw": hw, "heads": heads, "rw_cols": rw_cols, "zw": zw, "ld": ld, "la": la,
        "mu": _pad_cols(rw_mu[l][None, :], zw),
        "code": jnp.asarray(code),
        "k_k": rw_k_k[l][None, :], "k_a": rw_k_a[l][None, :], "r_k": rw_r_k[l].reshape(1, hw),
        "w0": rw_w0[l], "a0": rw_a0[l],
        "wup": pad_rows(rw_w_up[l]).astype(BF16), "aup": pad_rows(rw_a_up[l]).astype(BF16),
        "gup": rw_g_up[l].astype(BF16),
        "bd": jnp.asarray(bd, BF16),
        "ln_w": rw_ln_w[l][None, :], "ln_b": rw_ln_b[l][None, :],
        "q_norm": mla_q_norm[l][None, :], "kv_norm": mla_kv_norm[l][None, :],
        "q_up": jnp.pad(mla_q_up[l].reshape(q_rank, heads, MLA_QK),
                        ((0, 0), (0, 0), (0, MLA_HEAD_PAD - MLA_QK))
                        ).reshape(q_rank, heads * MLA_HEAD_PAD).astype(BF16),
        "kv_up": mla_kv_up[l].reshape(kv_rank, heads, 2, MLA_NOPE).transpose(0, 2, 1, 3)
                             .reshape(kv_rank, 2 * heads * MLA_NOPE).astype(BF16),
        "q_gain": _pad_cols(mla_q_gain[l][None, :], MLA_HEAD_PAD),
        "k_gain": _pad_cols(mla_k_gain[l][None, :], MLA_HEAD_PAD),
    }


def _ffn_half_step(xt, mod, g, ffn_w_in, ffn_w_out, lead, n_lat):
    h = _norm_mod(xt, g, mod[:, 0], mod[:, 1], n_lat)
    act = _swiglu_matmul(h, ffn_w_in, lead)
    return _residual_matmul([act], ffn_w_out, lead, xt, FFN_RES * mod[:, 2], n_lat)


def kernel(x, c, ctx, c_ctx, w_mod, b_mod, norm_g, ffn_w_in, ffn_w_out, w_in, w_out, rw_mu, rw_w0,
           rw_w_up, rw_a0, rw_a_up, rw_g_up, rw_k_k, rw_k_a, rw_r_k, rw_ln_w, rw_ln_b, mla_q_norm,
           mla_q_up, mla_kv_norm, mla_kv_up, mla_q_gain, mla_k_gain):
    batch, n_lat, d = x.shape
    n_ctx = ctx.shape[1]
    depth = w_mod.shape[0]
    assert batch == 1 and c.shape[0] == 1 and ctx.shape[0] == 1
    assert n_lat % GRID_W == 0 and n_ctx % GRID_W == 0

    xt = jnp.concatenate([x[0], ctx[0]], axis=0)
    cc = jnp.zeros((8, d), F32).at[0].set(c[0]).at[1].set(c_ctx)
    rope = _rope_tables(n_lat, n_ctx)

    for l in range(depth):
        last = l == depth - 1
        p = _layer_params(l, rw_mu, rw_w0, rw_w_up, rw_a0, rw_a_up, rw_g_up, rw_k_k, rw_k_a,
                          rw_r_k, rw_ln_w, rw_ln_b, mla_q_norm, mla_q_up, mla_kv_norm, mla_kv_up,
                          mla_q_gain, mla_k_gain)
        mod = _modulation(cc, w_mod, b_mod, l)[:2].reshape(2, N_MOD, d)

        xt = _ffn_half_step(xt, mod[:, 0:3], norm_g[l, 0], ffn_w_in, ffn_w_out, (l, 0), n_lat)

        hz = _norm_mod(xt, norm_g[l, 1], mod[:, 3], mod[:, 4], n_lat)
        z = _matmul(hz, w_in, (l,))

        st = _rw_streams(z, p, n_lat)
        y_f = _wkv(st, 0, n_lat, reverse=False)
        y_b = _wkv(st, 1, n_lat, reverse=True)
        o_rw = _rw_output(y_f, y_b, st, p)

        q, k, v = _mla_prep(z, p["rw_cols"], p, rope)
        vt = jnp.concatenate([v.T.reshape(p["heads"], MLA_V_DIM, -1),
                              jnp.ones((p["heads"], V_ROWS - MLA_V_DIM, v.shape[0]), BF16)], axis=1)
        o_mla = _attention(q, k, vt.reshape(p["heads"] * V_ROWS, -1), p["heads"], n_lat,
                           with_ctx=not last)

        xt = _residual_matmul([o_rw, o_mla], w_out, (l,), xt, mod[:, 5], n_lat,
                              rows=n_lat if last else None)

        xt = _ffn_half_step(xt, mod[:, 6:9], norm_g[l, 2], ffn_w_in, ffn_w_out, (l, 1), n_lat)
    return xt[:n_lat][None]
```

```python
import functools
import math

import numpy as np
import jax
import jax.numpy as jnp
from jax import lax
from jax.experimental import pallas as pl
from jax.experimental.pallas import tpu as pltpu

F32 = jnp.float32
BF16 = jnp.bfloat16
HI = lax.Precision.HIGHEST

GRID_W = 64
NORM_EPS = 1e-6
FFN_RES = 0.5
RW_HEAD = 64
RW_GN_EPS = 64e-5
RW_DECAY_SCALE = math.exp(-0.5)
MLA_V_DIM = 128
MLA_NOPE = 128
MLA_ROPE = 64
MLA_QK = MLA_NOPE + MLA_ROPE
MLA_HEAD_PAD = 256
ROPE_FREQ = MLA_ROPE // 4
ROPE_THETA = 10000.0
N_MOD = 9
LANE = 128
CHUNK = 64
ATTN_Q_SCALE = math.log2(math.e) / math.sqrt(MLA_QK)
V_ROWS = MLA_V_DIM + 16
VMEM_LIMIT = 60 * 1024 * 1024


def _pick(n, candidates):
    for c in candidates:
        if n % c == 0:
            return c
    raise ValueError(f"no block size for {n} among {candidates}")


def _params(sem):
    return pltpu.CompilerParams(dimension_semantics=sem, vmem_limit_bytes=VMEM_LIMIT)


def _dot(a, b, precision=None):
    return jnp.dot(a, b, preferred_element_type=F32, precision=precision)


def _dot_nt(a, b, precision=None):
    return lax.dot_general(a, b, (((1,), (1,)), ((), ())), preferred_element_type=F32,
                           precision=precision)


def _dot_tn(a, b, precision=None):
    return lax.dot_general(a, b, (((0,), (0,)), ((), ())), preferred_element_type=F32,
                           precision=precision)


def _bdot(a, b):
    return _dot(a.astype(BF16), b.astype(BF16))


def _bdot_nt(a, b):
    return _dot_nt(a.astype(BF16), b.astype(BF16))


def _wspec(lead, block, index_map):
    return pl.BlockSpec((None,) * len(lead) + block, lambda *g: lead + index_map(*g))


def _row_select(row0, n_rows, n_lat, ref):
    rows = row0 + lax.broadcasted_iota(jnp.int32, (n_rows, 1), 0)
    return jnp.where(rows < n_lat, ref[0:1, :], ref[1:2, :])


def _mod_kernel(c_ref, w_ref, b_ref, o_ref):
    c = c_ref[...]
    a = (c * jax.nn.sigmoid(c)).astype(BF16)
    o_ref[...] = _dot(a, w_ref[...].astype(BF16)) + b_ref[...]


def _modulation(cc, w_mod, b_mod, l):
    _, d, n = w_mod.shape
    bn = _pick(n, (512, 256, 128))
    return pl.pallas_call(
        _mod_kernel,
        out_shape=jax.ShapeDtypeStruct((8, n), F32),
        grid=(n // bn,),
        in_specs=[pl.BlockSpec((8, d), lambda j: (0, 0)),
                  _wspec((l,), (d, bn), lambda j: (0, j)),
                  _wspec((l,), (1, bn), lambda j: (0, j))],
        out_specs=pl.BlockSpec((8, bn), lambda j: (0, j)),
        compiler_params=_params(("parallel",)),
        name="modulation",
    )(cc, w_mod, b_mod.reshape(b_mod.shape[0], 1, n))


def _norm_mod_kernel(x_ref, g_ref, sh_ref, sc_ref, o_ref, *, n_lat, bm):
    x = x_ref[...]
    y = x * lax.rsqrt(jnp.mean(x * x, axis=-1, keepdims=True) + NORM_EPS) * g_ref[...]
    kind = pl.ds((pl.program_id(0) * bm >= n_lat).astype(jnp.int32), 1)
    o_ref[...] = (y * (1.0 + sc_ref[kind, :]) + sh_ref[kind, :]).astype(o_ref.dtype)


def _norm_mod(x, g, shift, scale, n_lat):
    m, d = x.shape
    bm = _pick(math.gcd(n_lat, m - n_lat) if m > n_lat else m, (256, 128, 64, 32, 16))
    return pl.pallas_call(
        functools.partial(_norm_mod_kernel, n_lat=n_lat, bm=bm),
        out_shape=jax.ShapeDtypeStruct((m, d), BF16),
        grid=(m // bm,),
        in_specs=[pl.BlockSpec((bm, d), lambda i: (i, 0)),
                  pl.BlockSpec((1, d), lambda i: (0, 0)),
                  pl.BlockSpec((2, d), lambda i: (0, 0)),
                  pl.BlockSpec((2, d), lambda i: (0, 0))],
        out_specs=pl.BlockSpec((bm, d), lambda i: (i, 0)),
        compiler_params=_params(("parallel",)),
        name="norm_mod",
    )(x, g.reshape(1, d), shift, scale)


def _resident(block_shape, index_map):
    return pl.BlockSpec(block_shape, index_map, pipeline_mode=pl.Buffered(1))


def _side_cast(side, side_lead, steps, nb):
    rows, cols = side.shape[-2:]
    sr = min(r for r in range(16, rows + 1, 16) if rows % r == 0 and rows // r <= steps)
    slab = lambda i, j: (jnp.minimum(i * nb + j, rows // sr - 1), 0)
    return (_wspec(side_lead, (sr, cols), slab), pl.BlockSpec((sr, cols), slab),
            jax.ShapeDtypeStruct((rows, cols), BF16))


def _mm_kernel(a_ref, w_ref, side_ref, o_ref, side_out):
    o_ref[...] = _dot(a_ref[...], w_ref[...].astype(BF16)).astype(o_ref.dtype)
    side_out[...] = side_ref[...].astype(BF16)


def _matmul(a, w, lead, side, side_lead):
    m, k = a.shape
    n = w.shape[-1]
    bm = _pick(m, (1056, 1024, 512, 320, 256, 128, 64))
    bn = _pick(n, (512, 256, 128))
    side_in, side_out, side_shape = _side_cast(side, side_lead, (m // bm) * (n // bn), n // bn)
    return pl.pallas_call(
        _mm_kernel,
        out_shape=[jax.ShapeDtypeStruct((m, n), F32), side_shape],
        grid=(m // bm, n // bn),
        in_specs=[_resident((bm, k), lambda i, j: (i, 0)),
                  _wspec(lead, (k, bn), lambda i, j: (0, j)), side_in],
        out_specs=[pl.BlockSpec((bm, bn), lambda i, j: (i, j)), side_out],
        compiler_params=_params(("parallel", "arbitrary")),
        name="matmul",
    )(a, w, side)


def _swiglu_kernel(a_ref, wg_ref, wu_ref, side_ref, o_ref, side_out):
    a = a_ref[...]
    gate = _dot(a, wg_ref[...].astype(BF16))
    up = _dot(a, wu_ref[...].astype(BF16))
    o_ref[...] = (gate * jax.nn.sigmoid(gate) * up).astype(o_ref.dtype)
    side_out[...] = side_ref[...].astype(BF16)


def _swiglu_matmul(a, w, lead, side):
    m, k = a.shape
    f = w.shape[-1] // 2
    bm = _pick(m, (1056, 1024, 512, 320, 256, 128, 64))
    bn = _pick(f, (512, 256, 128))
    nb = f // bn
    side_in, side_out, side_shape = _side_cast(side, lead, (m // bm) * nb, nb)
    return pl.pallas_call(
        _swiglu_kernel,
        out_shape=[jax.ShapeDtypeStruct((m, f), BF16), side_shape],
        grid=(m // bm, nb),
        in_specs=[_resident((bm, k), lambda i, j: (i, 0)),
                  _wspec(lead, (k, bn), lambda i, j: (0, j)),
                  _wspec(lead, (k, bn), lambda i, j: (0, j + nb)), side_in],
        out_specs=[pl.BlockSpec((bm, bn), lambda i, j: (i, j)), side_out],
        compiler_params=_params(("parallel", "arbitrary")),
        name="swiglu_matmul",
    )(a, w, w, side)


def _residual_kernel(*refs, n_a, n_lat, bm):
    a_refs = refs[:n_a]
    w_refs = refs[n_a:2 * n_a]
    x_ref, g_ref, o_ref = refs[2 * n_a:]
    acc = _dot(a_refs[0][...], w_refs[0][...])
    for a_ref, w_ref in zip(a_refs[1:], w_refs[1:]):
        acc = acc + _dot(a_ref[...], w_ref[...])
    gate = _row_select(pl.program_id(0) * bm, bm, n_lat, g_ref)
    o_ref[...] = x_ref[...] + gate * acc


def _residual_matmul(a_list, w, x, gate, n_lat, rows=None):
    m, n = (rows or x.shape[0]), x.shape[1]
    n_a = len(a_list)
    k = a_list[0].shape[1]
    assert all(a.shape[1] == k for a in a_list) and w.shape == (n_a * k, n) and w.dtype == BF16
    bm = _pick(m, (1056, 1024, 512, 320, 256, 128, 64))
    bn = _pick(n, (512, 256, 128))
    a_specs = [_resident((bm, k), lambda i, j: (i, 0)) for _ in a_list]
    w_specs = [pl.BlockSpec((k, bn), functools.partial(lambda i, j, p: (p, j), p=p))
               for p in range(n_a)]
    return pl.pallas_call(
        functools.partial(_residual_kernel, n_a=n_a, n_lat=n_lat, bm=bm),
        out_shape=jax.ShapeDtypeStruct((m, n), F32),
        grid=(m // bm, n // bn),
        in_specs=a_specs + w_specs + [pl.BlockSpec((bm, bn), lambda i, j: (i, j)),
                                      pl.BlockSpec((2, bn), lambda i, j: (0, j))],
        out_specs=pl.BlockSpec((bm, bn), lambda i, j: (i, j)),
        compiler_params=_params(("parallel", "arbitrary")),
        name="residual_matmul",
    )(*a_list, *([w] * n_a), x, gate)


def _segsum(x, bd):
    hi = x.astype(BF16)
    r1 = x - hi.astype(F32)
    mid = r1.astype(BF16)
    lo = (r1 - mid.astype(F32)).astype(BF16)
    parts = [_dot(hi[:, j:j + LANE], bd) + _dot(mid[:, j:j + LANE], bd) + _dot(lo[:, j:j + LANE], bd)
             for j in range(0, x.shape[1], LANE)]
    return jnp.concatenate(parts, axis=1)


def _streams_kernel(zc_ref, zp_ref, zn_ref, mu_ref, code_ref, kkw_ref, ka_ref, rk_ref, w0_ref,
                    a0_ref, wup_ref, aup_ref, gup_ref, bd_ref,
                    r_out, v_out, kk_out, lw0_out, lw1_out, b0_out, b1_out, kd0_out, kd1_out,
                    g_out, bonus_out, zs_ref, *, rb, n_lat_blocks, n_blocks, hw, ld, la, lg,
                    rw_cols):
    i = pl.program_id(0)
    rows = lax.broadcasted_iota(jnp.int32, (rb, 1), 0)
    quarter = rw_cols // 4
    n_tiles = zc_ref.shape[1] // LANE

    def token_shift(neighbour):
        for c in range(n_tiles):
            cols = slice(c * LANE, (c + 1) * LANE)
            z = zc_ref[:, cols]
            kinds = sorted({min(col // quarter, 3) for col in (c * LANE, (c + 1) * LANE - 1)})
            shifted = neighbour(kinds[-1], cols)
            for kind in reversed(kinds[:-1]):
                shifted = jnp.where(code_ref[:, cols] <= kind, neighbour(kind, cols), shifted)
            zs_ref[:, cols] = z + (shifted - z) * mu_ref[:, cols]

    prev_tok = lambda cols: pltpu.roll(zc_ref[:, cols], 1, 0)
    next_tok = lambda cols: pltpu.roll(zc_ref[:, cols], rb - 1, 0)

    @pl.when(i < n_lat_blocks)
    def _():
        def neighbour(kind, cols):
            if kind == 0:
                return jnp.where(rows % GRID_W == 0, 0.0, prev_tok(cols))
            if kind == 1:
                return jnp.where(rows % GRID_W == GRID_W - 1, 0.0, next_tok(cols))
            if kind == 2:
                above = zp_ref[:, cols] * (i > 0).astype(F32)
                return jnp.concatenate([above, zc_ref[:rb - GRID_W, cols]], axis=0)
            below = zn_ref[:, cols] * (i < n_lat_blocks - 1).astype(F32)
            return jnp.concatenate([zc_ref[GRID_W:, cols], below], axis=0)
        token_shift(neighbour)

    @pl.when(i >= n_lat_blocks)
    def _():
        def neighbour(kind, cols):
            if kind < 2:
                before = zp_ref[GRID_W - 1:GRID_W, cols] * (i > n_lat_blocks).astype(F32)
                return jnp.where(rows == 0, before, prev_tok(cols))
            after = zn_ref[0:1, cols] * (i < n_blocks - 1).astype(F32)
            return jnp.where(rows == rb - 1, after, next_tok(cols))
        token_shift(neighbour)

    zs = zs_ref[...]

    r = zs[:, 0:hw]
    k = zs[:, hw:2 * hw]
    v = zs[:, 2 * hw:3 * hw]
    wd = zs[:, 3 * hw:3 * hw + LANE]
    ad = zs[:, 3 * hw + ld:3 * hw + ld + LANE]
    gd = zs[:, 3 * hw + ld + la:3 * hw + ld + la + lg]
    bd = bd_ref[...]

    kk = k * kkw_ref[...]
    kk = kk / jnp.maximum(jnp.sqrt(_segsum(kk * kk, bd)), 1e-12)
    tw = jnp.tanh(wd).astype(BF16)
    adb = ad.astype(BF16)
    ka = ka_ref[...]
    kds = []
    for d, (lw_out, b_out, kd_out) in enumerate(((lw0_out, b0_out, kd0_out),
                                                 (lw1_out, b1_out, kd1_out))):
        w_logit = w0_ref[d:d + 1, :] + _dot(tw, wup_ref[d])
        lw_out[...] = -RW_DECAY_SCALE * jax.nn.sigmoid(w_logit)
        a = jax.nn.sigmoid(a0_ref[d:d + 1, :] + _dot(adb, aup_ref[d]))
        kd = k * (1.0 + (a - 1.0) * ka)
        kd_out[...] = kd.astype(kd_out.dtype)
        b_out[...] = (kk * a).astype(b_out.dtype)
        kds.append(kd)
    g_out[...] = _dot(jax.nn.sigmoid(gd).astype(BF16), gup_ref[...]).astype(g_out.dtype)
    kmean = (kds[0] + kds[1]) / 2.0
    bonus_out[...] = (_segsum(r * kmean * rk_ref[...], bd) * v).astype(bonus_out.dtype)
    r_out[...] = r.astype(r_out.dtype)
    v_out[...] = v.astype(v_out.dtype)
    kk_out[...] = kk.astype(kk_out.dtype)


def _rw_streams(z, p, n_lat):
    t = z.shape[0]
    hw, zw = p["hw"], p["zw"]
    lg = p["gup"].shape[0]
    rb = _pick(math.gcd(n_lat, t - n_lat), (256, 128, 64))
    n_blocks = t // rb
    n_lat_blocks = n_lat // rb
    halo = lambda f: pl.BlockSpec((GRID_W, zw), f)
    per = rb // GRID_W
    const2 = lambda shape: pl.BlockSpec(shape, lambda i: (0, 0))
    const3 = lambda shape: pl.BlockSpec(shape, lambda i: (0, 0, 0))
    out_spec = pl.BlockSpec((rb, hw), lambda i: (i, 0))
    out_dtypes = {"lw0": F32, "lw1": F32}
    names = ("r", "v", "kk", "lw0", "lw1", "b0", "b1", "kd0", "kd1", "g", "bonus")
    outs = pl.pallas_call(
        functools.partial(_streams_kernel, rb=rb, n_lat_blocks=n_lat_blocks, n_blocks=n_blocks,
                          hw=hw, ld=p["ld"], la=p["la"], lg=lg, rw_cols=p["rw_cols"]),
        out_shape=[jax.ShapeDtypeStruct((t, hw), out_dtypes.get(n, BF16)) for n in names],
        grid=(n_blocks,),
        in_specs=[pl.BlockSpec((rb, zw), lambda i: (i, 0)),
                  halo(lambda i: (jnp.maximum(i * per - 1, 0), 0)),
                  halo(lambda i: (jnp.minimum((i + 1) * per, t // GRID_W - 1), 0)),
                  const2((1, zw)), const2((1, zw)),
                  const2((1, hw)), const2((1, hw)), const2((1, hw)),
                  const2((2, hw)), const2((2, hw)),
                  const3((2, LANE, hw)), const3((2, LANE, hw)), const2((lg, hw)),
                  const2((LANE, LANE))],
        out_specs=[out_spec] * 11,
        scratch_shapes=[pltpu.VMEM((rb, zw), F32)],
        compiler_params=_params(("parallel",)),
        name="rw_streams",
    )(z, z, z, p["mu"], p["code"], p["k_k"], p["k_a"], p["r_k"], p["w0"], p["a0"],
      p["wup"], p["aup"], p["gup"], p["bd"])
    return dict(zip(names, outs))


def _wkv_kernel(lw_ref, kk_ref, b_ref, kd_ref, r_ref, v_ref, y_ref, st_ref, *, heads, tb, reverse):
    @pl.when(pl.program_id(1) == 0)
    def _():
        st_ref[...] = jnp.zeros(st_ref.shape, F32)

    n_chunks = tb // CHUNK
    t_idx = lax.broadcasted_iota(jnp.int32, (CHUNK, CHUNK), 0)
    s_idx = lax.broadcasted_iota(jnp.int32, (CHUNK, CHUNK), 1)
    strict = (s_idx > t_idx) if reverse else (s_idx < t_idx)
    incl = (s_idx >= t_idx) if reverse else (s_idx <= t_idx)
    eye = (s_idx == t_idx).astype(F32)

    bt = lax.broadcasted_iota(jnp.int32, (tb, tb), 0)
    bs = lax.broadcasted_iota(jnp.int32, (tb, tb), 1)
    before = (bs >= bt) if reverse else (bs <= bt)
    tri = jnp.where(((bt // CHUNK) == (bs // CHUNK)) & before, 1.0, 0.0).astype(BF16)
    lw = lw_ref[...]
    lw_hi = lw.astype(BF16)
    lw_lo = (lw - lw_hi.astype(F32)).astype(BF16)
    cum = _dot(tri, lw_hi) + _dot(tri, lw_lo)
    edge = 0 if reverse else CHUNK - 1
    tot = jnp.concatenate(
        [jnp.broadcast_to(cum[cc * CHUNK + edge:cc * CHUNK + edge + 1], (CHUNK, cum.shape[1]))
         for cc in range(n_chunks)], axis=0)
    kk, b, kd = (x[...].astype(F32) for x in (kk_ref, b_ref, kd_ref))
    e_neg = jnp.exp(-cum)
    e_end = jnp.exp(tot - cum)
    alpha = kk * jnp.exp(cum - lw)
    rho = r_ref[...].astype(F32) * jnp.exp(cum)
    ar_all = (alpha.astype(BF16), rho.astype(BF16))
    bk_all = ((b * e_neg).astype(BF16), (kd * e_neg).astype(BF16))
    beta_e = (b * e_end).astype(BF16)
    kappa_e = (kd * e_end).astype(BF16)
    g_end = jnp.exp(tot)
    v_all = v_ref[...].astype(BF16)

    pairs = [(cc, h) for cc in range(n_chunks) for h in range(heads)]
    rs = lambda cc: slice(cc * CHUNK, (cc + 1) * CHUNK)
    ls = lambda h: slice(h * RW_HEAD, (h + 1) * RW_HEAD)
    cut = lambda x, cc, h: x[rs(cc), ls(h)]
    stack = lambda xs, cc, h: jnp.concatenate([cut(x, cc, h) for x in xs], axis=0)
    m = [_dot_nt(stack(ar_all, cc, h), stack(bk_all, cc, h)) for cc, h in pairs]
    pw = [jnp.where(strict, -x[:CHUNK, :CHUNK], 0.0) for x in m]
    m_akrk = [jnp.concatenate([jnp.where(strict, x[:CHUNK, CHUNK:], 0.0),
                               jnp.where(incl, x[CHUNK:, CHUNK:], 0.0)], axis=0).astype(BF16)
              for x in m]
    m_rb = [jnp.where(incl, x[CHUNK:, :CHUNK], 0.0).astype(BF16) for x in m]
    t_inv = [eye + x for x in pw]
    for _ in range(int(math.log2(CHUNK)) - 1):
        pwb = [x.astype(BF16) for x in pw]
        pw = [_dot(x, x) for x in pwb]
        t_inv = [t + _dot(t.astype(BF16), x.astype(BF16)) for t, x in zip(t_inv, pw)]
    mv = [_dot(x, cut(v_all, cc, h)) for x, (cc, h) in zip(m_akrk, pairs)]
    w12 = [_dot(t.astype(BF16),
                jnp.concatenate([x[:CHUNK].astype(BF16), cut(ar_all[0], cc, h)], axis=1)
                ).astype(BF16)
           for t, x, (cc, h) in zip(t_inv, mv, pairs)]
    mw = [_dot(x, w) for x, w in zip(m_rb, w12)]
    bw = [_dot_tn(cut(beta_e, cc, h), w) for w, (cc, h) in zip(w12, pairs)]
    kv = [_dot_tn(cut(kappa_e, cc, h), cut(v_all, cc, h)) for cc, h in pairs]
    pre = {}
    for i, (cc, h) in enumerate(pairs):
        y1 = mv[i][CHUNK:] - mw[i][:, :RW_HEAD]
        r2 = cut(rho, cc, h) - mw[i][:, RW_HEAD:]
        g_m = kv[i] - bw[i][:, :RW_HEAD]
        p_m = eye * g_end[cc * CHUNK:cc * CHUNK + 1, ls(h)] - bw[i][:, RW_HEAD:]
        pre[cc, h] = (y1, jnp.concatenate([r2, p_m], axis=0).astype(BF16), g_m)

    st = [st_ref[h] for h in range(heads)]
    ys = {}
    for cc in (range(n_chunks - 1, -1, -1) if reverse else range(n_chunks)):
        prod = [_dot(pre[cc, h][1], st[h].astype(BF16)) for h in range(heads)]
        for h in range(heads):
            ys[cc, h] = pre[cc, h][0] + prod[h][:CHUNK]
            st[h] = prod[h][CHUNK:] + pre[cc, h][2]
    for h in range(heads):
        st_ref[h] = st[h]
    for cc in range(n_chunks):
        y_ref[rs(cc), :] = jnp.concatenate([ys[cc, h] for h in range(heads)], axis=1)


def _wkv(st, d, n_lat, reverse):
    lw, b, kd = st[f"lw{d}"], st[f"b{d}"], st[f"kd{d}"]
    t, hw = lw.shape
    heads = _pick(hw // RW_HEAD, (16, 8, 4, 2))
    tb = _pick(math.gcd(n_lat, t - n_lat), (256, 128, 64))
    nb = t // tb
    n_lat_blocks = n_lat // tb
    if reverse:
        blk = lambda h, c: (nb - 1 - c, h)
    else:
        blk = lambda h, c: ((c + n_lat_blocks) % nb, h)
    spec = pl.BlockSpec((tb, heads * RW_HEAD), blk)
    return pl.pallas_call(
        functools.partial(_wkv_kernel, heads=heads, tb=tb, reverse=reverse),
        out_shape=jax.ShapeDtypeStruct((t, hw), F32),
        grid=(hw // (heads * RW_HEAD), nb),
        in_specs=[spec] * 6,
        out_specs=spec,
        scratch_shapes=[pltpu.VMEM((heads, RW_HEAD, RW_HEAD), F32)],
        compiler_params=_params(("parallel", "arbitrary")),
        name="wkv_bwd" if reverse else "wkv_fwd",
    )(lw, st["kk"], b, kd, st["r"], st["v"])


def _rw_out_kernel(yf_ref, yb_ref, bonus_ref, g_ref, lnw_ref, lnb_ref, bd_ref, o_ref):
    y = yf_ref[...] + yb_ref[...]
    bd = bd_ref[...]
    mean = _segsum(y, bd) / RW_HEAD
    yc = y - mean
    var = _segsum(yc * yc, bd) / RW_HEAD
    o = yc * lax.rsqrt(var + RW_GN_EPS) * lnw_ref[...] + lnb_ref[...] + bonus_ref[...].astype(F32)
    o_ref[...] = (o * g_ref[...].astype(F32)).astype(o_ref.dtype)


def _rw_output(y_f, y_b, st, p):
    t, hw = y_f.shape
    bm = _pick(t, (256, 128, 64))
    row = pl.BlockSpec((bm, hw), lambda i: (i, 0))
    vec = pl.BlockSpec((1, hw), lambda i: (0, 0))
    return pl.pallas_call(
        _rw_out_kernel,
        out_shape=jax.ShapeDtypeStruct((t, hw), BF16),
        grid=(t // bm,),
        in_specs=[row, row, row, row, vec, vec, pl.BlockSpec((LANE, LANE), lambda i: (0, 0))],
        out_specs=row,
        compiler_params=_params(("parallel",)),
        name="rw_output",
    )(y_f, y_b, st["bonus"], st["g"], p["ln_w"], p["ln_b"], p["bd"])


def _mla_prep_kernel(z_ref, qn_ref, qup_ref, kvn_ref, kvup_ref, qg_ref, kg_ref, c_ref, s1_ref,
                     s2_ref, q_out, k_out, v_out, *, off, q_rank, kv_rank, heads, q_scale):
    z = z_ref[...]

    def rms(x, g):
        return x * lax.rsqrt(jnp.mean(x * x, axis=-1, keepdims=True) + NORM_EPS) * g

    q = _dot(rms(z[:, off:off + q_rank], qn_ref[...]).astype(BF16), qup_ref[...])
    kv_lat = z[:, off + q_rank:off + q_rank + kv_rank]
    kv = _dot(rms(kv_lat, kvn_ref[...]).astype(BF16), kvup_ref[...])
    k_rope = z[:, off + q_rank + kv_rank:off + q_rank + kv_rank + MLA_ROPE]
    k_rope = jnp.concatenate([k_rope, jnp.zeros((z.shape[0], LANE - MLA_ROPE), F32)], axis=1)
    cos, s1, s2 = c_ref[...], s1_ref[...], s2_ref[...]
    sumsq = lambda x: jnp.sum(x * x, axis=-1, keepdims=True)
    inv_rms = lambda ss: lax.rsqrt(ss / MLA_QK + NORM_EPS)

    def rotary(x):
        return (x * cos + pltpu.roll(x, LANE - ROPE_FREQ, 1) * s1
                + pltpu.roll(x, ROPE_FREQ, 1) * s2)

    qg_n, qg_r = qg_ref[:, :MLA_NOPE], qg_ref[:, MLA_NOPE:]
    kg_n, kg_r = kg_ref[:, :MLA_NOPE], kg_ref[:, MLA_NOPE:]
    kr_ss = sumsq(k_rope)
    kr_rot = rotary(k_rope * kg_r)
    for h in range(heads):
        nope = slice(h * MLA_HEAD_PAD, h * MLA_HEAD_PAD + MLA_NOPE)
        rope = slice(h * MLA_HEAD_PAD + MLA_NOPE, (h + 1) * MLA_HEAD_PAD)
        qn, qr = q[:, nope], q[:, rope]
        q_inv = inv_rms(sumsq(qn) + sumsq(qr))
        q_out[:, nope] = (qn * q_inv * qg_n * q_scale).astype(BF16)
        q_out[:, rope] = (rotary(qr * q_inv * qg_r) * q_scale).astype(BF16)
        kn = kv[:, h * MLA_NOPE:(h + 1) * MLA_NOPE]
        k_inv = inv_rms(sumsq(kn) + kr_ss)
        k_out[:, nope] = (kn * k_inv * kg_n).astype(BF16)
        k_out[:, rope] = (kr_rot * k_inv).astype(BF16)
    v_out[...] = kv[:, heads * MLA_NOPE:].astype(BF16)


def _mla_prep(z, rw_cols, p, rope):
    t, n = z.shape
    heads = p["heads"]
    q_rank, kv_rank = p["q_up"].shape[0], p["kv_up"].shape[0]
    zw = min(w for w in range(LANE, n + 1, LANE) if n % w == 0 and n - w <= rw_cols)
    bm = _pick(t, (256, 128, 64))
    qw = heads * MLA_HEAD_PAD
    row = lambda w: pl.BlockSpec((bm, w), lambda i: (i, 0))
    const = lambda a: pl.BlockSpec(a.shape, lambda i: (0, 0))
    args = (z, p["q_norm"], p["q_up"], p["kv_norm"], p["kv_up"], p["q_gain"], p["k_gain"])
    return pl.pallas_call(
        functools.partial(_mla_prep_kernel, off=rw_cols - (n - zw), q_rank=q_rank, kv_rank=kv_rank,
                          heads=heads, q_scale=ATTN_Q_SCALE),
        out_shape=[jax.ShapeDtypeStruct((t, qw), BF16), jax.ShapeDtypeStruct((t, qw), BF16),
                   jax.ShapeDtypeStruct((t, heads * MLA_V_DIM), BF16)],
        grid=(t // bm,),
        in_specs=([pl.BlockSpec((bm, zw), lambda i: (i, n // zw - 1))]
                  + [const(a) for a in args[1:]] + [row(LANE)] * 3),
        out_specs=[row(qw), row(qw), row(heads * MLA_V_DIM)],
        compiler_params=_params(("parallel",)),
        name="mla_prep",
    )(*args, *rope)


def _attn_kernel(q_ref, k_ref, vt_ref, o_ref, *, n_lat, bq, hps, with_ctx):
    def attend(key0, n_keys):
        bk = _pick(n_keys, (1408, 768, 512, 256, 128))
        qcols = lambda c: slice(c * MLA_HEAD_PAD, (c + 1) * MLA_HEAD_PAD)
        q = [q_ref[:, qcols(c)] for c in range(hps)]
        keys = lambda j: slice(key0 + j * bk, key0 + (j + 1) * bk)
        scores = lambda j: [_dot_nt(k_ref[keys(j), qcols(c)], q[c]) for c in range(hps)]
        s_next = scores(0)
        m = [jnp.full((1, bq), -jnp.inf, F32)] * hps
        acc = [jnp.zeros((V_ROWS, bq), F32)] * hps
        for j in range(n_keys // bk):
            s = s_next
            if (j + 1) * bk < n_keys:
                s_next = scores(j + 1)
            for c in range(hps):
                m_new = jnp.maximum(m[c], jnp.max(s[c], axis=0, keepdims=True))
                p = jnp.exp2(s[c] - m_new).astype(BF16)
                acc[c] = (jnp.exp2(m[c] - m_new) * acc[c]
                          + _dot(vt_ref[c * V_ROWS:(c + 1) * V_ROWS, keys(j)], p))
                m[c] = m_new
        for c in range(hps):
            o_ref[:, c * MLA_V_DIM:(c + 1) * MLA_V_DIM] = (
                acc[c][:MLA_V_DIM] / acc[c][MLA_V_DIM:MLA_V_DIM + 1]).T.astype(o_ref.dtype)

    n_all = k_ref.shape[0]
    if not with_ctx:
        attend(0, n_all)
        return
    is_latent = pl.program_id(1) < n_lat // bq

    @pl.when(is_latent)
    def _():
        attend(0, n_all)

    @pl.when(jnp.logical_not(is_latent))
    def _():
        attend(n_lat, n_all - n_lat)


def _attention(q, k, vt, heads, n_lat, with_ctx):
    t = k.shape[0]
    n_q = t if with_ctx else n_lat
    bq = _pick(math.gcd(n_lat, t - n_lat), (256, 128, 64))
    hps = 4
    return pl.pallas_call(
        functools.partial(_attn_kernel, n_lat=n_lat, bq=bq, hps=hps, with_ctx=with_ctx),
        out_shape=jax.ShapeDtypeStruct((n_q, heads * MLA_V_DIM), BF16),
        grid=(heads // hps, n_q // bq),
        in_specs=[pl.BlockSpec((bq, hps * MLA_HEAD_PAD), lambda h, i: (i, h)),
                  _resident((t, hps * MLA_HEAD_PAD), lambda h, i: (0, h)),
                  _resident((hps * V_ROWS, t), lambda h, i: (h, 0))],
        out_specs=pl.BlockSpec((bq, hps * MLA_V_DIM), lambda h, i: (i, h)),
        compiler_params=_params(("parallel", "parallel")),
        name="attention",
    )(q, k, vt)


def _pad_cols(w, width):
    return jnp.pad(w, ((0, 0), (0, width - w.shape[1])))


def _rope_tables(n_lat, n_ctx):
    t = jnp.arange(n_lat, dtype=jnp.int32)
    pos = jnp.stack([t // GRID_W, t % GRID_W], axis=-1).astype(F32)
    inv_freq = ROPE_THETA ** (-jnp.arange(ROPE_FREQ, dtype=F32) / ROPE_FREQ)
    ang = pos[:, :, None] * inv_freq
    cos, sin = jnp.cos(ang), jnp.sin(ang)
    zero = jnp.zeros_like(sin)
    lay = lambda first, second: jnp.stack([first, second], axis=2).reshape(n_lat, MLA_ROPE)
    tabs = []
    for rope_part, fill in ((lay(cos, cos), 1.0), (lay(-sin, zero), 0.0), (lay(zero, sin), 0.0)):
        tab = jnp.concatenate([rope_part, jnp.full((n_lat, LANE - MLA_ROPE), fill, F32)], axis=1)
        tabs.append(jnp.concatenate([tab, jnp.full((n_ctx, LANE), fill, F32)], axis=0))
    return tuple(tabs)


def _layer_params(l, rw_mu, rw_w0, rw_w_up, rw_a0, rw_a_up, rw_g_up, rw_k_k, rw_k_a, rw_r_k,
                  rw_ln_w, rw_ln_b, mla_q_norm, mla_q_up, mla_kv_norm, mla_kv_up, mla_q_gain,
                  mla_k_gain):
    hw = rw_k_k.shape[1]
    ld, la, lg = rw_w_up.shape[2], rw_a_up.shape[2], rw_g_up.shape[1]
    q_rank, kv_rank = mla_q_up.shape[1], mla_kv_up.shape[1]
    heads = mla_kv_up.shape[2] // (MLA_NOPE + MLA_V_DIM)
    rw_cols = 3 * hw + ld + la + lg
    zw = -(-rw_cols // LANE) * LANE
    assert ld <= LANE and la <= LANE and rw_cols % 4 == 0 and 3 * hw + ld + LANE <= zw
    code = np.minimum(np.arange(zw, dtype=np.int32) // (rw_cols // 4), 3)[None, :]
    bd = (np.arange(LANE)[:, None] // RW_HEAD == np.arange(LANE)[None, :] // RW_HEAD)
    pad_rows = lambda x: jnp.pad(x, ((0, 0), (0, LANE - x.shape[1]), (0, 0)))
    return {
        "hw": hw, "heads": heads, "rw_cols": rw_cols, "zw": zw, "ld": ld, "la": la,
        "mu": _pad_cols(rw_mu[l][None, :], zw),
        "code": jnp.asarray(code),
        "k_k": rw_k_k[l][None, :], "k_a": rw_k_a[l][None, :], "r_k": rw_r_k[l].reshape(1, hw),
        "w0": rw_w0[l], "a0": rw_a0[l],
        "wup": pad_rows(rw_w_up[l]).astype(BF16), "aup": pad_rows(rw_a_up[l]).astype(BF16),
        "gup": rw_g_up[l].astype(BF16),
        "bd": jnp.asarray(bd, BF16),
        "ln_w": rw_ln_w[l][None, :], "ln_b": rw_ln_b[l][None, :],
        "q_norm": mla_q_norm[l][None, :], "kv_norm": mla_kv_norm[l][None, :],
        "q_up": jnp.pad(mla_q_up[l].reshape(q_rank, heads, MLA_QK),
                        ((0, 0), (0, 0), (0, MLA_HEAD_PAD - MLA_QK))
                        ).reshape(q_rank, heads * MLA_HEAD_PAD).astype(BF16),
        "kv_up": mla_kv_up[l].reshape(kv_rank, heads, 2, MLA_NOPE).transpose(0, 2, 1, 3)
                             .reshape(kv_rank, 2 * heads * MLA_NOPE).astype(BF16),
        "q_gain": _pad_cols(mla_q_gain[l][None, :], MLA_HEAD_PAD),
        "k_gain": _pad_cols(mla_k_gain[l][None, :], MLA_HEAD_PAD),
    }


def _ffn_half_step(xt, mod, g, ffn_w_in, ffn_w_out, lead, n_lat):
    h = _norm_mod(xt, g, mod[:, 0], mod[:, 1], n_lat)
    act, w_out_bf16 = _swiglu_matmul(h, ffn_w_in, lead, ffn_w_out)
    return _residual_matmul([act], w_out_bf16, xt, FFN_RES * mod[:, 2], n_lat)


def kernel(x, c, ctx, c_ctx, w_mod, b_mod, norm_g, ffn_w_in, ffn_w_out, w_in, w_out, rw_mu, rw_w0,
           rw_w_up, rw_a0, rw_a_up, rw_g_up, rw_k_k, rw_k_a, rw_r_k, rw_ln_w, rw_ln_b, mla_q_norm,
           mla_q_up, mla_kv_norm, mla_kv_up, mla_q_gain, mla_k_gain):
    batch, n_lat, d = x.shape
    n_ctx = ctx.shape[1]
    depth = w_mod.shape[0]
    assert batch == 1 and c.shape[0] == 1 and ctx.shape[0] == 1
    assert n_lat % GRID_W == 0 and n_ctx % GRID_W == 0

    xt = jnp.concatenate([x[0], ctx[0]], axis=0)
    cc = jnp.zeros((8, d), F32).at[0].set(c[0]).at[1].set(c_ctx)
    rope = _rope_tables(n_lat, n_ctx)

    for l in range(depth):
        last = l == depth - 1
        p = _layer_params(l, rw_mu, rw_w0, rw_w_up, rw_a0, rw_a_up, rw_g_up, rw_k_k, rw_k_a,
                          rw_r_k, rw_ln_w, rw_ln_b, mla_q_norm, mla_q_up, mla_kv_norm, mla_kv_up,
                          mla_q_gain, mla_k_gain)
        mod = _modulation(cc, w_mod, b_mod, l)[:2].reshape(2, N_MOD, d)

        xt = _ffn_half_step(xt, mod[:, 0:3], norm_g[l, 0], ffn_w_in, ffn_w_out, (l, 0), n_lat)

        hz = _norm_mod(xt, norm_g[l, 1], mod[:, 3], mod[:, 4], n_lat)
        z, w_out_bf16 = _matmul(hz, w_in, (l,), w_out, (l,))

        st = _rw_streams(z, p, n_lat)
        y_f = _wkv(st, 0, n_lat, reverse=False)
        y_b = _wkv(st, 1, n_lat, reverse=True)
        o_rw = _rw_output(y_f, y_b, st, p)

        q, k, v = _mla_prep(z, p["rw_cols"], p, rope)
        vt = jnp.concatenate([v.T.reshape(p["heads"], MLA_V_DIM, -1),
                              jnp.ones((p["heads"], V_ROWS - MLA_V_DIM, v.shape[0]), BF16)], axis=1)
        o_mla = _attention(q, k, vt.reshape(p["heads"] * V_ROWS, -1), p["heads"], n_lat,
                           with_ctx=not last)

        xt = _residual_matmul([o_rw, o_mla], w_out_bf16, xt, mod[:, 5], n_lat,
                              rows=n_lat if last else None)

        xt = _ffn_half_step(xt, mod[:, 6:9], norm_g[l, 2], ffn_w_in, ffn_w_out, (l, 1), n_lat)
    return xt[:n_lat][None]
```

```python
import functools
import math

import numpy as np
import jax
import jax.numpy as jnp
from jax import lax
from jax.experimental import pallas as pl
from jax.experimental.pallas import tpu as pltpu

F32 = jnp.float32
BF16 = jnp.bfloat16

GRID_W = 64
NORM_EPS = 1e-6
FFN_RES = 0.5
RW_HEAD = 64
RW_GN_EPS = 64e-5
RW_DECAY_SCALE = math.exp(-0.5)
MLA_V_DIM = 128
MLA_NOPE = 128
MLA_ROPE = 64
MLA_QK = MLA_NOPE + MLA_ROPE
MLA_HEAD_PAD = 256
ROPE_FREQ = MLA_ROPE // 4
ROPE_THETA = 10000.0
N_MOD = 9
LANE = 128
CHUNK = 64
ATTN_Q_SCALE = math.log2(math.e) / math.sqrt(MLA_QK)
V_ROWS = MLA_V_DIM + 16
VMEM_LIMIT = 60 * 1024 * 1024


def _pick(n, candidates):
    for c in candidates:
        if n % c == 0:
            return c
    raise ValueError(f"no block size for {n} among {candidates}")


def _params(sem):
    return pltpu.CompilerParams(dimension_semantics=sem, vmem_limit_bytes=VMEM_LIMIT)


def _dot(a, b):
    return jnp.dot(a, b, preferred_element_type=F32)


def _dot_nt(a, b):
    return lax.dot_general(a, b, (((1,), (1,)), ((), ())), preferred_element_type=F32)


def _dot_tn(a, b):
    return lax.dot_general(a, b, (((0,), (0,)), ((), ())), preferred_element_type=F32)


def _wspec(lead, block, index_map):
    return pl.BlockSpec((None,) * len(lead) + block, lambda *g: lead + index_map(*g))


def _row_select(row0, n_rows, n_lat, ref):
    rows = row0 + lax.broadcasted_iota(jnp.int32, (n_rows, 1), 0)
    return jnp.where(rows < n_lat, ref[0:1, :], ref[1:2, :])


def _mod_kernel(c_ref, w_ref, b_ref, o_ref):
    c = c_ref[...]
    a = (c * jax.nn.sigmoid(c)).astype(BF16)
    o_ref[...] = _dot(a, w_ref[...].astype(BF16)) + b_ref[...]


def _modulation(cc, w_mod, b_mod, l):
    _, d, n = w_mod.shape
    bn = _pick(n, (512, 256, 128))
    return pl.pallas_call(
        _mod_kernel,
        out_shape=jax.ShapeDtypeStruct((8, n), F32),
        grid=(n // bn,),
        in_specs=[pl.BlockSpec((8, d), lambda j: (0, 0)),
                  _wspec((l,), (d, bn), lambda j: (0, j)),
                  _wspec((l,), (1, bn), lambda j: (0, j))],
        out_specs=pl.BlockSpec((8, bn), lambda j: (0, j)),
        compiler_params=_params(("parallel",)),
        name="modulation",
    )(cc, w_mod, b_mod.reshape(b_mod.shape[0], 1, n))


def _norm_mod_kernel(x_ref, g_ref, sh_ref, sc_ref, o_ref, *, n_lat, bm):
    x = x_ref[...]
    y = x * lax.rsqrt(jnp.mean(x * x, axis=-1, keepdims=True) + NORM_EPS) * g_ref[...]
    kind = pl.ds((pl.program_id(0) * bm >= n_lat).astype(jnp.int32), 1)
    o_ref[...] = (y * (1.0 + sc_ref[kind, :]) + sh_ref[kind, :]).astype(o_ref.dtype)


def _norm_mod(x, g, shift, scale, n_lat):
    m, d = x.shape
    bm = _pick(math.gcd(n_lat, m - n_lat) if m > n_lat else m, (256, 128, 64, 32, 16))
    return pl.pallas_call(
        functools.partial(_norm_mod_kernel, n_lat=n_lat, bm=bm),
        out_shape=jax.ShapeDtypeStruct((m, d), BF16),
        grid=(m // bm,),
        in_specs=[pl.BlockSpec((bm, d), lambda i: (i, 0)),
                  pl.BlockSpec((1, d), lambda i: (0, 0)),
                  pl.BlockSpec((2, d), lambda i: (0, 0)),
                  pl.BlockSpec((2, d), lambda i: (0, 0))],
        out_specs=pl.BlockSpec((bm, d), lambda i: (i, 0)),
        compiler_params=_params(("parallel",)),
        name="norm_mod",
    )(x, g.reshape(1, d), shift, scale)


def _resident(block_shape, index_map):
    return pl.BlockSpec(block_shape, index_map, pipeline_mode=pl.Buffered(1))


def _side_cast(side, side_lead, steps, nb):
    rows, cols = side.shape[-2:]
    sr = min(r for r in range(16, rows + 1, 16) if rows % r == 0 and rows // r <= steps)
    slab = lambda i, j: (jnp.minimum(i * nb + j, rows // sr - 1), 0)
    return (_wspec(side_lead, (sr, cols), slab), pl.BlockSpec((sr, cols), slab),
            jax.ShapeDtypeStruct((rows, cols), BF16))


def _mm_kernel(a_ref, w_ref, side_ref, o_ref, side_out):
    o_ref[...] = _dot(a_ref[...], w_ref[...].astype(BF16)).astype(o_ref.dtype)
    side_out[...] = side_ref[...].astype(BF16)


def _matmul(a, w, lead, side, side_lead):
    m, k = a.shape
    n = w.shape[-1]
    bm = _pick(m, (1056, 1024, 512, 320, 256, 128, 64))
    bn = _pick(n, (512, 256, 128))
    side_in, side_out, side_shape = _side_cast(side, side_lead, (m // bm) * (n // bn), n // bn)
    return pl.pallas_call(
        _mm_kernel,
        out_shape=[jax.ShapeDtypeStruct((m, n), F32), side_shape],
        grid=(m // bm, n // bn),
        in_specs=[_resident((bm, k), lambda i, j: (i, 0)),
                  _wspec(lead, (k, bn), lambda i, j: (0, j)), side_in],
        out_specs=[pl.BlockSpec((bm, bn), lambda i, j: (i, j)), side_out],
        compiler_params=_params(("parallel", "arbitrary")),
        name="matmul",
    )(a, w, side)


def _swiglu_kernel(a_ref, wg_ref, wu_ref, side_ref, o_ref, side_out):
    a = a_ref[...]
    gate = _dot(a, wg_ref[...].astype(BF16))
    up = _dot(a, wu_ref[...].astype(BF16))
    o_ref[...] = (gate * jax.nn.sigmoid(gate) * up).astype(o_ref.dtype)
    side_out[...] = side_ref[...].astype(BF16)


def _swiglu_matmul(a, w, lead, side):
    m, k = a.shape
    f = w.shape[-1] // 2
    bm = _pick(m, (1056, 1024, 512, 320, 256, 128, 64))
    bn = _pick(f, (512, 256, 128))
    nb = f // bn
    side_in, side_out, side_shape = _side_cast(side, lead, (m // bm) * nb, nb)
    return pl.pallas_call(
        _swiglu_kernel,
        out_shape=[jax.ShapeDtypeStruct((m, f), BF16), side_shape],
        grid=(m // bm, nb),
        in_specs=[_resident((bm, k), lambda i, j: (i, 0)),
                  _wspec(lead, (k, bn), lambda i, j: (0, j)),
                  _wspec(lead, (k, bn), lambda i, j: (0, j + nb)), side_in],
        out_specs=[pl.BlockSpec((bm, bn), lambda i, j: (i, j)), side_out],
        compiler_params=_params(("parallel", "arbitrary")),
        name="swiglu_matmul",
    )(a, w, w, side)


def _residual_kernel(*refs, n_a, n_lat, bm):
    a_refs = refs[:n_a]
    w_refs = refs[n_a:2 * n_a]
    x_ref, g_ref, o_ref = refs[2 * n_a:]
    acc = _dot(a_refs[0][...], w_refs[0][...])
    for a_ref, w_ref in zip(a_refs[1:], w_refs[1:]):
        acc = acc + _dot(a_ref[...], w_ref[...])
    gate = _row_select(pl.program_id(0) * bm, bm, n_lat, g_ref)
    o_ref[...] = x_ref[...] + gate * acc


def _residual_matmul(a_list, w, x, gate, n_lat, rows=None):
    m, n = (rows or x.shape[0]), x.shape[1]
    n_a = len(a_list)
    k = a_list[0].shape[1]
    assert all(a.shape[1] == k for a in a_list) and w.shape == (n_a * k, n) and w.dtype == BF16
    bm = _pick(m, (1056, 1024, 512, 320, 256, 128, 64))
    bn = _pick(n, (512, 256, 128))
    a_specs = [_resident((bm, k), lambda i, j: (i, 0)) for _ in a_list]
    w_specs = [pl.BlockSpec((k, bn), functools.partial(lambda i, j, p: (p, j), p=p))
               for p in range(n_a)]
    return pl.pallas_call(
        functools.partial(_residual_kernel, n_a=n_a, n_lat=n_lat, bm=bm),
        out_shape=jax.ShapeDtypeStruct((m, n), F32),
        grid=(m // bm, n // bn),
        in_specs=a_specs + w_specs + [pl.BlockSpec((bm, bn), lambda i, j: (i, j)),
                                      pl.BlockSpec((2, bn), lambda i, j: (0, j))],
        out_specs=pl.BlockSpec((bm, bn), lambda i, j: (i, j)),
        compiler_params=_params(("parallel", "arbitrary")),
        name="residual_matmul",
    )(*a_list, *([w] * n_a), x, gate)


def _segsum(x, bd):
    hi = x.astype(BF16)
    r1 = x - hi.astype(F32)
    mid = r1.astype(BF16)
    lo = (r1 - mid.astype(F32)).astype(BF16)
    parts = [_dot(hi[:, j:j + LANE], bd) + _dot(mid[:, j:j + LANE], bd) + _dot(lo[:, j:j + LANE], bd)
             for j in range(0, x.shape[1], LANE)]
    return jnp.concatenate(parts, axis=1)


def _streams_kernel(zc_ref, zp_ref, zn_ref, mu_ref, code_ref, kkw_ref, ka_ref, rk_ref, w0_ref,
                    a0_ref, wup_ref, aup_ref, gup_ref, bd_ref,
                    r_out, v_out, kk_out, lw0_out, lw1_out, b0_out, b1_out, kd0_out, kd1_out,
                    g_out, bonus_out, zs_ref, *, rb, n_lat_blocks, n_blocks, hw, ld, la, lg,
                    rw_cols):
    i = pl.program_id(0)
    rows = lax.broadcasted_iota(jnp.int32, (rb, 1), 0)
    quarter = rw_cols // 4
    n_tiles = zc_ref.shape[1] // LANE

    def token_shift(neighbour):
        for c in range(n_tiles):
            cols = slice(c * LANE, (c + 1) * LANE)
            z = zc_ref[:, cols]
            kinds = sorted({min(col // quarter, 3) for col in (c * LANE, (c + 1) * LANE - 1)})
            shifted = neighbour(kinds[-1], cols)
            for kind in reversed(kinds[:-1]):
                shifted = jnp.where(code_ref[:, cols] <= kind, neighbour(kind, cols), shifted)
            zs_ref[:, cols] = z + (shifted - z) * mu_ref[:, cols]

    prev_tok = lambda cols: pltpu.roll(zc_ref[:, cols], 1, 0)
    next_tok = lambda cols: pltpu.roll(zc_ref[:, cols], rb - 1, 0)

    @pl.when(i < n_lat_blocks)
    def _():
        def neighbour(kind, cols):
            if kind == 0:
                return jnp.where(rows % GRID_W == 0, 0.0, prev_tok(cols))
            if kind == 1:
                return jnp.where(rows % GRID_W == GRID_W - 1, 0.0, next_tok(cols))
            if kind == 2:
                above = zp_ref[:, cols] * (i > 0).astype(F32)
                return jnp.concatenate([above, zc_ref[:rb - GRID_W, cols]], axis=0)
            below = zn_ref[:, cols] * (i < n_lat_blocks - 1).astype(F32)
            return jnp.concatenate([zc_ref[GRID_W:, cols], below], axis=0)
        token_shift(neighbour)

    @pl.when(i >= n_lat_blocks)
    def _():
        def neighbour(kind, cols):
            if kind < 2:
                before = zp_ref[GRID_W - 1:GRID_W, cols] * (i > n_lat_blocks).astype(F32)
                return jnp.where(rows == 0, before, prev_tok(cols))
            after = zn_ref[0:1, cols] * (i < n_blocks - 1).astype(F32)
            return jnp.where(rows == rb - 1, after, next_tok(cols))
        token_shift(neighbour)

    zs = zs_ref[...]

    r = zs[:, 0:hw]
    k = zs[:, hw:2 * hw]
    v = zs[:, 2 * hw:3 * hw]
    wd = zs[:, 3 * hw:3 * hw + LANE]
    ad = zs[:, 3 * hw + ld:3 * hw + ld + LANE]
    gd = zs[:, 3 * hw + ld + la:3 * hw + ld + la + lg]
    bd = bd_ref[...]

    kk = k * kkw_ref[...]
    kk = kk / jnp.maximum(jnp.sqrt(_segsum(kk * kk, bd)), 1e-12)
    tw = jnp.tanh(wd).astype(BF16)
    adb = ad.astype(BF16)
    ka = ka_ref[...]
    kds = []
    for d, (lw_out, b_out, kd_out) in enumerate(((lw0_out, b0_out, kd0_out),
                                                 (lw1_out, b1_out, kd1_out))):
        w_logit = w0_ref[d:d + 1, :] + _dot(tw, wup_ref[d])
        lw_out[...] = -RW_DECAY_SCALE * jax.nn.sigmoid(w_logit)
        a = jax.nn.sigmoid(a0_ref[d:d + 1, :] + _dot(adb, aup_ref[d]))
        kd = k * (1.0 + (a - 1.0) * ka)
        kd_out[...] = kd.astype(kd_out.dtype)
        b_out[...] = (kk * a).astype(b_out.dtype)
        kds.append(kd)
    g_out[...] = _dot(jax.nn.sigmoid(gd).astype(BF16), gup_ref[...]).astype(g_out.dtype)
    kmean = (kds[0] + kds[1]) / 2.0
    bonus_out[...] = (_segsum(r * kmean * rk_ref[...], bd) * v).astype(bonus_out.dtype)
    r_out[...] = r.astype(r_out.dtype)
    v_out[...] = v.astype(v_out.dtype)
    kk_out[...] = kk.astype(kk_out.dtype)


def _rw_streams(z, p, n_lat):
    t = z.shape[0]
    hw, zw = p["hw"], p["zw"]
    lg = p["gup"].shape[0]
    rb = _pick(math.gcd(n_lat, t - n_lat), (256, 128, 64))
    n_blocks = t // rb
    n_lat_blocks = n_lat // rb
    halo = lambda f: pl.BlockSpec((GRID_W, zw), f)
    per = rb // GRID_W
    const2 = lambda shape: pl.BlockSpec(shape, lambda i: (0, 0))
    const3 = lambda shape: pl.BlockSpec(shape, lambda i: (0, 0, 0))
    out_spec = pl.BlockSpec((rb, hw), lambda i: (i, 0))
    out_dtypes = {"lw0": F32, "lw1": F32}
    names = ("r", "v", "kk", "lw0", "lw1", "b0", "b1", "kd0", "kd1", "g", "bonus")
    outs = pl.pallas_call(
        functools.partial(_streams_kernel, rb=rb, n_lat_blocks=n_lat_blocks, n_blocks=n_blocks,
                          hw=hw, ld=p["ld"], la=p["la"], lg=lg, rw_cols=p["rw_cols"]),
        out_shape=[jax.ShapeDtypeStruct((t, hw), out_dtypes.get(n, BF16)) for n in names],
        grid=(n_blocks,),
        in_specs=[pl.BlockSpec((rb, zw), lambda i: (i, 0)),
                  halo(lambda i: (jnp.maximum(i * per - 1, 0), 0)),
                  halo(lambda i: (jnp.minimum((i + 1) * per, t // GRID_W - 1), 0)),
                  const2((1, zw)), const2((1, zw)),
                  const2((1, hw)), const2((1, hw)), const2((1, hw)),
                  const2((2, hw)), const2((2, hw)),
                  const3((2, LANE, hw)), const3((2, LANE, hw)), const2((lg, hw)),
                  const2((LANE, LANE))],
        out_specs=[out_spec] * 11,
        scratch_shapes=[pltpu.VMEM((rb, zw), F32)],
        compiler_params=_params(("parallel",)),
        name="rw_streams",
    )(z, z, z, p["mu"], p["code"], p["k_k"], p["k_a"], p["r_k"], p["w0"], p["a0"],
      p["wup"], p["aup"], p["gup"], p["bd"])
    return dict(zip(names, outs))


def _wkv_kernel(lw_ref, kk_ref, b_ref, kd_ref, r_ref, v_ref, y_ref, st_ref, *, heads, tb, reverse):
    @pl.when(pl.program_id(1) == 0)
    def _():
        st_ref[...] = jnp.zeros(st_ref.shape, F32)

    n_chunks = tb // CHUNK
    t_idx = lax.broadcasted_iota(jnp.int32, (CHUNK, CHUNK), 0)
    s_idx = lax.broadcasted_iota(jnp.int32, (CHUNK, CHUNK), 1)
    strict = (s_idx > t_idx) if reverse else (s_idx < t_idx)
    incl = (s_idx >= t_idx) if reverse else (s_idx <= t_idx)
    eye = (s_idx == t_idx).astype(F32)

    bt = lax.broadcasted_iota(jnp.int32, (tb, tb), 0)
    bs = lax.broadcasted_iota(jnp.int32, (tb, tb), 1)
    before = (bs >= bt) if reverse else (bs <= bt)
    tri = jnp.where(((bt // CHUNK) == (bs // CHUNK)) & before, 1.0, 0.0).astype(BF16)
    lw = lw_ref[...]
    lw_hi = lw.astype(BF16)
    lw_lo = (lw - lw_hi.astype(F32)).astype(BF16)
    cum = _dot(tri, lw_hi) + _dot(tri, lw_lo)
    edge = 0 if reverse else CHUNK - 1
    tot = jnp.concatenate(
        [jnp.broadcast_to(cum[cc * CHUNK + edge:cc * CHUNK + edge + 1], (CHUNK, cum.shape[1]))
         for cc in range(n_chunks)], axis=0)
    kk, b, kd = (x[...].astype(F32) for x in (kk_ref, b_ref, kd_ref))
    e_neg = jnp.exp(-cum)
    e_end = jnp.exp(tot - cum)
    alpha = kk * jnp.exp(cum - lw)
    rho = r_ref[...].astype(F32) * jnp.exp(cum)
    ar_all = (alpha.astype(BF16), rho.astype(BF16))
    bk_all = ((b * e_neg).astype(BF16), (kd * e_neg).astype(BF16))
    beta_e = (b * e_end).astype(BF16)
    kappa_e = (kd * e_end).astype(BF16)
    g_end = jnp.exp(tot)
    v_all = v_ref[...]

    pairs = [(cc, h) for cc in range(n_chunks) for h in range(heads)]
    rs = lambda cc: slice(cc * CHUNK, (cc + 1) * CHUNK)
    ls = lambda h: slice(h * RW_HEAD, (h + 1) * RW_HEAD)
    cut = lambda x, cc, h: x[rs(cc), ls(h)]
    stack = lambda xs, cc, h: jnp.concatenate([cut(x, cc, h) for x in xs], axis=0)
    m = [_dot_nt(stack(ar_all, cc, h), stack(bk_all, cc, h)) for cc, h in pairs]
    pw = [jnp.where(strict, -x[:CHUNK, :CHUNK], 0.0) for x in m]
    m_akrk = [jnp.concatenate([jnp.where(strict, x[:CHUNK, CHUNK:], 0.0),
                               jnp.where(incl, x[CHUNK:, CHUNK:], 0.0)], axis=0).astype(BF16)
              for x in m]
    m_rb = [jnp.where(incl, x[CHUNK:, :CHUNK], 0.0).astype(BF16) for x in m]
    t_inv = [eye + x for x in pw]
    for _ in range(int(math.log2(CHUNK)) - 1):
        pwb = [x.astype(BF16) for x in pw]
        pw = [_dot(x, x) for x in pwb]
        t_inv = [t + _dot(t.astype(BF16), x.astype(BF16)) for t, x in zip(t_inv, pw)]
    mv = [_dot(x, cut(v_all, cc, h)) for x, (cc, h) in zip(m_akrk, pairs)]
    w12 = [_dot(t.astype(BF16),
                jnp.concatenate([x[:CHUNK].astype(BF16), cut(ar_all[0], cc, h)], axis=1)
                ).astype(BF16)
           for t, x, (cc, h) in zip(t_inv, mv, pairs)]
    mw = [_dot(x, w) for x, w in zip(m_rb, w12)]
    bw = [_dot_tn(cut(beta_e, cc, h), w) for w, (cc, h) in zip(w12, pairs)]
    kv = [_dot_tn(cut(kappa_e, cc, h), cut(v_all, cc, h)) for cc, h in pairs]
    pre = {}
    for i, (cc, h) in enumerate(pairs):
        y1 = mv[i][CHUNK:] - mw[i][:, :RW_HEAD]
        r2 = cut(rho, cc, h) - mw[i][:, RW_HEAD:]
        g_m = kv[i] - bw[i][:, :RW_HEAD]
        p_m = eye * g_end[cc * CHUNK:cc * CHUNK + 1, ls(h)] - bw[i][:, RW_HEAD:]
        pre[cc, h] = (y1, jnp.concatenate([r2, p_m], axis=0).astype(BF16), g_m)

    st = [st_ref[h] for h in range(heads)]
    ys = {}
    for cc in (range(n_chunks - 1, -1, -1) if reverse else range(n_chunks)):
        prod = [_dot(pre[cc, h][1], st[h].astype(BF16)) for h in range(heads)]
        for h in range(heads):
            ys[cc, h] = pre[cc, h][0] + prod[h][:CHUNK]
            st[h] = prod[h][CHUNK:] + pre[cc, h][2]
    for h in range(heads):
        st_ref[h] = st[h]
    for cc in range(n_chunks):
        y_ref[rs(cc), :] = jnp.concatenate([ys[cc, h] for h in range(heads)], axis=1)


def _wkv(st, d, n_lat, reverse):
    lw, b, kd = st[f"lw{d}"], st[f"b{d}"], st[f"kd{d}"]
    t, hw = lw.shape
    heads = _pick(hw // RW_HEAD, (16, 8, 4, 2))
    tb = _pick(math.gcd(n_lat, t - n_lat), (256, 128, 64))
    nb = t // tb
    n_lat_blocks = n_lat // tb
    if reverse:
        blk = lambda h, c: (nb - 1 - c, h)
    else:
        blk = lambda h, c: ((c + n_lat_blocks) % nb, h)
    spec = pl.BlockSpec((tb, heads * RW_HEAD), blk)
    return pl.pallas_call(
        functools.partial(_wkv_kernel, heads=heads, tb=tb, reverse=reverse),
        out_shape=jax.ShapeDtypeStruct((t, hw), F32),
        grid=(hw // (heads * RW_HEAD), nb),
        in_specs=[spec] * 6,
        out_specs=spec,
        scratch_shapes=[pltpu.VMEM((heads, RW_HEAD, RW_HEAD), F32)],
        compiler_params=_params(("parallel", "arbitrary")),
        name="wkv_bwd" if reverse else "wkv_fwd",
    )(lw, st["kk"], b, kd, st["r"], st["v"])


def _rw_out_kernel(yf_ref, yb_ref, bonus_ref, g_ref, lnw_ref, lnb_ref, bd_ref, o_ref):
    y = yf_ref[...] + yb_ref[...]
    bd = bd_ref[...]
    mean = _segsum(y, bd) / RW_HEAD
    yc = y - mean
    var = _segsum(yc * yc, bd) / RW_HEAD
    o = yc * lax.rsqrt(var + RW_GN_EPS) * lnw_ref[...] + lnb_ref[...] + bonus_ref[...].astype(F32)
    o_ref[...] = (o * g_ref[...].astype(F32)).astype(o_ref.dtype)


def _rw_output(y_f, y_b, st, p):
    t, hw = y_f.shape
    bm = _pick(t, (256, 128, 64))
    row = pl.BlockSpec((bm, hw), lambda i: (i, 0))
    vec = pl.BlockSpec((1, hw), lambda i: (0, 0))
    return pl.pallas_call(
        _rw_out_kernel,
        out_shape=jax.ShapeDtypeStruct((t, hw), BF16),
        grid=(t // bm,),
        in_specs=[row, row, row, row, vec, vec, pl.BlockSpec((LANE, LANE), lambda i: (0, 0))],
        out_specs=row,
        compiler_params=_params(("parallel",)),
        name="rw_output",
    )(y_f, y_b, st["bonus"], st["g"], p["ln_w"], p["ln_b"], p["bd"])


def _mla_prep_kernel(z_ref, qn_ref, qup_ref, kvn_ref, kvup_ref, qg_ref, kg_ref, c_ref, s1_ref,
                     s2_ref, q_out, k_out, v_out, *, off, q_rank, kv_rank, heads, q_scale):
    z = z_ref[...]

    def rms(x, g):
        return x * lax.rsqrt(jnp.mean(x * x, axis=-1, keepdims=True) + NORM_EPS) * g

    q = _dot(rms(z[:, off:off + q_rank], qn_ref[...]).astype(BF16), qup_ref[...])
    kv_lat = z[:, off + q_rank:off + q_rank + kv_rank]
    kv = _dot(rms(kv_lat, kvn_ref[...]).astype(BF16), kvup_ref[...])
    k_rope = z[:, off + q_rank + kv_rank:off + q_rank + kv_rank + MLA_ROPE]
    k_rope = jnp.concatenate([k_rope, jnp.zeros((z.shape[0], LANE - MLA_ROPE), F32)], axis=1)
    cos, s1, s2 = c_ref[...], s1_ref[...], s2_ref[...]
    sumsq = lambda x: jnp.sum(x * x, axis=-1, keepdims=True)
    inv_rms = lambda ss: lax.rsqrt(ss / MLA_QK + NORM_EPS)

    def rotary(x):
        return (x * cos + pltpu.roll(x, LANE - ROPE_FREQ, 1) * s1
                + pltpu.roll(x, ROPE_FREQ, 1) * s2)

    qg_n, qg_r = qg_ref[:, :MLA_NOPE], qg_ref[:, MLA_NOPE:]
    kg_n, kg_r = kg_ref[:, :MLA_NOPE], kg_ref[:, MLA_NOPE:]
    kr_ss = sumsq(k_rope)
    kr_rot = rotary(k_rope * kg_r)
    for h in range(heads):
        nope = slice(h * MLA_HEAD_PAD, h * MLA_HEAD_PAD + MLA_NOPE)
        rope = slice(h * MLA_HEAD_PAD + MLA_NOPE, (h + 1) * MLA_HEAD_PAD)
        qn, qr = q[:, nope], q[:, rope]
        q_inv = inv_rms(sumsq(qn) + sumsq(qr))
        q_out[:, nope] = (qn * q_inv * qg_n * q_scale).astype(BF16)
        q_out[:, rope] = (rotary(qr * q_inv * qg_r) * q_scale).astype(BF16)
        kn = kv[:, h * MLA_NOPE:(h + 1) * MLA_NOPE]
        k_inv = inv_rms(sumsq(kn) + kr_ss)
        k_out[:, nope] = (kn * k_inv * kg_n).astype(BF16)
        k_out[:, rope] = (kr_rot * k_inv).astype(BF16)
    v_out[...] = kv[:, heads * MLA_NOPE:].astype(BF16)


def _mla_prep(z, rw_cols, p, rope):
    t, n = z.shape
    heads = p["heads"]
    q_rank, kv_rank = p["q_up"].shape[0], p["kv_up"].shape[0]
    zw = min(w for w in range(LANE, n + 1, LANE) if n % w == 0 and n - w <= rw_cols)
    bm = _pick(t, (256, 128, 64))
    qw = heads * MLA_HEAD_PAD
    row = lambda w: pl.BlockSpec((bm, w), lambda i: (i, 0))
    const = lambda a: pl.BlockSpec(a.shape, lambda i: (0, 0))
    args = (z, p["q_norm"], p["q_up"], p["kv_norm"], p["kv_up"], p["q_gain"], p["k_gain"])
    return pl.pallas_call(
        functools.partial(_mla_prep_kernel, off=rw_cols - (n - zw), q_rank=q_rank, kv_rank=kv_rank,
                          heads=heads, q_scale=ATTN_Q_SCALE),
        out_shape=[jax.ShapeDtypeStruct((t, qw), BF16), jax.ShapeDtypeStruct((t, qw), BF16),
                   jax.ShapeDtypeStruct((t, heads * MLA_V_DIM), BF16)],
        grid=(t // bm,),
        in_specs=([pl.BlockSpec((bm, zw), lambda i: (i, n // zw - 1))]
                  + [const(a) for a in args[1:]] + [row(LANE)] * 3),
        out_specs=[row(qw), row(qw), row(heads * MLA_V_DIM)],
        compiler_params=_params(("parallel",)),
        name="mla_prep",
    )(*args, *rope)


def _attn_kernel(q_ref, k_ref, vt_ref, o_ref, *, n_lat, bq, hps, with_ctx):
    def attend(key0, n_keys):
        bk = _pick(n_keys, (2816, 1408, 768, 512, 256, 128))
        qcols = lambda c: slice(c * MLA_HEAD_PAD, (c + 1) * MLA_HEAD_PAD)
        q = [q_ref[:, qcols(c)] for c in range(hps)]
        keys = lambda j: slice(key0 + j * bk, key0 + (j + 1) * bk)
        scores = lambda j: [_dot_nt(k_ref[keys(j), qcols(c)], q[c]) for c in range(hps)]
        s_next = scores(0)
        m = [jnp.full((1, bq), -jnp.inf, F32)] * hps
        acc = [jnp.zeros((V_ROWS, bq), F32)] * hps
        for j in range(n_keys // bk):
            s = s_next
            if (j + 1) * bk < n_keys:
                s_next = scores(j + 1)
            for c in range(hps):
                m_new = jnp.maximum(m[c], jnp.max(s[c], axis=0, keepdims=True))
                p = jnp.exp2(s[c] - m_new).astype(BF16)
                acc[c] = (jnp.exp2(m[c] - m_new) * acc[c]
                          + _dot(vt_ref[c * V_ROWS:(c + 1) * V_ROWS, keys(j)], p))
                m[c] = m_new
        for c in range(hps):
            o_ref[:, c * MLA_V_DIM:(c + 1) * MLA_V_DIM] = (
                acc[c][:MLA_V_DIM] / acc[c][MLA_V_DIM:MLA_V_DIM + 1]).T.astype(o_ref.dtype)

    n_all = k_ref.shape[0]
    if not with_ctx:
        attend(0, n_all)
        return
    is_latent = pl.program_id(1) < n_lat // bq

    @pl.when(is_latent)
    def _():
        attend(0, n_all)

    @pl.when(jnp.logical_not(is_latent))
    def _():
        attend(n_lat, n_all - n_lat)


def _attention(q, k, vt, heads, n_lat, with_ctx):
    t = k.shape[0]
    n_q = t if with_ctx else n_lat
    bq = _pick(math.gcd(n_lat, t - n_lat), (256, 128, 64))
    hps = 4
    return pl.pallas_call(
        functools.partial(_attn_kernel, n_lat=n_lat, bq=bq, hps=hps, with_ctx=with_ctx),
        out_shape=jax.ShapeDtypeStruct((n_q, heads * MLA_V_DIM), BF16),
        grid=(heads // hps, n_q // bq),
        in_specs=[pl.BlockSpec((bq, hps * MLA_HEAD_PAD), lambda h, i: (i, h)),
                  _resident((t, hps * MLA_HEAD_PAD), lambda h, i: (0, h)),
                  _resident((hps * V_ROWS, t), lambda h, i: (h, 0))],
        out_specs=pl.BlockSpec((bq, hps * MLA_V_DIM), lambda h, i: (i, h)),
        compiler_params=_params(("parallel", "parallel")),
        name="attention",
    )(q, k, vt)


def _pad_cols(w, width):
    return jnp.pad(w, ((0, 0), (0, width - w.shape[1])))


def _rope_tables(n_lat, n_ctx):
    t = jnp.arange(n_lat, dtype=jnp.int32)
    pos = jnp.stack([t // GRID_W, t % GRID_W], axis=-1).astype(F32)
    inv_freq = ROPE_THETA ** (-jnp.arange(ROPE_FREQ, dtype=F32) / ROPE_FREQ)
    ang = pos[:, :, None] * inv_freq
    cos, sin = jnp.cos(ang), jnp.sin(ang)
    zero = jnp.zeros_like(sin)
    lay = lambda first, second: jnp.stack([first, second], axis=2).reshape(n_lat, MLA_ROPE)
    tabs = []
    for rope_part, fill in ((lay(cos, cos), 1.0), (lay(-sin, zero), 0.0), (lay(zero, sin), 0.0)):
        tab = jnp.concatenate([rope_part, jnp.full((n_lat, LANE - MLA_ROPE), fill, F32)], axis=1)
        tabs.append(jnp.concatenate([tab, jnp.full((n_ctx, LANE), fill, F32)], axis=0))
    return tuple(tabs)


def _layer_params(l, rw_mu, rw_w0, rw_w_up, rw_a0, rw_a_up, rw_g_up, rw_k_k, rw_k_a, rw_r_k,
                  rw_ln_w, rw_ln_b, mla_q_norm, mla_q_up, mla_kv_norm, mla_kv_up, mla_q_gain,
                  mla_k_gain):
    hw = rw_k_k.shape[1]
    ld, la, lg = rw_w_up.shape[2], rw_a_up.shape[2], rw_g_up.shape[1]
    q_rank, kv_rank = mla_q_up.shape[1], mla_kv_up.shape[1]
    heads = mla_kv_up.shape[2] // (MLA_NOPE + MLA_V_DIM)
    rw_cols = 3 * hw + ld + la + lg
    zw = -(-rw_cols // LANE) * LANE
    assert ld <= LANE and la <= LANE and rw_cols % 4 == 0 and 3 * hw + ld + LANE <= zw
    code = np.minimum(np.arange(zw, dtype=np.int32) // (rw_cols // 4), 3)[None, :]
    bd = (np.arange(LANE)[:, None] // RW_HEAD == np.arange(LANE)[None, :] // RW_HEAD)
    pad_rows = lambda x: jnp.pad(x, ((0, 0), (0, LANE - x.shape[1]), (0, 0)))
    return {
        "hw": hw, "heads": heads, "rw_cols": rw_cols, "zw": zw, "ld": ld, "la": la,
        "mu": _pad_cols(rw_mu[l][None, :], zw),
        "code": jnp.asarray(code),
        "k_k": rw_k_k[l][None, :], "k_a": rw_k_a[l][None, :], "r_k": rw_r_k[l].reshape(1, hw),
        "w0": rw_w0[l], "a0": rw_a0[l],
        "wup": pad_rows(rw_w_up[l]).astype(BF16), "aup": pad_rows(rw_a_up[l]).astype(BF16),
        "gup": rw_g_up[l].astype(BF16),
        "bd": jnp.asarray(bd, BF16),
        "ln_w": rw_ln_w[l][None, :], "ln_b": rw_ln_b[l][None, :],
        "q_norm": mla_q_norm[l][None, :], "kv_norm": mla_kv_norm[l][None, :],
        "q_up": jnp.pad(mla_q_up[l].reshape(q_rank, heads, MLA_QK),
                        ((0, 0), (0, 0), (0, MLA_HEAD_PAD - MLA_QK))
                        ).reshape(q_rank, heads * MLA_HEAD_PAD).astype(BF16),
        "kv_up": mla_kv_up[l].reshape(kv_rank, heads, 2, MLA_NOPE).transpose(0, 2, 1, 3)
                             .reshape(kv_rank, 2 * heads * MLA_NOPE).astype(BF16),
        "q_gain": _pad_cols(mla_q_gain[l][None, :], MLA_HEAD_PAD),
        "k_gain": _pad_cols(mla_k_gain[l][None, :], MLA_HEAD_PAD),
    }


def _ffn_half_step(xt, mod, g, ffn_w_in, ffn_w_out, lead, n_lat):
    h = _norm_mod(xt, g, mod[:, 0], mod[:, 1], n_lat)
    act, w_out_bf16 = _swiglu_matmul(h, ffn_w_in, lead, ffn_w_out)
    return _residual_matmul([act], w_out_bf16, xt, FFN_RES * mod[:, 2], n_lat)


def kernel(x, c, ctx, c_ctx, w_mod, b_mod, norm_g, ffn_w_in, ffn_w_out, w_in, w_out, rw_mu, rw_w0,
           rw_w_up, rw_a0, rw_a_up, rw_g_up, rw_k_k, rw_k_a, rw_r_k, rw_ln_w, rw_ln_b, mla_q_norm,
           mla_q_up, mla_kv_norm, mla_kv_up, mla_q_gain, mla_k_gain):
    batch, n_lat, d = x.shape
    n_ctx = ctx.shape[1]
    depth = w_mod.shape[0]
    assert batch == 1 and c.shape[0] == 1 and ctx.shape[0] == 1
    assert n_lat % GRID_W == 0 and n_ctx % GRID_W == 0

    xt = jnp.concatenate([x[0], ctx[0]], axis=0)
    cc = jnp.zeros((8, d), F32).at[0].set(c[0]).at[1].set(c_ctx)
    rope = _rope_tables(n_lat, n_ctx)

    for l in range(depth):
        last = l == depth - 1
        p = _layer_params(l, rw_mu, rw_w0, rw_w_up, rw_a0, rw_a_up, rw_g_up, rw_k_k, rw_k_a,
                          rw_r_k, rw_ln_w, rw_ln_b, mla_q_norm, mla_q_up, mla_kv_norm, mla_kv_up,
                          mla_q_gain, mla_k_gain)
        mod = _modulation(cc, w_mod, b_mod, l)[:2].reshape(2, N_MOD, d)

        xt = _ffn_half_step(xt, mod[:, 0:3], norm_g[l, 0], ffn_w_in, ffn_w_out, (l, 0), n_lat)

        hz = _norm_mod(xt, norm_g[l, 1], mod[:, 3], mod[:, 4], n_lat)
        z, w_out_bf16 = _matmul(hz, w_in, (l,), w_out, (l,))

        st = _rw_streams(z, p, n_lat)
        y_f = _wkv(st, 0, n_lat, reverse=False)
        y_b = _wkv(st, 1, n_lat, reverse=True)
        o_rw = _rw_output(y_f, y_b, st, p)

        q, k, v = _mla_prep(z, p["rw_cols"], p, rope)
        vt = jnp.concatenate([v.T.reshape(p["heads"], MLA_V_DIM, -1),
                              jnp.ones((p["heads"], V_ROWS - MLA_V_DIM, v.shape[0]), BF16)], axis=1)
        o_mla = _attention(q, k, vt.reshape(p["heads"] * V_ROWS, -1), p["heads"], n_lat,
                           with_ctx=not last)

        xt = _residual_matmul([o_rw, o_mla], w_out_bf16, xt, mod[:, 5], n_lat,
                              rows=n_lat if last else None)

        xt = _ffn_half_step(xt, mod[:, 6:9], norm_g[l, 2], ffn_w_in, ffn_w_out, (l, 1), n_lat)
    return xt[:n_lat][None]
```

```python
import functools
import math

import numpy as np
import jax
import jax.numpy as jnp
from jax import lax
from jax.experimental import pallas as pl
from jax.experimental.pallas import tpu as pltpu

F32 = jnp.float32
BF16 = jnp.bfloat16

GRID_W = 64
NORM_EPS = 1e-6
FFN_RES = 0.5
RW_HEAD = 64
RW_GN_EPS = 64e-5
RW_DECAY_SCALE = math.exp(-0.5)
MLA_V_DIM = 128
MLA_NOPE = 128
MLA_ROPE = 64
MLA_QK = MLA_NOPE + MLA_ROPE
MLA_HEAD_PAD = 256
ROPE_FREQ = MLA_ROPE // 4
ROPE_THETA = 10000.0
N_MOD = 9
LANE = 128
CHUNK = 64
ATTN_Q_SCALE = math.log2(math.e) / math.sqrt(MLA_QK)
V_ROWS = MLA_V_DIM + 16
VMEM_LIMIT = 60 * 1024 * 1024


def _pick(n, candidates):
    for c in candidates:
        if n % c == 0:
            return c
    raise ValueError(f"no block size for {n} among {candidates}")


def _params(sem):
    return pltpu.CompilerParams(dimension_semantics=sem, vmem_limit_bytes=VMEM_LIMIT)


def _dot(a, b):
    return jnp.dot(a, b, preferred_element_type=F32)


def _dot_nt(a, b):
    return lax.dot_general(a, b, (((1,), (1,)), ((), ())), preferred_element_type=F32)


def _dot_tn(a, b):
    return lax.dot_general(a, b, (((0,), (0,)), ((), ())), preferred_element_type=F32)


def _wspec(lead, block, index_map):
    return pl.BlockSpec((None,) * len(lead) + block, lambda *g: lead + index_map(*g))


def _row_select(row0, n_rows, n_lat, ref):
    rows = row0 + lax.broadcasted_iota(jnp.int32, (n_rows, 1), 0)
    return jnp.where(rows < n_lat, ref[0:1, :], ref[1:2, :])


def _mod_kernel(c_ref, w_ref, b_ref, o_ref):
    c = c_ref[...]
    a = (c * jax.nn.sigmoid(c)).astype(BF16)
    o_ref[...] = _dot(a, w_ref[...].astype(BF16)) + b_ref[...]


def _modulation(cc, w_mod, b_mod, l):
    _, d, n = w_mod.shape
    bn = _pick(n, (512, 256, 128))
    return pl.pallas_call(
        _mod_kernel,
        out_shape=jax.ShapeDtypeStruct((8, n), F32),
        grid=(n // bn,),
        in_specs=[pl.BlockSpec((8, d), lambda j: (0, 0)),
                  _wspec((l,), (d, bn), lambda j: (0, j)),
                  _wspec((l,), (1, bn), lambda j: (0, j))],
        out_specs=pl.BlockSpec((8, bn), lambda j: (0, j)),
        compiler_params=_params(("parallel",)),
        name="modulation",
    )(cc, w_mod, b_mod.reshape(b_mod.shape[0], 1, n))


def _norm_mod_kernel(x_ref, g_ref, sh_ref, sc_ref, o_ref, *, n_lat, bm):
    x = x_ref[...]
    y = x * lax.rsqrt(jnp.mean(x * x, axis=-1, keepdims=True) + NORM_EPS) * g_ref[...]
    kind = pl.ds((pl.program_id(0) * bm >= n_lat).astype(jnp.int32), 1)
    o_ref[...] = (y * (1.0 + sc_ref[kind, :]) + sh_ref[kind, :]).astype(o_ref.dtype)


def _norm_mod(x, g, shift, scale, n_lat):
    m, d = x.shape
    bm = _pick(math.gcd(n_lat, m - n_lat) if m > n_lat else m, (256, 128, 64, 32, 16))
    return pl.pallas_call(
        functools.partial(_norm_mod_kernel, n_lat=n_lat, bm=bm),
        out_shape=jax.ShapeDtypeStruct((m, d), BF16),
        grid=(m // bm,),
        in_specs=[pl.BlockSpec((bm, d), lambda i: (i, 0)),
                  pl.BlockSpec((1, d), lambda i: (0, 0)),
                  pl.BlockSpec((2, d), lambda i: (0, 0)),
                  pl.BlockSpec((2, d), lambda i: (0, 0))],
        out_specs=pl.BlockSpec((bm, d), lambda i: (i, 0)),
        compiler_params=_params(("parallel",)),
        name="norm_mod",
    )(x, g.reshape(1, d), shift, scale)


def _resident(block_shape, index_map):
    return pl.BlockSpec(block_shape, index_map, pipeline_mode=pl.Buffered(1))


def _side_cast(side, side_lead, steps, nb):
    rows, cols = side.shape[-2:]
    sr = min(r for r in range(16, rows + 1, 16) if rows % r == 0 and rows // r <= steps)
    slab = lambda i, j: (jnp.minimum(i * nb + j, rows // sr - 1), 0)
    return (_wspec(side_lead, (sr, cols), slab), pl.BlockSpec((sr, cols), slab),
            jax.ShapeDtypeStruct((rows, cols), BF16))


def _mm_kernel(a_ref, w_ref, side_ref, o_ref, side_out):
    o_ref[...] = _dot(a_ref[...], w_ref[...].astype(BF16)).astype(o_ref.dtype)
    side_out[...] = side_ref[...].astype(BF16)


def _matmul(a, w, lead, side, side_lead):
    m, k = a.shape
    n = w.shape[-1]
    bm = _pick(m, (1056, 1024, 512, 320, 256, 128, 64))
    bn = _pick(n, (512, 256, 128))
    side_in, side_out, side_shape = _side_cast(side, side_lead, (m // bm) * (n // bn), n // bn)
    return pl.pallas_call(
        _mm_kernel,
        out_shape=[jax.ShapeDtypeStruct((m, n), F32), side_shape],
        grid=(m // bm, n // bn),
        in_specs=[pl.BlockSpec((bm, k), lambda i, j: (i, 0)),
                  _wspec(lead, (k, bn), lambda i, j: (0, j)), side_in],
        out_specs=[pl.BlockSpec((bm, bn), lambda i, j: (i, j)), side_out],
        compiler_params=_params(("parallel", "arbitrary")),
        name="matmul",
    )(a, w, side)


def _swiglu_kernel(a_ref, wg_ref, wu_ref, side_ref, o_ref, side_out):
    a = a_ref[...]
    gate = _dot(a, wg_ref[...].astype(BF16))
    up = _dot(a, wu_ref[...].astype(BF16))
    o_ref[...] = (gate * jax.nn.sigmoid(gate) * up).astype(o_ref.dtype)
    side_out[...] = side_ref[...].astype(BF16)


def _swiglu_matmul(a, w, lead, side):
    m, k = a.shape
    f = w.shape[-1] // 2
    bm = _pick(m, (1056, 1024, 512, 320, 256, 128, 64))
    bn = _pick(f, (512, 256, 128))
    nb = f // bn
    side_in, side_out, side_shape = _side_cast(side, lead, (m // bm) * nb, nb)
    return pl.pallas_call(
        _swiglu_kernel,
        out_shape=[jax.ShapeDtypeStruct((m, f), BF16), side_shape],
        grid=(m // bm, nb),
        in_specs=[_resident((bm, k), lambda i, j: (i, 0)),
                  _wspec(lead, (k, bn), lambda i, j: (0, j)),
                  _wspec(lead, (k, bn), lambda i, j: (0, j + nb)), side_in],
        out_specs=[pl.BlockSpec((bm, bn), lambda i, j: (i, j)), side_out],
        compiler_params=_params(("parallel", "arbitrary")),
        name="swiglu_matmul",
    )(a, w, w, side)


def _residual_kernel(*refs, n_a, n_lat, bm):
    a_refs = refs[:n_a]
    w_refs = refs[n_a:2 * n_a]
    x_ref, g_ref, o_ref = refs[2 * n_a:]
    acc = _dot(a_refs[0][...], w_refs[0][...])
    for a_ref, w_ref in zip(a_refs[1:], w_refs[1:]):
        acc = acc + _dot(a_ref[...], w_ref[...])
    gate = _row_select(pl.program_id(0) * bm, bm, n_lat, g_ref)
    o_ref[...] = x_ref[...] + gate * acc


def _residual_matmul(a_list, w, x, gate, n_lat, rows=None):
    m, n = (rows or x.shape[0]), x.shape[1]
    n_a = len(a_list)
    k = a_list[0].shape[1]
    assert all(a.shape[1] == k for a in a_list) and w.shape == (n_a * k, n) and w.dtype == BF16
    bm = _pick(m, (1056, 1024, 512, 320, 256, 128, 64))
    a_vmem = 2 * n_a * bm * k * a_list[0].dtype.itemsize
    bn = _pick(n, (512, 256, 128) if a_vmem <= VMEM_LIMIT // 3 else (256, 128))
    a_specs = [pl.BlockSpec((bm, k), lambda i, j: (i, 0)) for _ in a_list]
    w_specs = [pl.BlockSpec((k, bn), functools.partial(lambda i, j, p: (p, j), p=p))
               for p in range(n_a)]
    return pl.pallas_call(
        functools.partial(_residual_kernel, n_a=n_a, n_lat=n_lat, bm=bm),
        out_shape=jax.ShapeDtypeStruct((m, n), F32),
        grid=(m // bm, n // bn),
        in_specs=a_specs + w_specs + [pl.BlockSpec((bm, bn), lambda i, j: (i, j)),
                                      pl.BlockSpec((2, bn), lambda i, j: (0, j))],
        out_specs=pl.BlockSpec((bm, bn), lambda i, j: (i, j)),
        compiler_params=_params(("parallel", "arbitrary")),
        name="residual_matmul",
    )(*a_list, *([w] * n_a), x, gate)


def _segsum(x, bd):
    hi = x.astype(BF16)
    r1 = x - hi.astype(F32)
    mid = r1.astype(BF16)
    lo = (r1 - mid.astype(F32)).astype(BF16)
    parts = [_dot(hi[:, j:j + LANE], bd) + _dot(mid[:, j:j + LANE], bd) + _dot(lo[:, j:j + LANE], bd)
             for j in range(0, x.shape[1], LANE)]
    return jnp.concatenate(parts, axis=1)


def _streams_kernel(zc_ref, zp_ref, zn_ref, mu_ref, code_ref, kkw_ref, ka_ref, rk_ref, w0_ref,
                    a0_ref, wup_ref, aup_ref, gup_ref, bd_ref,
                    r_out, v_out, kk_out, lw0_out, lw1_out, b0_out, b1_out, kd0_out, kd1_out,
                    g_out, bonus_out, zs_ref, *, rb, n_lat_blocks, n_blocks, hw, ld, la, lg,
                    rw_cols):
    i = pl.program_id(0)
    rows = lax.broadcasted_iota(jnp.int32, (rb, 1), 0)
    quarter = rw_cols // 4
    n_tiles = zc_ref.shape[1] // LANE

    def token_shift(neighbour):
        for c in range(n_tiles):
            cols = slice(c * LANE, (c + 1) * LANE)
            z = zc_ref[:, cols]
            kinds = sorted({min(col // quarter, 3) for col in (c * LANE, (c + 1) * LANE - 1)})
            shifted = neighbour(kinds[-1], cols)
            for kind in reversed(kinds[:-1]):
                shifted = jnp.where(code_ref[:, cols] <= kind, neighbour(kind, cols), shifted)
            zs_ref[:, cols] = z + (shifted - z) * mu_ref[:, cols]

    prev_tok = lambda cols: pltpu.roll(zc_ref[:, cols], 1, 0)
    next_tok = lambda cols: pltpu.roll(zc_ref[:, cols], rb - 1, 0)

    @pl.when(i < n_lat_blocks)
    def _():
        def neighbour(kind, cols):
            if kind == 0:
                return jnp.where(rows % GRID_W == 0, 0.0, prev_tok(cols))
            if kind == 1:
                return jnp.where(rows % GRID_W == GRID_W - 1, 0.0, next_tok(cols))
            if kind == 2:
                above = zp_ref[:, cols] * (i > 0).astype(F32)
                return jnp.concatenate([above, zc_ref[:rb - GRID_W, cols]], axis=0)
            below = zn_ref[:, cols] * (i < n_lat_blocks - 1).astype(F32)
            return jnp.concatenate([zc_ref[GRID_W:, cols], below], axis=0)
        token_shift(neighbour)

    @pl.when(i >= n_lat_blocks)
    def _():
        def neighbour(kind, cols):
            if kind < 2:
                before = zp_ref[GRID_W - 1:GRID_W, cols] * (i > n_lat_blocks).astype(F32)
                return jnp.where(rows == 0, before, prev_tok(cols))
            after = zn_ref[0:1, cols] * (i < n_blocks - 1).astype(F32)
            return jnp.where(rows == rb - 1, after, next_tok(cols))
        token_shift(neighbour)

    zs = zs_ref[...]

    r = zs[:, 0:hw]
    k = zs[:, hw:2 * hw]
    v = zs[:, 2 * hw:3 * hw]
    wd = zs[:, 3 * hw:3 * hw + LANE]
    ad = zs[:, 3 * hw + ld:3 * hw + ld + LANE]
    gd = zs[:, 3 * hw + ld + la:3 * hw + ld + la + lg]
    bd = bd_ref[...]

    kk = k * kkw_ref[...]
    kk = kk / jnp.maximum(jnp.sqrt(_segsum(kk * kk, bd)), 1e-12)
    tw = jnp.tanh(wd).astype(BF16)
    adb = ad.astype(BF16)
    ka = ka_ref[...]
    kds = []
    for d, (lw_out, b_out, kd_out) in enumerate(((lw0_out, b0_out, kd0_out),
                                                 (lw1_out, b1_out, kd1_out))):
        w_logit = w0_ref[d:d + 1, :] + _dot(tw, wup_ref[d])
        lw_out[...] = -RW_DECAY_SCALE * jax.nn.sigmoid(w_logit)
        a = jax.nn.sigmoid(a0_ref[d:d + 1, :] + _dot(adb, aup_ref[d]))
        kd = k * (1.0 + (a - 1.0) * ka)
        kd_out[...] = kd.astype(kd_out.dtype)
        b_out[...] = (kk * a).astype(b_out.dtype)
        kds.append(kd)
    g_out[...] = _dot(jax.nn.sigmoid(gd).astype(BF16), gup_ref[...]).astype(g_out.dtype)
    kmean = (kds[0] + kds[1]) / 2.0
    bonus_out[...] = (_segsum(r * kmean * rk_ref[...], bd) * v).astype(bonus_out.dtype)
    r_out[...] = r.astype(r_out.dtype)
    v_out[...] = v.astype(v_out.dtype)
    kk_out[...] = kk.astype(kk_out.dtype)


def _rw_streams(z, p, n_lat):
    t = z.shape[0]
    hw, zw = p["hw"], p["zw"]
    lg = p["gup"].shape[0]
    rb = _pick(math.gcd(n_lat, t - n_lat), (256, 128, 64))
    n_blocks = t // rb
    n_lat_blocks = n_lat // rb
    halo = lambda f: pl.BlockSpec((GRID_W, zw), f)
    per = rb // GRID_W
    const2 = lambda shape: pl.BlockSpec(shape, lambda i: (0, 0))
    const3 = lambda shape: pl.BlockSpec(shape, lambda i: (0, 0, 0))
    out_spec = pl.BlockSpec((rb, hw), lambda i: (i, 0))
    out_dtypes = {"lw0": F32, "lw1": F32}
    names = ("r", "v", "kk", "lw0", "lw1", "b0", "b1", "kd0", "kd1", "g", "bonus")
    outs = pl.pallas_call(
        functools.partial(_streams_kernel, rb=rb, n_lat_blocks=n_lat_blocks, n_blocks=n_blocks,
                          hw=hw, ld=p["ld"], la=p["la"], lg=lg, rw_cols=p["rw_cols"]),
        out_shape=[jax.ShapeDtypeStruct((t, hw), out_dtypes.get(n, BF16)) for n in names],
        grid=(n_blocks,),
        in_specs=[pl.BlockSpec((rb, zw), lambda i: (i, 0)),
                  halo(lambda i: (jnp.maximum(i * per - 1, 0), 0)),
                  halo(lambda i: (jnp.minimum((i + 1) * per, t // GRID_W - 1), 0)),
                  const2((1, zw)), const2((1, zw)),
                  const2((1, hw)), const2((1, hw)), const2((1, hw)),
                  const2((2, hw)), const2((2, hw)),
                  const3((2, LANE, hw)), const3((2, LANE, hw)), const2((lg, hw)),
                  const2((LANE, LANE))],
        out_specs=[out_spec] * 11,
        scratch_shapes=[pltpu.VMEM((rb, zw), F32)],
        compiler_params=_params(("parallel",)),
        name="rw_streams",
    )(z, z, z, p["mu"], p["code"], p["k_k"], p["k_a"], p["r_k"], p["w0"], p["a0"],
      p["wup"], p["aup"], p["gup"], p["bd"])
    return dict(zip(names, outs))


def _wkv_kernel(lw_ref, kk_ref, b_ref, kd_ref, r_ref, v_ref, y_ref, st_ref, *, heads, tb, reverse):
    @pl.when(pl.program_id(1) == 0)
    def _():
        st_ref[...] = jnp.zeros(st_ref.shape, F32)

    n_chunks = tb // CHUNK
    t_idx = lax.broadcasted_iota(jnp.int32, (CHUNK, CHUNK), 0)
    s_idx = lax.broadcasted_iota(jnp.int32, (CHUNK, CHUNK), 1)
    strict = (s_idx > t_idx) if reverse else (s_idx < t_idx)
    incl = (s_idx >= t_idx) if reverse else (s_idx <= t_idx)
    eye = (s_idx == t_idx).astype(F32)

    bt = lax.broadcasted_iota(jnp.int32, (tb, tb), 0)
    bs = lax.broadcasted_iota(jnp.int32, (tb, tb), 1)
    before = (bs >= bt) if reverse else (bs <= bt)
    tri = jnp.where(((bt // CHUNK) == (bs // CHUNK)) & before, 1.0, 0.0).astype(BF16)
    lw = lw_ref[...]
    lw_hi = lw.astype(BF16)
    lw_lo = (lw - lw_hi.astype(F32)).astype(BF16)
    cum = _dot(tri, lw_hi) + _dot(tri, lw_lo)
    edge = 0 if reverse else CHUNK - 1
    tot = jnp.concatenate(
        [jnp.broadcast_to(cum[cc * CHUNK + edge:cc * CHUNK + edge + 1], (CHUNK, cum.shape[1]))
         for cc in range(n_chunks)], axis=0)
    kk, b, kd = (x[...].astype(F32) for x in (kk_ref, b_ref, kd_ref))
    e_neg = jnp.exp(-cum)
    e_end = jnp.exp(tot - cum)
    alpha = kk * jnp.exp(cum - lw)
    rho = r_ref[...].astype(F32) * jnp.exp(cum)
    ar_all = (alpha.astype(BF16), rho.astype(BF16))
    bk_all = ((b * e_neg).astype(BF16), (kd * e_neg).astype(BF16))
    beta_e = (b * e_end).astype(BF16)
    kappa_e = (kd * e_end).astype(BF16)
    g_end = jnp.exp(tot)
    v_all = v_ref[...]

    pairs = [(cc, h) for cc in range(n_chunks) for h in range(heads)]
    rs = lambda cc: slice(cc * CHUNK, (cc + 1) * CHUNK)
    ls = lambda h: slice(h * RW_HEAD, (h + 1) * RW_HEAD)
    cut = lambda x, cc, h: x[rs(cc), ls(h)]
    stack = lambda xs, cc, h: jnp.concatenate([cut(x, cc, h) for x in xs], axis=0)
    m = [_dot_nt(stack(ar_all, cc, h), stack(bk_all, cc, h)) for cc, h in pairs]
    pw = [jnp.where(strict, -x[:CHUNK, :CHUNK], 0.0) for x in m]
    m_akrk = [jnp.concatenate([jnp.where(strict, x[:CHUNK, CHUNK:], 0.0),
                               jnp.where(incl, x[CHUNK:, CHUNK:], 0.0)], axis=0).astype(BF16)
              for x in m]
    m_rb = [jnp.where(incl, x[CHUNK:, :CHUNK], 0.0).astype(BF16) for x in m]
    t_inv = [eye + x for x in pw]
    for _ in range(int(math.log2(CHUNK)) - 1):
        pwb = [x.astype(BF16) for x in pw]
        pw = [_dot(x, x) for x in pwb]
        t_inv = [t + _dot(t.astype(BF16), x.astype(BF16)) for t, x in zip(t_inv, pw)]
    mv = [_dot(x, cut(v_all, cc, h)) for x, (cc, h) in zip(m_akrk, pairs)]
    w12 = [_dot(t.astype(BF16),
                jnp.concatenate([x[:CHUNK].astype(BF16), cut(ar_all[0], cc, h)], axis=1)
                ).astype(BF16)
           for t, x, (cc, h) in zip(t_inv, mv, pairs)]
    mw = [_dot(x, w) for x, w in zip(m_rb, w12)]
    bw = [_dot_tn(cut(beta_e, cc, h), w) for w, (cc, h) in zip(w12, pairs)]
    kv = [_dot_tn(cut(kappa_e, cc, h), cut(v_all, cc, h)) for cc, h in pairs]
    pre = {}
    for i, (cc, h) in enumerate(pairs):
        y1 = mv[i][CHUNK:] - mw[i][:, :RW_HEAD]
        r2 = cut(rho, cc, h) - mw[i][:, RW_HEAD:]
        g_m = kv[i] - bw[i][:, :RW_HEAD]
        p_m = eye * g_end[cc * CHUNK:cc * CHUNK + 1, ls(h)] - bw[i][:, RW_HEAD:]
        pre[cc, h] = (y1, jnp.concatenate([r2, p_m], axis=0).astype(BF16), g_m)

    st = [st_ref[h] for h in range(heads)]
    ys = {}
    for cc in (range(n_chunks - 1, -1, -1) if reverse else range(n_chunks)):
        prod = [_dot(pre[cc, h][1], st[h].astype(BF16)) for h in range(heads)]
        for h in range(heads):
            ys[cc, h] = pre[cc, h][0] + prod[h][:CHUNK]
            st[h] = prod[h][CHUNK:] + pre[cc, h][2]
    for h in range(heads):
        st_ref[h] = st[h]
    for cc in range(n_chunks):
        y_ref[rs(cc), :] = jnp.concatenate([ys[cc, h] for h in range(heads)], axis=1)


def _wkv(st, d, n_lat, reverse):
    lw, b, kd = st[f"lw{d}"], st[f"b{d}"], st[f"kd{d}"]
    t, hw = lw.shape
    heads = _pick(hw // RW_HEAD, (16, 8, 4, 2))
    tb = _pick(math.gcd(n_lat, t - n_lat), (256, 128, 64))
    nb = t // tb
    n_lat_blocks = n_lat // tb
    if reverse:
        blk = lambda h, c: (nb - 1 - c, h)
    else:
        blk = lambda h, c: ((c + n_lat_blocks) % nb, h)
    spec = pl.BlockSpec((tb, heads * RW_HEAD), blk)
    return pl.pallas_call(
        functools.partial(_wkv_kernel, heads=heads, tb=tb, reverse=reverse),
        out_shape=jax.ShapeDtypeStruct((t, hw), F32),
        grid=(hw // (heads * RW_HEAD), nb),
        in_specs=[spec] * 6,
        out_specs=spec,
        scratch_shapes=[pltpu.VMEM((heads, RW_HEAD, RW_HEAD), F32)],
        compiler_params=_params(("parallel", "arbitrary")),
        name="wkv_bwd" if reverse else "wkv_fwd",
    )(lw, st["kk"], b, kd, st["r"], st["v"])


def _rw_out_kernel(yf_ref, yb_ref, bonus_ref, g_ref, lnw_ref, lnb_ref, bd_ref, o_ref):
    y = yf_ref[...] + yb_ref[...]
    bd = bd_ref[...]
    mean = _segsum(y, bd) / RW_HEAD
    yc = y - mean
    var = _segsum(yc * yc, bd) / RW_HEAD
    o = yc * lax.rsqrt(var + RW_GN_EPS) * lnw_ref[...] + lnb_ref[...] + bonus_ref[...].astype(F32)
    o_ref[...] = (o * g_ref[...].astype(F32)).astype(o_ref.dtype)


def _rw_output(y_f, y_b, st, p):
    t, hw = y_f.shape
    bm = _pick(t, (256, 128, 64))
    row = pl.BlockSpec((bm, hw), lambda i: (i, 0))
    vec = pl.BlockSpec((1, hw), lambda i: (0, 0))
    return pl.pallas_call(
        _rw_out_kernel,
        out_shape=jax.ShapeDtypeStruct((t, hw), BF16),
        grid=(t // bm,),
        in_specs=[row, row, row, row, vec, vec, pl.BlockSpec((LANE, LANE), lambda i: (0, 0))],
        out_specs=row,
        compiler_params=_params(("parallel",)),
        name="rw_output",
    )(y_f, y_b, st["bonus"], st["g"], p["ln_w"], p["ln_b"], p["bd"])


def _mla_prep_kernel(z_ref, qn_ref, qup_ref, kvn_ref, kvup_ref, qg_ref, kg_ref, c_ref, s1_ref,
                     s2_ref, q_out, k_out, v_out, *, off, q_rank, kv_rank, heads, q_scale):
    z = z_ref[...]

    def rms(x, g):
        return x * lax.rsqrt(jnp.mean(x * x, axis=-1, keepdims=True) + NORM_EPS) * g

    q = _dot(rms(z[:, off:off + q_rank], qn_ref[...]).astype(BF16), qup_ref[...])
    kv_lat = z[:, off + q_rank:off + q_rank + kv_rank]
    kv = _dot(rms(kv_lat, kvn_ref[...]).astype(BF16), kvup_ref[...])
    k_rope = z[:, off + q_rank + kv_rank:off + q_rank + kv_rank + MLA_ROPE]
    k_rope = jnp.concatenate([k_rope, jnp.zeros((z.shape[0], LANE - MLA_ROPE), F32)], axis=1)
    cos, s1, s2 = c_ref[...], s1_ref[...], s2_ref[...]
    sumsq = lambda x: jnp.sum(x * x, axis=-1, keepdims=True)
    inv_rms = lambda ss: lax.rsqrt(ss / MLA_QK + NORM_EPS)

    def rotary(x):
        return (x * cos + pltpu.roll(x, LANE - ROPE_FREQ, 1) * s1
                + pltpu.roll(x, ROPE_FREQ, 1) * s2)

    qg_n, qg_r = qg_ref[:, :MLA_NOPE], qg_ref[:, MLA_NOPE:]
    kg_n, kg_r = kg_ref[:, :MLA_NOPE], kg_ref[:, MLA_NOPE:]
    kr_ss = sumsq(k_rope)
    kr_rot = rotary(k_rope * kg_r)
    for h in range(heads):
        nope = slice(h * MLA_HEAD_PAD, h * MLA_HEAD_PAD + MLA_NOPE)
        rope = slice(h * MLA_HEAD_PAD + MLA_NOPE, (h + 1) * MLA_HEAD_PAD)
        qn, qr = q[:, nope], q[:, rope]
        q_inv = inv_rms(sumsq(qn) + sumsq(qr))
        q_out[:, nope] = (qn * q_inv * qg_n * q_scale).astype(BF16)
        q_out[:, rope] = (rotary(qr * q_inv * qg_r) * q_scale).astype(BF16)
        kn = kv[:, h * MLA_NOPE:(h + 1) * MLA_NOPE]
        k_inv = inv_rms(sumsq(kn) + kr_ss)
        k_out[:, nope] = (kn * k_inv * kg_n).astype(BF16)
        k_out[:, rope] = (kr_rot * k_inv).astype(BF16)
    v_out[...] = kv[:, heads * MLA_NOPE:].astype(BF16)


def _mla_prep(z, rw_cols, p, rope):
    t, n = z.shape
    heads = p["heads"]
    q_rank, kv_rank = p["q_up"].shape[0], p["kv_up"].shape[0]
    zw = min(w for w in range(LANE, n + 1, LANE) if n % w == 0 and n - w <= rw_cols)
    bm = _pick(t, (256, 128, 64))
    qw = heads * MLA_HEAD_PAD
    row = lambda w: pl.BlockSpec((bm, w), lambda i: (i, 0))
    const = lambda a: pl.BlockSpec(a.shape, lambda i: (0, 0))
    args = (z, p["q_norm"], p["q_up"], p["kv_norm"], p["kv_up"], p["q_gain"], p["k_gain"])
    return pl.pallas_call(
        functools.partial(_mla_prep_kernel, off=rw_cols - (n - zw), q_rank=q_rank, kv_rank=kv_rank,
                          heads=heads, q_scale=ATTN_Q_SCALE),
        out_shape=[jax.ShapeDtypeStruct((t, qw), BF16), jax.ShapeDtypeStruct((t, qw), BF16),
                   jax.ShapeDtypeStruct((t, heads * MLA_V_DIM), BF16)],
        grid=(t // bm,),
        in_specs=([pl.BlockSpec((bm, zw), lambda i: (i, n // zw - 1))]
                  + [const(a) for a in args[1:]] + [row(LANE)] * 3),
        out_specs=[row(qw), row(qw), row(heads * MLA_V_DIM)],
        compiler_params=_params(("parallel",)),
        name="mla_prep",
    )(*args, *rope)


def _attn_kernel(q_ref, k_ref, vt_ref, o_ref, *, n_lat, bq, hps, with_ctx):
    def attend(key0, n_keys):
        bk = _pick(n_keys, (2816, 1408, 768, 512, 256, 128))
        qcols = lambda c: slice(c * MLA_HEAD_PAD, (c + 1) * MLA_HEAD_PAD)
        q = [q_ref[:, qcols(c)] for c in range(hps)]
        keys = lambda j: slice(key0 + j * bk, key0 + (j + 1) * bk)
        scores = lambda j: [_dot_nt(k_ref[keys(j), qcols(c)], q[c]) for c in range(hps)]
        s_next = scores(0)
        m = [jnp.full((1, bq), -jnp.inf, F32)] * hps
        acc = [jnp.zeros((V_ROWS, bq), F32)] * hps
        for j in range(n_keys // bk):
            s = s_next
            if (j + 1) * bk < n_keys:
                s_next = scores(j + 1)
            for c in range(hps):
                m_new = jnp.maximum(m[c], jnp.max(s[c], axis=0, keepdims=True))
                p = jnp.exp2(s[c] - m_new).astype(BF16)
                acc[c] = (jnp.exp2(m[c] - m_new) * acc[c]
                          + _dot(vt_ref[c * V_ROWS:(c + 1) * V_ROWS, keys(j)], p))
                m[c] = m_new
        for c in range(hps):
            o_ref[:, c * MLA_V_DIM:(c + 1) * MLA_V_DIM] = (
                acc[c][:MLA_V_DIM] / acc[c][MLA_V_DIM:MLA_V_DIM + 1]).T.astype(o_ref.dtype)

    n_all = k_ref.shape[0]
    if not with_ctx:
        attend(0, n_all)
        return
    is_latent = pl.program_id(1) < n_lat // bq

    @pl.when(is_latent)
    def _():
        attend(0, n_all)

    @pl.when(jnp.logical_not(is_latent))
    def _():
        attend(n_lat, n_all - n_lat)


def _attention(q, k, vt, heads, n_lat, with_ctx):
    t = k.shape[0]
    n_q = t if with_ctx else n_lat
    bq = _pick(math.gcd(n_lat, t - n_lat), (256, 128, 64))
    hps = 4
    return pl.pallas_call(
        functools.partial(_attn_kernel, n_lat=n_lat, bq=bq, hps=hps, with_ctx=with_ctx),
        out_shape=jax.ShapeDtypeStruct((n_q, heads * MLA_V_DIM), BF16),
        grid=(heads // hps, n_q // bq),
        in_specs=[pl.BlockSpec((bq, hps * MLA_HEAD_PAD), lambda h, i: (i, h)),
                  _resident((t, hps * MLA_HEAD_PAD), lambda h, i: (0, h)),
                  _resident((hps * V_ROWS, t), lambda h, i: (h, 0))],
        out_specs=pl.BlockSpec((bq, hps * MLA_V_DIM), lambda h, i: (i, h)),
        compiler_params=_params(("parallel", "parallel")),
        name="attention",
    )(q, k, vt)


def _pad_cols(w, width):
    return jnp.pad(w, ((0, 0), (0, width - w.shape[1])))


def _rope_tables(n_lat, n_ctx):
    t = jnp.arange(n_lat, dtype=jnp.int32)
    pos = jnp.stack([t // GRID_W, t % GRID_W], axis=-1).astype(F32)
    inv_freq = ROPE_THETA ** (-jnp.arange(ROPE_FREQ, dtype=F32) / ROPE_FREQ)
    ang = pos[:, :, None] * inv_freq
    cos, sin = jnp.cos(ang), jnp.sin(ang)
    zero = jnp.zeros_like(sin)
    lay = lambda first, second: jnp.stack([first, second], axis=2).reshape(n_lat, MLA_ROPE)
    tabs = []
    for rope_part, fill in ((lay(cos, cos), 1.0), (lay(-sin, zero), 0.0), (lay(zero, sin), 0.0)):
        tab = jnp.concatenate([rope_part, jnp.full((n_lat, LANE - MLA_ROPE), fill, F32)], axis=1)
        tabs.append(jnp.concatenate([tab, jnp.full((n_ctx, LANE), fill, F32)], axis=0))
    return tuple(tabs)


def _layer_params(l, rw_mu, rw_w0, rw_w_up, rw_a0, rw_a_up, rw_g_up, rw_k_k, rw_k_a, rw_r_k,
                  rw_ln_w, rw_ln_b, mla_q_norm, mla_q_up, mla_kv_norm, mla_kv_up, mla_q_gain,
                  mla_k_gain):
    hw = rw_k_k.shape[1]
    ld, la, lg = rw_w_up.shape[2], rw_a_up.shape[2], rw_g_up.shape[1]
    q_rank, kv_rank = mla_q_up.shape[1], mla_kv_up.shape[1]
    heads = mla_kv_up.shape[2] // (MLA_NOPE + MLA_V_DIM)
    rw_cols = 3 * hw + ld + la + lg
    zw = -(-rw_cols // LANE) * LANE
    assert ld <= LANE and la <= LANE and rw_cols % 4 == 0 and 3 * hw + ld + LANE <= zw
    code = np.minimum(np.arange(zw, dtype=np.int32) // (rw_cols // 4), 3)[None, :]
    bd = (np.arange(LANE)[:, None] // RW_HEAD == np.arange(LANE)[None, :] // RW_HEAD)
    pad_rows = lambda x: jnp.pad(x, ((0, 0), (0, LANE - x.shape[1]), (0, 0)))
    return {
        "hw": hw, "heads": heads, "rw_cols": rw_cols, "zw": zw, "ld": ld, "la": la,
        "mu": _pad_cols(rw_mu[l][None, :], zw),
        "code": jnp.asarray(code),
        "k_k": rw_k_k[l][None, :], "k_a": rw_k_a[l][None, :], "r_k": rw_r_k[l].reshape(1, hw),
        "w0": rw_w0[l], "a0": rw_a0[l],
        "wup": pad_rows(rw_w_up[l]).astype(BF16), "aup": pad_rows(rw_a_up[l]).astype(BF16),
        "gup": rw_g_up[l].astype(BF16),
        "bd": jnp.asarray(bd, BF16),
        "ln_w": rw_ln_w[l][None, :], "ln_b": rw_ln_b[l][None, :],
        "q_norm": mla_q_norm[l][None, :], "kv_norm": mla_kv_norm[l][None, :],
        "q_up": jnp.pad(mla_q_up[l].reshape(q_rank, heads, MLA_QK),
                        ((0, 0), (0, 0), (0, MLA_HEAD_PAD - MLA_QK))
                        ).reshape(q_rank, heads * MLA_HEAD_PAD).astype(BF16),
        "kv_up": mla_kv_up[l].reshape(kv_rank, heads, 2, MLA_NOPE).transpose(0, 2, 1, 3)
                             .reshape(kv_rank, 2 * heads * MLA_NOPE).astype(BF16),
        "q_gain": _pad_cols(mla_q_gain[l][None, :], MLA_HEAD_PAD),
        "k_gain": _pad_cols(mla_k_gain[l][None, :], MLA_HEAD_PAD),
    }


def _ffn_half_step(xt, mod, g, ffn_w_in, ffn_w_out, lead, n_lat):
    h = _norm_mod(xt, g, mod[:, 0], mod[:, 1], n_lat)
    act, w_out_bf16 = _swiglu_matmul(h, ffn_w_in, lead, ffn_w_out)
    return _residual_matmul([act], w_out_bf16, xt, FFN_RES * mod[:, 2], n_lat)


def kernel(x, c, ctx, c_ctx, w_mod, b_mod, norm_g, ffn_w_in, ffn_w_out, w_in, w_out, rw_mu, rw_w0,
           rw_w_up, rw_a0, rw_a_up, rw_g_up, rw_k_k, rw_k_a, rw_r_k, rw_ln_w, rw_ln_b, mla_q_norm,
           mla_q_up, mla_kv_norm, mla_kv_up, mla_q_gain, mla_k_gain):
    batch, n_lat, d = x.shape
    n_ctx = ctx.shape[1]
    depth = w_mod.shape[0]
    assert batch == 1 and c.shape[0] == 1 and ctx.shape[0] == 1
    assert n_lat % GRID_W == 0 and n_ctx % GRID_W == 0

    xt = jnp.concatenate([x[0], ctx[0]], axis=0)
    cc = jnp.zeros((8, d), F32).at[0].set(c[0]).at[1].set(c_ctx)
    rope = _rope_tables(n_lat, n_ctx)

    for l in range(depth):
        last = l == depth - 1
        p = _layer_params(l, rw_mu, rw_w0, rw_w_up, rw_a0, rw_a_up, rw_g_up, rw_k_k, rw_k_a,
                          rw_r_k, rw_ln_w, rw_ln_b, mla_q_norm, mla_q_up, mla_kv_norm, mla_kv_up,
                          mla_q_gain, mla_k_gain)
        mod = _modulation(cc, w_mod, b_mod, l)[:2].reshape(2, N_MOD, d)

        xt = _ffn_half_step(xt, mod[:, 0:3], norm_g[l, 0], ffn_w_in, ffn_w_out, (l, 0), n_lat)

        hz = _norm_mod(xt, norm_g[l, 1], mod[:, 3], mod[:, 4], n_lat)
        z, w_out_bf16 = _matmul(hz, w_in, (l,), w_out, (l,))

        st = _rw_streams(z, p, n_lat)
        y_f = _wkv(st, 0, n_lat, reverse=False)
        y_b = _wkv(st, 1, n_lat, reverse=True)
        o_rw = _rw_output(y_f, y_b, st, p)

        q, k, v = _mla_prep(z, p["rw_cols"], p, rope)
        vt = jnp.concatenate([v.T.reshape(p["heads"], MLA_V_DIM, -1),
                              jnp.ones((p["heads"], V_ROWS - MLA_V_DIM, v.shape[0]), BF16)], axis=1)
        o_mla = _attention(q, k, vt.reshape(p["heads"] * V_ROWS, -1), p["heads"], n_lat,
                           with_ctx=not last)

        xt = _residual_matmul([o_rw, o_mla], w_out_bf16, xt, mod[:, 5], n_lat,
                              rows=n_lat if last else None)

        xt = _ffn_half_step(xt, mod[:, 6:9], norm_g[l, 2], ffn_w_in, ffn_w_out, (l, 1), n_lat)
    return xt[:n_lat][None]
```

```python
import functools
import math

import numpy as np
import jax
import jax.numpy as jnp
from jax import lax
from jax.experimental import pallas as pl
from jax.experimental.pallas import tpu as pltpu

F32 = jnp.float32
BF16 = jnp.bfloat16

GRID_W = 64
NORM_EPS = 1e-6
FFN_RES = 0.5
RW_HEAD = 64
RW_GN_EPS = 64e-5
RW_DECAY_SCALE = math.exp(-0.5)
MLA_V_DIM = 128
MLA_NOPE = 128
MLA_ROPE = 64
MLA_QK = MLA_NOPE + MLA_ROPE
MLA_HEAD_PAD = 256
ROPE_FREQ = MLA_ROPE // 4
ROPE_THETA = 10000.0
N_MOD = 9
LANE = 128
CHUNK = 64
ATTN_Q_SCALE = math.log2(math.e) / math.sqrt(MLA_QK)
V_ROWS = MLA_V_DIM + 16
VMEM_LIMIT = 60 * 1024 * 1024


def _pick(n, candidates):
    for c in candidates:
        if n % c == 0:
            return c
    raise ValueError(f"no block size for {n} among {candidates}")


def _params(sem):
    return pltpu.CompilerParams(dimension_semantics=sem, vmem_limit_bytes=VMEM_LIMIT)


def _dot(a, b):
    return jnp.dot(a, b, preferred_element_type=F32)


def _dot_nt(a, b):
    return lax.dot_general(a, b, (((1,), (1,)), ((), ())), preferred_element_type=F32)


def _dot_tn(a, b):
    return lax.dot_general(a, b, (((0,), (0,)), ((), ())), preferred_element_type=F32)


def _wspec(lead, block, index_map):
    return pl.BlockSpec((None,) * len(lead) + block, lambda *g: lead + index_map(*g))


def _row_select(row0, n_rows, n_lat, ref):
    rows = row0 + lax.broadcasted_iota(jnp.int32, (n_rows, 1), 0)
    return jnp.where(rows < n_lat, ref[0:1, :], ref[1:2, :])


def _mod_kernel(c_ref, w_ref, b_ref, o_ref):
    c = c_ref[...]
    a = (c * jax.nn.sigmoid(c)).astype(BF16)
    o_ref[...] = _dot(a, w_ref[...].astype(BF16)) + b_ref[...]


def _modulation(cc, w_mod, b_mod, l):
    _, d, n = w_mod.shape
    bn = _pick(n, (512, 256, 128))
    return pl.pallas_call(
        _mod_kernel,
        out_shape=jax.ShapeDtypeStruct((8, n), F32),
        grid=(n // bn,),
        in_specs=[pl.BlockSpec((8, d), lambda j: (0, 0)),
                  _wspec((l,), (d, bn), lambda j: (0, j)),
                  _wspec((l,), (1, bn), lambda j: (0, j))],
        out_specs=pl.BlockSpec((8, bn), lambda j: (0, j)),
        compiler_params=_params(("parallel",)),
        name="modulation",
    )(cc, w_mod, b_mod.reshape(b_mod.shape[0], 1, n))


def _norm_mod_kernel(x_ref, g_ref, sh_ref, sc_ref, o_ref, *, n_lat, bm):
    x = x_ref[...]
    y = x * lax.rsqrt(jnp.mean(x * x, axis=-1, keepdims=True) + NORM_EPS) * g_ref[...]
    kind = pl.ds((pl.program_id(0) * bm >= n_lat).astype(jnp.int32), 1)
    o_ref[...] = (y * (1.0 + sc_ref[kind, :]) + sh_ref[kind, :]).astype(o_ref.dtype)


def _norm_mod(x, g, shift, scale, n_lat):
    m, d = x.shape
    bm = _pick(math.gcd(n_lat, m - n_lat) if m > n_lat else m, (256, 128, 64, 32, 16))
    return pl.pallas_call(
        functools.partial(_norm_mod_kernel, n_lat=n_lat, bm=bm),
        out_shape=jax.ShapeDtypeStruct((m, d), BF16),
        grid=(m // bm,),
        in_specs=[pl.BlockSpec((bm, d), lambda i: (i, 0)),
                  pl.BlockSpec((1, d), lambda i: (0, 0)),
                  pl.BlockSpec((2, d), lambda i: (0, 0)),
                  pl.BlockSpec((2, d), lambda i: (0, 0))],
        out_specs=pl.BlockSpec((bm, d), lambda i: (i, 0)),
        compiler_params=_params(("parallel",)),
        name="norm_mod",
    )(x, g.reshape(1, d), shift, scale)


def _resident(block_shape, index_map):
    return pl.BlockSpec(block_shape, index_map, pipeline_mode=pl.Buffered(1))


def _side_cast(side, side_lead, steps, nb):
    rows, cols = side.shape[-2:]
    sr = min(r for r in range(16, rows + 1, 16) if rows % r == 0 and rows // r <= steps)
    slab = lambda i, j: (jnp.minimum(i * nb + j, rows // sr - 1), 0)
    return (_wspec(side_lead, (sr, cols), slab), pl.BlockSpec((sr, cols), slab),
            jax.ShapeDtypeStruct((rows, cols), BF16))


def _mm_kernel(a_ref, w_ref, side_ref, o_ref, side_out):
    o_ref[...] = _dot(a_ref[...], w_ref[...].astype(BF16)).astype(o_ref.dtype)
    side_out[...] = side_ref[...].astype(BF16)


def _matmul(a, w, lead, side, side_lead):
    m, k = a.shape
    n = w.shape[-1]
    bm = _pick(m, (1056, 1024, 512, 320, 256, 128, 64))
    bn = _pick(n, (512, 256, 128))
    side_in, side_out, side_shape = _side_cast(side, side_lead, (m // bm) * (n // bn), n // bn)
    return pl.pallas_call(
        _mm_kernel,
        out_shape=[jax.ShapeDtypeStruct((m, n), F32), side_shape],
        grid=(m // bm, n // bn),
        in_specs=[pl.BlockSpec((bm, k), lambda i, j: (i, 0)),
                  _wspec(lead, (k, bn), lambda i, j: (0, j)), side_in],
        out_specs=[pl.BlockSpec((bm, bn), lambda i, j: (i, j)), side_out],
        compiler_params=_params(("parallel", "arbitrary")),
        name="matmul",
    )(a, w, side)


def _swiglu_kernel(a_ref, wg_ref, wu_ref, side_ref, o_ref, side_out):
    a = a_ref[...]
    gate = _dot(a, wg_ref[...].astype(BF16))
    up = _dot(a, wu_ref[...].astype(BF16))
    o_ref[...] = (gate * jax.nn.sigmoid(gate) * up).astype(o_ref.dtype)
    side_out[...] = side_ref[...].astype(BF16)


def _swiglu_matmul(a, w, lead, side):
    m, k = a.shape
    f = w.shape[-1] // 2
    bm = _pick(m, (1056, 1024, 512, 320, 256, 128, 64))
    bn = _pick(f, (512, 256, 128))
    nb = f // bn
    side_in, side_out, side_shape = _side_cast(side, lead, (m // bm) * nb, nb)
    return pl.pallas_call(
        _swiglu_kernel,
        out_shape=[jax.ShapeDtypeStruct((m, f), BF16), side_shape],
        grid=(m // bm, nb),
        in_specs=[_resident((bm, k), lambda i, j: (i, 0)),
                  _wspec(lead, (k, bn), lambda i, j: (0, j)),
                  _wspec(lead, (k, bn), lambda i, j: (0, j + nb)), side_in],
        out_specs=[pl.BlockSpec((bm, bn), lambda i, j: (i, j)), side_out],
        compiler_params=_params(("parallel", "arbitrary")),
        name="swiglu_matmul",
    )(a, w, w, side)


def _residual_kernel(*refs, n_a, n_lat, bm):
    a_refs = refs[:n_a]
    w_refs = refs[n_a:2 * n_a]
    x_ref, g_ref, o_ref = refs[2 * n_a:]
    acc = _dot(a_refs[0][...], w_refs[0][...])
    for a_ref, w_ref in zip(a_refs[1:], w_refs[1:]):
        acc = acc + _dot(a_ref[...], w_ref[...])
    gate = _row_select(pl.program_id(0) * bm, bm, n_lat, g_ref)
    o_ref[...] = x_ref[...] + gate * acc


def _residual_matmul(a_list, w, x, gate, n_lat, rows=None):
    m, n = (rows or x.shape[0]), x.shape[1]
    n_a = len(a_list)
    k = a_list[0].shape[1]
    assert all(a.shape[1] == k for a in a_list) and w.shape == (n_a * k, n) and w.dtype == BF16
    bm = _pick(m, (1056, 1024, 512, 320, 256, 128, 64))
    a_vmem = 2 * n_a * bm * k * a_list[0].dtype.itemsize
    bn = _pick(n, (512, 256, 128) if a_vmem <= VMEM_LIMIT // 3 else (256, 128))
    a_specs = [pl.BlockSpec((bm, k), lambda i, j: (i, 0)) for _ in a_list]
    w_specs = [pl.BlockSpec((k, bn), functools.partial(lambda i, j, p: (p, j), p=p))
               for p in range(n_a)]
    return pl.pallas_call(
        functools.partial(_residual_kernel, n_a=n_a, n_lat=n_lat, bm=bm),
        out_shape=jax.ShapeDtypeStruct((m, n), F32),
        grid=(m // bm, n // bn),
        in_specs=a_specs + w_specs + [pl.BlockSpec((bm, bn), lambda i, j: (i, j)),
                                      pl.BlockSpec((2, bn), lambda i, j: (0, j))],
        out_specs=pl.BlockSpec((bm, bn), lambda i, j: (i, j)),
        compiler_params=_params(("parallel", "arbitrary")),
        name="residual_matmul",
    )(*a_list, *([w] * n_a), x, gate)


def _segsum(x, bd):
    hi = x.astype(BF16)
    r1 = x - hi.astype(F32)
    mid = r1.astype(BF16)
    lo = (r1 - mid.astype(F32)).astype(BF16)
    parts = [_dot(hi[:, j:j + LANE], bd) + _dot(mid[:, j:j + LANE], bd) + _dot(lo[:, j:j + LANE], bd)
             for j in range(0, x.shape[1], LANE)]
    return jnp.concatenate(parts, axis=1)


def _streams_kernel(zc_ref, zp_ref, zn_ref, mu_ref, code_ref, kkw_ref, ka_ref, rk_ref, w0_ref,
                    a0_ref, wup_ref, aup_ref, gup_ref, bd_ref,
                    r_out, v_out, kk_out, lw0_out, lw1_out, b0_out, b1_out, kd0_out, kd1_out,
                    g_out, bonus_out, zs_ref, *, rb, n_lat_blocks, n_blocks, hw, ld, la, lg,
                    rw_cols):
    i = pl.program_id(0)
    rows = lax.broadcasted_iota(jnp.int32, (rb, 1), 0)
    quarter = rw_cols // 4
    n_tiles = zc_ref.shape[1] // LANE

    def token_shift(neighbour):
        for c in range(n_tiles):
            cols = slice(c * LANE, (c + 1) * LANE)
            z = zc_ref[:, cols]
            kinds = sorted({min(col // quarter, 3) for col in (c * LANE, (c + 1) * LANE - 1)})
            shifted = neighbour(kinds[-1], cols)
            for kind in reversed(kinds[:-1]):
                shifted = jnp.where(code_ref[:, cols] <= kind, neighbour(kind, cols), shifted)
            zs_ref[:, cols] = z + (shifted - z) * mu_ref[:, cols]

    prev_tok = lambda cols: pltpu.roll(zc_ref[:, cols], 1, 0)
    next_tok = lambda cols: pltpu.roll(zc_ref[:, cols], rb - 1, 0)

    @pl.when(i < n_lat_blocks)
    def _():
        def neighbour(kind, cols):
            if kind == 0:
                return jnp.where(rows % GRID_W == 0, 0.0, prev_tok(cols))
            if kind == 1:
                return jnp.where(rows % GRID_W == GRID_W - 1, 0.0, next_tok(cols))
            if kind == 2:
                above = zp_ref[:, cols] * (i > 0).astype(F32)
                return jnp.concatenate([above, zc_ref[:rb - GRID_W, cols]], axis=0)
            below = zn_ref[:, cols] * (i < n_lat_blocks - 1).astype(F32)
            return jnp.concatenate([zc_ref[GRID_W:, cols], below], axis=0)
        token_shift(neighbour)

    @pl.when(i >= n_lat_blocks)
    def _():
        def neighbour(kind, cols):
            if kind < 2:
                before = zp_ref[GRID_W - 1:GRID_W, cols] * (i > n_lat_blocks).astype(F32)
                return jnp.where(rows == 0, before, prev_tok(cols))
            after = zn_ref[0:1, cols] * (i < n_blocks - 1).astype(F32)
            return jnp.where(rows == rb - 1, after, next_tok(cols))
        token_shift(neighbour)

    zs = zs_ref[...]

    r = zs[:, 0:hw]
    k = zs[:, hw:2 * hw]
    v = zs[:, 2 * hw:3 * hw]
    wd = zs[:, 3 * hw:3 * hw + LANE]
    ad = zs[:, 3 * hw + ld:3 * hw + ld + LANE]
    gd = zs[:, 3 * hw + ld + la:3 * hw + ld + la + lg]
    bd = bd_ref[...]

    kk = k * kkw_ref[...]
    kk = kk / jnp.maximum(jnp.sqrt(_segsum(kk * kk, bd)), 1e-12)
    tw = jnp.tanh(wd).astype(BF16)
    adb = ad.astype(BF16)
    ka = ka_ref[...]
    kds = []
    for d, (lw_out, b_out, kd_out) in enumerate(((lw0_out, b0_out, kd0_out),
                                                 (lw1_out, b1_out, kd1_out))):
        w_logit = w0_ref[d:d + 1, :] + _dot(tw, wup_ref[d])
        lw_out[...] = -RW_DECAY_SCALE * jax.nn.sigmoid(w_logit)
        a = jax.nn.sigmoid(a0_ref[d:d + 1, :] + _dot(adb, aup_ref[d]))
        kd = k * (1.0 + (a - 1.0) * ka)
        kd_out[...] = kd.astype(kd_out.dtype)
        b_out[...] = (kk * a).astype(b_out.dtype)
        kds.append(kd)
    g_out[...] = _dot(jax.nn.sigmoid(gd).astype(BF16), gup_ref[...]).astype(g_out.dtype)
    kmean = (kds[0] + kds[1]) / 2.0
    bonus_out[...] = (_segsum(r * kmean * rk_ref[...], bd) * v).astype(bonus_out.dtype)
    r_out[...] = r.astype(r_out.dtype)
    v_out[...] = v.astype(v_out.dtype)
    kk_out[...] = kk.astype(kk_out.dtype)


def _rw_streams(z, p, n_lat):
    t = z.shape[0]
    hw, zw = p["hw"], p["zw"]
    lg = p["gup"].shape[0]
    rb = _pick(math.gcd(n_lat, t - n_lat), (256, 128, 64))
    n_blocks = t // rb
    n_lat_blocks = n_lat // rb
    halo = lambda f: pl.BlockSpec((GRID_W, zw), f)
    per = rb // GRID_W
    const2 = lambda shape: pl.BlockSpec(shape, lambda i: (0, 0))
    const3 = lambda shape: pl.BlockSpec(shape, lambda i: (0, 0, 0))
    out_spec = pl.BlockSpec((rb, hw), lambda i: (i, 0))
    out_dtypes = {"lw0": F32, "lw1": F32}
    names = ("r", "v", "kk", "lw0", "lw1", "b0", "b1", "kd0", "kd1", "g", "bonus")
    outs = pl.pallas_call(
        functools.partial(_streams_kernel, rb=rb, n_lat_blocks=n_lat_blocks, n_blocks=n_blocks,
                          hw=hw, ld=p["ld"], la=p["la"], lg=lg, rw_cols=p["rw_cols"]),
        out_shape=[jax.ShapeDtypeStruct((t, hw), out_dtypes.get(n, BF16)) for n in names],
        grid=(n_blocks,),
        in_specs=[pl.BlockSpec((rb, zw), lambda i: (i, 0)),
                  halo(lambda i: (jnp.maximum(i * per - 1, 0), 0)),
                  halo(lambda i: (jnp.minimum((i + 1) * per, t // GRID_W - 1), 0)),
                  const2((1, zw)), const2((1, zw)),
                  const2((1, hw)), const2((1, hw)), const2((1, hw)),
                  const2((2, hw)), const2((2, hw)),
                  const3((2, LANE, hw)), const3((2, LANE, hw)), const2((lg, hw)),
                  const2((LANE, LANE))],
        out_specs=[out_spec] * 11,
        scratch_shapes=[pltpu.VMEM((rb, zw), F32)],
        compiler_params=_params(("parallel",)),
        name="rw_streams",
    )(z, z, z, p["mu"], p["code"], p["k_k"], p["k_a"], p["r_k"], p["w0"], p["a0"],
      p["wup"], p["aup"], p["gup"], p["bd"])
    return dict(zip(names, outs))


def _group_norm_out(y, bonus, g, ln_w, ln_b, bd):
    mean = _segsum(y, bd) / RW_HEAD
    yc = y - mean
    var = _segsum(yc * yc, bd) / RW_HEAD
    o = yc * lax.rsqrt(var + RW_GN_EPS) * ln_w + ln_b + bonus.astype(F32)
    return (o * g.astype(F32)).astype(BF16)


def _wkv_kernel(*refs, heads, tb, reverse, with_output):
    lw_ref, kk_ref, b_ref, kd_ref, r_ref, v_ref = refs[:6]
    out_ref, st_ref = refs[-2:]

    @pl.when(pl.program_id(1) == 0)
    def _():
        st_ref[...] = jnp.zeros(st_ref.shape, F32)

    n_chunks = tb // CHUNK
    t_idx = lax.broadcasted_iota(jnp.int32, (CHUNK, CHUNK), 0)
    s_idx = lax.broadcasted_iota(jnp.int32, (CHUNK, CHUNK), 1)
    strict = (s_idx > t_idx) if reverse else (s_idx < t_idx)
    incl = (s_idx >= t_idx) if reverse else (s_idx <= t_idx)
    eye = (s_idx == t_idx).astype(F32)

    bt = lax.broadcasted_iota(jnp.int32, (tb, tb), 0)
    bs = lax.broadcasted_iota(jnp.int32, (tb, tb), 1)
    before = (bs >= bt) if reverse else (bs <= bt)
    tri = jnp.where(((bt // CHUNK) == (bs // CHUNK)) & before, 1.0, 0.0).astype(BF16)
    lw = lw_ref[...]
    lw_hi = lw.astype(BF16)
    lw_lo = (lw - lw_hi.astype(F32)).astype(BF16)
    cum = _dot(tri, lw_hi) + _dot(tri, lw_lo)
    edge = 0 if reverse else CHUNK - 1
    tot = jnp.concatenate(
        [jnp.broadcast_to(cum[cc * CHUNK + edge:cc * CHUNK + edge + 1], (CHUNK, cum.shape[1]))
         for cc in range(n_chunks)], axis=0)
    kk, b, kd = (x[...].astype(F32) for x in (kk_ref, b_ref, kd_ref))
    e_neg = jnp.exp(-cum)
    e_end = jnp.exp(tot - cum)
    alpha = kk * jnp.exp(cum - lw)
    rho = r_ref[...].astype(F32) * jnp.exp(cum)
    ar_all = (alpha.astype(BF16), rho.astype(BF16))
    bk_all = ((b * e_neg).astype(BF16), (kd * e_neg).astype(BF16))
    beta_e = (b * e_end).astype(BF16)
    kappa_e = (kd * e_end).astype(BF16)
    g_end = jnp.exp(tot)
    v_all = v_ref[...]

    pairs = [(cc, h) for cc in range(n_chunks) for h in range(heads)]
    rs = lambda cc: slice(cc * CHUNK, (cc + 1) * CHUNK)
    ls = lambda h: slice(h * RW_HEAD, (h + 1) * RW_HEAD)
    cut = lambda x, cc, h: x[rs(cc), ls(h)]
    stack = lambda xs, cc, h: jnp.concatenate([cut(x, cc, h) for x in xs], axis=0)
    m = [_dot_nt(stack(ar_all, cc, h), stack(bk_all, cc, h)) for cc, h in pairs]
    pw = [jnp.where(strict, -x[:CHUNK, :CHUNK], 0.0) for x in m]
    m_akrk = [jnp.concatenate([jnp.where(strict, x[:CHUNK, CHUNK:], 0.0),
                               jnp.where(incl, x[CHUNK:, CHUNK:], 0.0)], axis=0).astype(BF16)
              for x in m]
    m_rb = [jnp.where(incl, x[CHUNK:, :CHUNK], 0.0).astype(BF16) for x in m]
    t_inv = [eye + x for x in pw]
    for _ in range(int(math.log2(CHUNK)) - 1):
        pwb = [x.astype(BF16) for x in pw]
        pw = [_dot(x, x) for x in pwb]
        t_inv = [t + _dot(t.astype(BF16), x.astype(BF16)) for t, x in zip(t_inv, pw)]
    mv = [_dot(x, cut(v_all, cc, h)) for x, (cc, h) in zip(m_akrk, pairs)]
    w12 = [_dot(t.astype(BF16),
                jnp.concatenate([x[:CHUNK].astype(BF16), cut(ar_all[0], cc, h)], axis=1)
                ).astype(BF16)
           for t, x, (cc, h) in zip(t_inv, mv, pairs)]
    mw = [_dot(x, w) for x, w in zip(m_rb, w12)]
    bw = [_dot_tn(cut(beta_e, cc, h), w) for w, (cc, h) in zip(w12, pairs)]
    kv = [_dot_tn(cut(kappa_e, cc, h), cut(v_all, cc, h)) for cc, h in pairs]
    pre = {}
    for i, (cc, h) in enumerate(pairs):
        y1 = mv[i][CHUNK:] - mw[i][:, :RW_HEAD]
        r2 = cut(rho, cc, h) - mw[i][:, RW_HEAD:]
        g_m = kv[i] - bw[i][:, :RW_HEAD]
        p_m = eye * g_end[cc * CHUNK:cc * CHUNK + 1, ls(h)] - bw[i][:, RW_HEAD:]
        pre[cc, h] = (y1, jnp.concatenate([r2, p_m], axis=0).astype(BF16), g_m)

    st = [st_ref[h] for h in range(heads)]
    ys = {}
    for cc in (range(n_chunks - 1, -1, -1) if reverse else range(n_chunks)):
        prod = [_dot(pre[cc, h][1], st[h].astype(BF16)) for h in range(heads)]
        for h in range(heads):
            ys[cc, h] = pre[cc, h][0] + prod[h][:CHUNK]
            st[h] = prod[h][CHUNK:] + pre[cc, h][2]
    for h in range(heads):
        st_ref[h] = st[h]
    y = jnp.concatenate([jnp.concatenate([ys[cc, h] for h in range(heads)], axis=1)
                         for cc in range(n_chunks)], axis=0)
    if with_output:
        y_other, bonus_ref, g_ref, lnw_ref, lnb_ref, bd_ref = refs[6:-2]
        out_ref[...] = _group_norm_out(y + y_other[...], bonus_ref[...], g_ref[...], lnw_ref[...],
                                       lnb_ref[...], bd_ref[...])
    else:
        out_ref[...] = y


def _wkv(st, d, n_lat, reverse, y_other=None, p=None):
    lw, b, kd = st[f"lw{d}"], st[f"b{d}"], st[f"kd{d}"]
    t, hw = lw.shape
    heads = _pick(hw // RW_HEAD, (16, 8, 4, 2))
    tb = _pick(math.gcd(n_lat, t - n_lat), (256, 128, 64))
    nb = t // tb
    n_lat_blocks = n_lat // tb
    if reverse:
        blk = lambda h, c: (nb - 1 - c, h)
    else:
        blk = lambda h, c: ((c + n_lat_blocks) % nb, h)
    spec = pl.BlockSpec((tb, heads * RW_HEAD), blk)
    args, specs = [lw, st["kk"], b, kd, st["r"], st["v"]], [spec] * 6
    if y_other is not None:
        vec = pl.BlockSpec((1, heads * RW_HEAD), lambda h, c: (0, h))
        args += [y_other, st["bonus"], st["g"], p["ln_w"], p["ln_b"], p["bd"]]
        specs += [spec, spec, spec, vec, vec, pl.BlockSpec((LANE, LANE), lambda h, c: (0, 0))]
    return pl.pallas_call(
        functools.partial(_wkv_kernel, heads=heads, tb=tb, reverse=reverse,
                          with_output=y_other is not None),
        out_shape=jax.ShapeDtypeStruct((t, hw), F32 if y_other is None else BF16),
        grid=(hw // (heads * RW_HEAD), nb),
        in_specs=specs,
        out_specs=spec,
        scratch_shapes=[pltpu.VMEM((heads, RW_HEAD, RW_HEAD), F32)],
        compiler_params=_params(("parallel", "arbitrary")),
        name="wkv_bwd" if reverse else "wkv_fwd",
    )(*args)


def _mla_prep_kernel(z_ref, qn_ref, qup_ref, kvn_ref, kvup_ref, qg_ref, kg_ref, c_ref, s1_ref,
                     s2_ref, q_out, k_out, v_out, *, off, q_rank, kv_rank, heads, q_scale):
    z = z_ref[...]

    def rms(x, g):
        return x * lax.rsqrt(jnp.mean(x * x, axis=-1, keepdims=True) + NORM_EPS) * g

    q = _dot(rms(z[:, off:off + q_rank], qn_ref[...]).astype(BF16), qup_ref[...])
    kv_lat = z[:, off + q_rank:off + q_rank + kv_rank]
    kv = _dot(rms(kv_lat, kvn_ref[...]).astype(BF16), kvup_ref[...])
    k_rope = z[:, off + q_rank + kv_rank:off + q_rank + kv_rank + MLA_ROPE]
    k_rope = jnp.concatenate([k_rope, jnp.zeros((z.shape[0], LANE - MLA_ROPE), F32)], axis=1)
    cos, s1, s2 = c_ref[...], s1_ref[...], s2_ref[...]
    sumsq = lambda x: jnp.sum(x * x, axis=-1, keepdims=True)
    inv_rms = lambda ss: lax.rsqrt(ss / MLA_QK + NORM_EPS)

    def rotary(x):
        return (x * cos + pltpu.roll(x, LANE - ROPE_FREQ, 1) * s1
                + pltpu.roll(x, ROPE_FREQ, 1) * s2)

    qg_n, qg_r = qg_ref[:, :MLA_NOPE], qg_ref[:, MLA_NOPE:]
    kg_n, kg_r = kg_ref[:, :MLA_NOPE], kg_ref[:, MLA_NOPE:]
    kr_ss = sumsq(k_rope)
    kr_rot = rotary(k_rope * kg_r)
    for h in range(heads):
        nope = slice(h * MLA_HEAD_PAD, h * MLA_HEAD_PAD + MLA_NOPE)
        rope = slice(h * MLA_HEAD_PAD + MLA_NOPE, (h + 1) * MLA_HEAD_PAD)
        qn, qr = q[:, nope], q[:, rope]
        q_inv = inv_rms(sumsq(qn) + sumsq(qr))
        q_out[:, nope] = (qn * q_inv * qg_n * q_scale).astype(BF16)
        q_out[:, rope] = (rotary(qr * q_inv * qg_r) * q_scale).astype(BF16)
        kn = kv[:, h * MLA_NOPE:(h + 1) * MLA_NOPE]
        k_inv = inv_rms(sumsq(kn) + kr_ss)
        k_out[:, nope] = (kn * k_inv * kg_n).astype(BF16)
        k_out[:, rope] = (kr_rot * k_inv).astype(BF16)
    v_out[...] = kv[:, heads * MLA_NOPE:].astype(BF16)


def _mla_prep(z, rw_cols, p, rope):
    t, n = z.shape
    heads = p["heads"]
    q_rank, kv_rank = p["q_up"].shape[0], p["kv_up"].shape[0]
    zw = min(w for w in range(LANE, n + 1, LANE) if n % w == 0 and n - w <= rw_cols)
    bm = _pick(t, (256, 128, 64))
    qw = heads * MLA_HEAD_PAD
    row = lambda w: pl.BlockSpec((bm, w), lambda i: (i, 0))
    const = lambda a: pl.BlockSpec(a.shape, lambda i: (0, 0))
    args = (z, p["q_norm"], p["q_up"], p["kv_norm"], p["kv_up"], p["q_gain"], p["k_gain"])
    return pl.pallas_call(
        functools.partial(_mla_prep_kernel, off=rw_cols - (n - zw), q_rank=q_rank, kv_rank=kv_rank,
                          heads=heads, q_scale=ATTN_Q_SCALE),
        out_shape=[jax.ShapeDtypeStruct((t, qw), BF16), jax.ShapeDtypeStruct((t, qw), BF16),
                   jax.ShapeDtypeStruct((t, heads * MLA_V_DIM), BF16)],
        grid=(t // bm,),
        in_specs=([pl.BlockSpec((bm, zw), lambda i: (i, n // zw - 1))]
                  + [const(a) for a in args[1:]] + [row(LANE)] * 3),
        out_specs=[row(qw), row(qw), row(heads * MLA_V_DIM)],
        compiler_params=_params(("parallel",)),
        name="mla_prep",
    )(*args, *rope)


def _attn_kernel(q_ref, k_ref, vt_ref, o_ref, *, n_lat, bq, hps, with_ctx):
    def attend(key0, n_keys):
        bk = _pick(n_keys, (2816, 1408, 768, 512, 256, 128))
        qcols = lambda c: slice(c * MLA_HEAD_PAD, (c + 1) * MLA_HEAD_PAD)
        q = [q_ref[:, qcols(c)] for c in range(hps)]
        keys = lambda j: slice(key0 + j * bk, key0 + (j + 1) * bk)
        scores = lambda j: [_dot_nt(k_ref[keys(j), qcols(c)], q[c]) for c in range(hps)]
        s_next = scores(0)
        m = [jnp.full((1, bq), -jnp.inf, F32)] * hps
        acc = [jnp.zeros((V_ROWS, bq), F32)] * hps
        for j in range(n_keys // bk):
            s = s_next
            if (j + 1) * bk < n_keys:
                s_next = scores(j + 1)
            for c in range(hps):
                m_new = jnp.maximum(m[c], jnp.max(s[c], axis=0, keepdims=True))
                p = jnp.exp2(s[c] - m_new).astype(BF16)
                acc[c] = (jnp.exp2(m[c] - m_new) * acc[c]
                          + _dot(vt_ref[c * V_ROWS:(c + 1) * V_ROWS, keys(j)], p))
                m[c] = m_new
        for c in range(hps):
            o_ref[:, c * MLA_V_DIM:(c + 1) * MLA_V_DIM] = (
                acc[c][:MLA_V_DIM] / acc[c][MLA_V_DIM:MLA_V_DIM + 1]).T.astype(o_ref.dtype)

    n_all = k_ref.shape[0]
    if not with_ctx:
        attend(0, n_all)
        return
    is_latent = pl.program_id(1) < n_lat // bq

    @pl.when(is_latent)
    def _():
        attend(0, n_all)

    @pl.when(jnp.logical_not(is_latent))
    def _():
        attend(n_lat, n_all - n_lat)


def _attention(q, k, vt, heads, n_lat, with_ctx):
    t = k.shape[0]
    n_q = t if with_ctx else n_lat
    bq = _pick(math.gcd(n_lat, t - n_lat), (256, 128, 64))
    hps = 4
    return pl.pallas_call(
        functools.partial(_attn_kernel, n_lat=n_lat, bq=bq, hps=hps, with_ctx=with_ctx),
        out_shape=jax.ShapeDtypeStruct((n_q, heads * MLA_V_DIM), BF16),
        grid=(heads // hps, n_q // bq),
        in_specs=[pl.BlockSpec((bq, hps * MLA_HEAD_PAD), lambda h, i: (i, h)),
                  _resident((t, hps * MLA_HEAD_PAD), lambda h, i: (0, h)),
                  _resident((hps * V_ROWS, t), lambda h, i: (h, 0))],
        out_specs=pl.BlockSpec((bq, hps * MLA_V_DIM), lambda h, i: (i, h)),
        compiler_params=_params(("parallel", "parallel")),
        name="attention",
    )(q, k, vt)


def _pad_cols(w, width):
    return jnp.pad(w, ((0, 0), (0, width - w.shape[1])))


def _rope_tables(n_lat, n_ctx):
    t = jnp.arange(n_lat, dtype=jnp.int32)
    pos = jnp.stack([t // GRID_W, t % GRID_W], axis=-1).astype(F32)
    inv_freq = ROPE_THETA ** (-jnp.arange(ROPE_FREQ, dtype=F32) / ROPE_FREQ)
    ang = pos[:, :, None] * inv_freq
    cos, sin = jnp.cos(ang), jnp.sin(ang)
    zero = jnp.zeros_like(sin)
    lay = lambda first, second: jnp.stack([first, second], axis=2).reshape(n_lat, MLA_ROPE)
    tabs = []
    for rope_part, fill in ((lay(cos, cos), 1.0), (lay(-sin, zero), 0.0), (lay(zero, sin), 0.0)):
        tab = jnp.concatenate([rope_part, jnp.full((n_lat, LANE - MLA_ROPE), fill, F32)], axis=1)
        tabs.append(jnp.concatenate([tab, jnp.full((n_ctx, LANE), fill, F32)], axis=0))
    return tuple(tabs)


def _layer_params(l, rw_mu, rw_w0, rw_w_up, rw_a0, rw_a_up, rw_g_up, rw_k_k, rw_k_a, rw_r_k,
                  rw_ln_w, rw_ln_b, mla_q_norm, mla_q_up, mla_kv_norm, mla_kv_up, mla_q_gain,
                  mla_k_gain):
    hw = rw_k_k.shape[1]
    ld, la, lg = rw_w_up.shape[2], rw_a_up.shape[2], rw_g_up.shape[1]
    q_rank, kv_rank = mla_q_up.shape[1], mla_kv_up.shape[1]
    heads = mla_kv_up.shape[2] // (MLA_NOPE + MLA_V_DIM)
    rw_cols = 3 * hw + ld + la + lg
    zw = -(-rw_cols // LANE) * LANE
    assert ld <= LANE and la <= LANE and rw_cols % 4 == 0 and 3 * hw + ld + LANE <= zw
    code = np.minimum(np.arange(zw, dtype=np.int32) // (rw_cols // 4), 3)[None, :]
    bd = (np.arange(LANE)[:, None] // RW_HEAD == np.arange(LANE)[None, :] // RW_HEAD)
    pad_rows = lambda x: jnp.pad(x, ((0, 0), (0, LANE - x.shape[1]), (0, 0)))
    return {
        "hw": hw, "heads": heads, "rw_cols": rw_cols, "zw": zw, "ld": ld, "la": la,
        "mu": _pad_cols(rw_mu[l][None, :], zw),
        "code": jnp.asarray(code),
        "k_k": rw_k_k[l][None, :], "k_a": rw_k_a[l][None, :], "r_k": rw_r_k[l].reshape(1, hw),
        "w0": rw_w0[l], "a0": rw_a0[l],
        "wup": pad_rows(rw_w_up[l]).astype(BF16), "aup": pad_rows(rw_a_up[l]).astype(BF16),
        "gup": rw_g_up[l].astype(BF16),
        "bd": jnp.asarray(bd, BF16),
        "ln_w": rw_ln_w[l][None, :], "ln_b": rw_ln_b[l][None, :],
        "q_norm": mla_q_norm[l][None, :], "kv_norm": mla_kv_norm[l][None, :],
        "q_up": jnp.pad(mla_q_up[l].reshape(q_rank, heads, MLA_QK),
                        ((0, 0), (0, 0), (0, MLA_HEAD_PAD - MLA_QK))
                        ).reshape(q_rank, heads * MLA_HEAD_PAD).astype(BF16),
        "kv_up": mla_kv_up[l].reshape(kv_rank, heads, 2, MLA_NOPE).transpose(0, 2, 1, 3)
                             .reshape(kv_rank, 2 * heads * MLA_NOPE).astype(BF16),
        "q_gain": _pad_cols(mla_q_gain[l][None, :], MLA_HEAD_PAD),
        "k_gain": _pad_cols(mla_k_gain[l][None, :], MLA_HEAD_PAD),
    }


def _ffn_half_step(xt, mod, g, ffn_w_in, ffn_w_out, lead, n_lat):
    h = _norm_mod(xt, g, mod[:, 0], mod[:, 1], n_lat)
    act, w_out_bf16 = _swiglu_matmul(h, ffn_w_in, lead, ffn_w_out)
    return _residual_matmul([act], w_out_bf16, xt, FFN_RES * mod[:, 2], n_lat)


def kernel(x, c, ctx, c_ctx, w_mod, b_mod, norm_g, ffn_w_in, ffn_w_out, w_in, w_out, rw_mu, rw_w0,
           rw_w_up, rw_a0, rw_a_up, rw_g_up, rw_k_k, rw_k_a, rw_r_k, rw_ln_w, rw_ln_b, mla_q_norm,
           mla_q_up, mla_kv_norm, mla_kv_up, mla_q_gain, mla_k_gain):
    batch, n_lat, d = x.shape
    n_ctx = ctx.shape[1]
    depth = w_mod.shape[0]
    assert batch == 1 and c.shape[0] == 1 and ctx.shape[0] == 1
    assert n_lat % GRID_W == 0 and n_ctx % GRID_W == 0

    xt = jnp.concatenate([x[0], ctx[0]], axis=0)
    cc = jnp.zeros((8, d), F32).at[0].set(c[0]).at[1].set(c_ctx)
    rope = _rope_tables(n_lat, n_ctx)

    for l in range(depth):
        last = l == depth - 1
        p = _layer_params(l, rw_mu, rw_w0, rw_w_up, rw_a0, rw_a_up, rw_g_up, rw_k_k, rw_k_a,
                          rw_r_k, rw_ln_w, rw_ln_b, mla_q_norm, mla_q_up, mla_kv_norm, mla_kv_up,
                          mla_q_gain, mla_k_gain)
        mod = _modulation(cc, w_mod, b_mod, l)[:2].reshape(2, N_MOD, d)

        xt = _ffn_half_step(xt, mod[:, 0:3], norm_g[l, 0], ffn_w_in, ffn_w_out, (l, 0), n_lat)

        hz = _norm_mod(xt, norm_g[l, 1], mod[:, 3], mod[:, 4], n_lat)
        z, w_out_bf16 = _matmul(hz, w_in, (l,), w_out, (l,))

        st = _rw_streams(z, p, n_lat)
        y_f = _wkv(st, 0, n_lat, reverse=False)
        o_rw = _wkv(st, 1, n_lat, reverse=True, y_other=y_f, p=p)

        q, k, v = _mla_prep(z, p["rw_cols"], p, rope)
        vt = jnp.concatenate([v.T.reshape(p["heads"], MLA_V_DIM, -1),
                              jnp.ones((p["heads"], V_ROWS - MLA_V_DIM, v.shape[0]), BF16)], axis=1)
        o_mla = _attention(q, k, vt.reshape(p["heads"] * V_ROWS, -1), p["heads"], n_lat,
                           with_ctx=not last)

        xt = _residual_matmul([o_rw, o_mla], w_out_bf16, xt, mod[:, 5], n_lat,
                              rows=n_lat if last else None)

        xt = _ffn_half_step(xt, mod[:, 6:9], norm_g[l, 2], ffn_w_in, ffn_w_out, (l, 1), n_lat)
    return xt[:n_lat][None]
```
